```python
import math
import jax, jax.numpy as jnp
from jax import lax
import numpy as np

D_MODEL = 1024
BATCH = 8
SEQ = 4096
DEPTH = 1

HEAD_DIM = 64
A_Q_HEADS = 8
A_KV_HEADS = 2
B_HEADS = 8
D_MIX = (A_Q_HEADS + B_HEADS) * HEAD_DIM
A_Q_W = A_Q_HEADS * HEAD_DIM
A_KV_W = A_KV_HEADS * HEAD_DIM
B_W = B_HEADS * HEAD_DIM
D_IN = A_Q_W + 2 * A_KV_W + 3 * B_W
WINDOW = 128
BLOCK = 128
ROT_DIMS = HEAD_DIM // 4
ROPE_THETA = 500000.0
GRID_W = 64
NA_ROWS_MAX = 8
NA_COLS = 16
N_EXPERTS = 32
TOP_K = 4
D_FF = 1024
SWIGLU_LIMIT = 7.0
SWIGLU_ALPHA = 1.702
MOE_BLOCK = 128
EPS = 1e-5
NEG = -1e30

kernel_name = "hybrid_swa_natten_moe_adaln"


def rms_norm(x):
    xf = x.astype(jnp.float32)
    return (xf * lax.rsqrt(jnp.mean(xf * xf, axis=-1, keepdims=True) + EPS)).astype(x.dtype)


def modulate(xn, shift, scale):
    return xn * (1.0 + scale[:, None, :]) + shift[:, None, :]


def apply_partial_rope(x, seq_len):
    half = ROT_DIMS // 2
    pos = jnp.arange(seq_len, dtype=jnp.float32)
    inv_freq = ROPE_THETA ** (-jnp.arange(0, ROT_DIMS, 2, dtype=jnp.float32) / ROT_DIMS)
    ang = pos[:, None] * inv_freq[None, :]
    cos = jnp.cos(ang)[None, :, None, :].astype(x.dtype)
    sin = jnp.sin(ang)[None, :, None, :].astype(x.dtype)
    x1 = x[..., :half]
    x2 = x[..., half:ROT_DIMS]
    return jnp.concatenate([x1 * cos - x2 * sin, x2 * cos + x1 * sin, x[..., ROT_DIMS:]], axis=-1)


def windowed_gqa_sink(q, k, v, sink):
    b, s, hq, d = q.shape
    hkv = k.shape[2]
    g = hq // hkv
    nb = s // BLOCK
    qb = q.reshape(b, nb, BLOCK, hkv, g, d)
    pad = ((0, 0), (BLOCK, BLOCK), (0, 0), (0, 0))
    kp = jnp.pad(k, pad).reshape(b, nb + 2, BLOCK, hkv, d)
    vp = jnp.pad(v, pad).reshape(b, nb + 2, BLOCK, hkv, d)
    kb = jnp.concatenate([kp[:, :-2], kp[:, 1:-1], kp[:, 2:]], axis=2)
    vb = jnp.concatenate([vp[:, :-2], vp[:, 1:-1], vp[:, 2:]], axis=2)
    qi = jnp.arange(BLOCK)
    kj = jnp.arange(3 * BLOCK)
    rel = kj[None, :] - BLOCK - qi[:, None]
    band = jnp.abs(rel) <= WINDOW
    kpos = jnp.arange(nb)[:, None] * BLOCK - BLOCK + kj[None, :]
    inb = (kpos >= 0) & (kpos < s)
    mask = band[None, :, :] & inb[:, None, :]
    scores = jnp.einsum('bnqhgd,bnkhd->bnhgqk', qb, kb).astype(jnp.float32) * (d ** -0.5)
    scores = jnp.where(mask[None, :, None, None], scores, NEG)
    sink_l = sink.astype(jnp.float32).reshape(hkv, g)[None, None, :, :, None, None]
    m = jnp.maximum(jnp.max(scores, axis=-1, keepdims=True), sink_l)
    p = jnp.exp(scores - m)
    denom = jnp.sum(p, axis=-1, keepdims=True) + jnp.exp(sink_l - m)
    out = jnp.einsum('bnhgqk,bnkhd->bnqhgd', (p / denom).astype(v.dtype), vb)
    return out.reshape(b, s, hq, d)


def neighbourhood_attn(q, k, v, rpb):
    b, s, h, d = q.shape
    rows = s // GRID_W
    kr = min(NA_ROWS_MAX, rows)
    kc = NA_COLS
    qg = q.reshape(b, rows, GRID_W, h, d)
    kg = k.reshape(b, rows, GRID_W, h, d)
    vg = v.reshape(b, rows, GRID_W, h, d)
    r = jnp.arange(rows)
    row_start = jnp.clip(r - kr // 2, 0, rows - kr)
    row_idx = row_start[:, None] + jnp.arange(kr)[None, :]
    kw = kg[:, row_idx].reshape(b, rows, kr * GRID_W, h, d)
    vw = vg[:, row_idx].reshape(b, rows, kr * GRID_W, h, d)
    col = jnp.arange(GRID_W)
    col_start = jnp.clip(col - kc // 2, 0, GRID_W - kc)
    col_mask = (col[None, :] >= col_start[:, None]) & (col[None, :] < col_start[:, None] + kc)
    mask = jnp.tile(col_mask, (1, kr))
    row_off = row_idx - r[:, None]
    col_off = jnp.clip(col[None, :] - col[:, None], -(kc - 1), kc - 1)
    bias = rpb[:, row_off[:, None, :, None] + (NA_ROWS_MAX - 1),
               col_off[None, :, None, :] + (NA_COLS - 1)]
    bias = bias.reshape(h, rows, GRID_W, kr * GRID_W).astype(jnp.float32)
    scores = jnp.einsum('brqhd,brkhd->bhrqk', qg, kw).astype(jnp.float32) * (d ** -0.5) + bias[None]
    scores = jnp.where(mask[None, None, None], scores, NEG)
    p = jax.nn.softmax(scores, axis=-1)
    out = jnp.einsum('bhrqk,brkhd->brqhd', p.astype(v.dtype), vw)
    return out.reshape(b, s, h, d)


def moe_ffn(h, w_router, b_router, w_gate_up, b_gate_up, w_down, b_down):
    b, s, d = h.shape
    t = b * s
    xt = h.reshape(t, d)
    logits = (xt @ w_router + b_router).astype(jnp.float32)
    top_val, top_idx = lax.top_k(logits, TOP_K)
    gates = jax.nn.softmax(top_val, axis=-1)
    a = t * TOP_K
    e_flat = top_idx.reshape(a)
    tok_flat = jnp.repeat(jnp.arange(t, dtype=jnp.int32), TOP_K)
    g_flat = gates.reshape(a)
    order = jnp.argsort(e_flat)
    e_sorted = e_flat[order]
    tok_sorted = tok_flat[order]
    g_sorted = g_flat[order]
    counts = jnp.bincount(e_flat, length=N_EXPERTS)
    starts = jnp.cumsum(counts) - counts
    padded = ((counts + MOE_BLOCK - 1) // MOE_BLOCK) * MOE_BLOCK
    pends = jnp.cumsum(padded)
    pstarts = pends - padded
    dest = pstarts[e_sorted] + (jnp.arange(a) - starts[e_sorted])
    n_slots = a + N_EXPERTS * MOE_BLOCK
    n_blk = n_slots // MOE_BLOCK
    slot_tok = jnp.full((n_slots,), t, dtype=jnp.int32).at[dest].set(tok_sorted)
    slot_gate = jnp.zeros((n_slots,), jnp.float32).at[dest].set(g_sorted)
    blk_expert = jnp.minimum(
        jnp.searchsorted(pends, jnp.arange(n_blk) * MOE_BLOCK, side='right'), N_EXPERTS - 1)
    x_pad = jnp.concatenate([xt, jnp.zeros((1, d), xt.dtype)], axis=0)
    xs = x_pad[slot_tok].reshape(n_blk, MOE_BLOCK, d)

    def expert_block(args):
        xb, e = args
        gu = xb @ w_gate_up[e] + b_gate_up[e]
        gate = jnp.minimum(gu[:, :D_FF], SWIGLU_LIMIT)
        up = jnp.clip(gu[:, D_FF:], -SWIGLU_LIMIT, SWIGLU_LIMIT)
        glu = gate * jax.nn.sigmoid(SWIGLU_ALPHA * gate)
        return ((up + 1.0) * glu) @ w_down[e] + b_down[e]

    ys = lax.map(expert_block, (xs, blk_expert)).reshape(n_slots, d)
    y = jax.ops.segment_sum(ys * slot_gate[:, None].astype(ys.dtype), slot_tok, num_segments=t + 1)[:t]
    return y.reshape(b, s, d)


def setup_inputs(seed: int = 0) -> dict:
    key = jax.random.key(seed)
    ks = jax.random.split(key, 20)
    f32 = jnp.float32
    nrm = lambda k, shape, sc: jax.random.normal(k, shape, f32) * sc
    return {
        "x": nrm(ks[0], (BATCH, SEQ, D_MODEL), 1.0),
        "c": nrm(ks[1], (BATCH, D_MODEL), 1.0),
        "w_ada": nrm(ks[2], (DEPTH, D_MODEL, 6 * D_MODEL), D_MODEL ** -0.5),
        "b_ada": nrm(ks[3], (DEPTH, 6 * D_MODEL), 0.02),
        "w_in": nrm(ks[4], (DEPTH, D_MODEL, D_IN), D_MODEL ** -0.5),
        "sink": nrm(ks[5], (DEPTH, A_Q_HEADS), 1.0),
        "rpb": nrm(ks[6], (DEPTH, B_HEADS, 2 * NA_ROWS_MAX - 1, 2 * NA_COLS - 1), 0.5),
        "g_out_a": 1.0 + nrm(ks[7], (DEPTH, A_Q_W), 0.02),
        "g_out_b": 1.0 + nrm(ks[8], (DEPTH, B_W), 0.02),
        "w_out": nrm(ks[9], (DEPTH, D_MIX, D_MODEL), D_MIX ** -0.5),
        "w_router": nrm(ks[10], (DEPTH, D_MODEL, N_EXPERTS), D_MODEL ** -0.5),
        "b_router": nrm(ks[11], (DEPTH, N_EXPERTS), 0.01),
        "w_gate_up": nrm(ks[12], (DEPTH, N_EXPERTS, D_MODEL, 2 * D_FF), D_MODEL ** -0.5),
        "b_gate_up": nrm(ks[13], (DEPTH, N_EXPERTS, 2 * D_FF), 0.01),
        "w_down": nrm(ks[14], (DEPTH, N_EXPERTS, D_FF, D_MODEL), D_FF ** -0.5),
        "b_down": nrm(ks[15], (DEPTH, N_EXPERTS, D_MODEL), 0.01),
        "g_final": 1.0 + nrm(ks[16], (D_MODEL,), 0.02),
    }


def reference(x, c, w_ada, b_ada, w_in, sink, rpb, g_out_a, g_out_b, w_out,
              w_router, b_router, w_gate_up, b_gate_up, w_down, b_down, g_final):
    b, s, _ = x.shape
    c_act = jax.nn.silu(c)
    for l in range(DEPTH):
        mod = c_act @ w_ada[l] + b_ada[l]
        shift_m, scale_m, gate_m, shift_f, scale_f, gate_f = jnp.split(mod, 6, axis=-1)

        h = modulate(rms_norm(x), shift_m, scale_m)
        proj = h @ w_in[l]
        qa, ka, va, qb, kb, vb = jnp.split(
            proj, np.cumsum([A_Q_W, A_KV_W, A_KV_W, B_W, B_W]).tolist(), axis=-1)
        qa = apply_partial_rope(qa.reshape(b, s, A_Q_HEADS, HEAD_DIM), s)
        ka = apply_partial_rope(ka.reshape(b, s, A_KV_HEADS, HEAD_DIM), s)
        va = va.reshape(b, s, A_KV_HEADS, HEAD_DIM)
        oa = windowed_gqa_sink(qa, ka, va, sink[l]).reshape(b, s, A_Q_W)
        ob = neighbourhood_attn(qb.reshape(b, s, B_HEADS, HEAD_DIM),
                                kb.reshape(b, s, B_HEADS, HEAD_DIM),
                                vb.reshape(b, s, B_HEADS, HEAD_DIM), rpb[l]).reshape(b, s, B_W)
        mixed = jnp.concatenate([rms_norm(oa) * g_out_a[l], rms_norm(ob) * g_out_b[l]], axis=-1)
        x = x + gate_m[:, None, :] * (mixed @ w_out[l])

        h = modulate(rms_norm(x), shift_f, scale_f)
        y = moe_ffn(h, w_router[l], b_router[l], w_gate_up[l], b_gate_up[l], w_down[l], b_down[l])
        x = x + gate_f[:, None, :] * y
    return rms_norm(x) * g_final
```

```python
import functools

import jax
import jax.numpy as jnp
from jax import lax
from jax.experimental import pallas as pl
from jax.experimental.pallas import tpu as pltpu

F32 = jnp.float32
BF16 = jnp.bfloat16
I32 = jnp.int32

HEAD_DIM = 64
A_Q_HEADS = 8
A_KV_HEADS = 2
A_GROUP = A_Q_HEADS // A_KV_HEADS
B_HEADS = 8
A_Q_W = A_Q_HEADS * HEAD_DIM
A_KV_W = A_KV_HEADS * HEAD_DIM
B_W = B_HEADS * HEAD_DIM
WINDOW = 128
ROT_DIMS = HEAD_DIM // 4
ROPE_THETA = 500000.0
GRID_W = 64
NA_ROWS_MAX = 8
NA_COLS = 16
N_EXPERTS = 32
TOP_K = 4
SWIGLU_LIMIT = 7.0
SWIGLU_ALPHA = 1.702
EPS = 1e-5
NEG = -1e30

LANES = 128
PROJ_BLK = 512
TM_PROJ = 512
TM_COMB = 256
MOE_BM = 256
VMEM_LIMIT = 48 << 20


def _cparams(sem):
    return pltpu.CompilerParams(dimension_semantics=sem, vmem_limit_bytes=VMEM_LIMIT)


def _rms(x):
    return x * lax.rsqrt(jnp.mean(x * x, axis=-1, keepdims=True) + EPS)


def _ada_body(c_ref, w_ref, b_ref, o_ref):
    c = c_ref[...]
    ca = c / (1.0 + jnp.exp(-c))
    o_ref[...] = jnp.dot(ca, w_ref[...], preferred_element_type=F32,
                         precision=lax.Precision.HIGHEST) + b_ref[...]


def _ada(c, w, b):
    bsz, d = c.shape
    n = w.shape[1]
    bn = n // 4
    return pl.pallas_call(
        _ada_body,
        out_shape=jax.ShapeDtypeStruct((bsz, n), F32),
        grid=(4,),
        in_specs=[pl.BlockSpec((bsz, d), lambda i: (0, 0)),
                  pl.BlockSpec((d, bn), lambda i: (0, i)),
                  pl.BlockSpec((1, bn), lambda i: (0, i))],
        out_specs=pl.BlockSpec((bsz, bn), lambda i: (0, i)),
        compiler_params=_cparams(("arbitrary",)),
        name="ada",
    )(c, w, b)


def _inproj_body(x_ref, sh_ref, sc_ref, w_ref, cos_ref, sa_ref, sb_ref, o_ref):
    h = (_rms(x_ref[...]) * (1.0 + sc_ref[0]) + sh_ref[0]).astype(BF16)
    cos = cos_ref[...]
    sa = sa_ref[...]
    sb = sb_ref[...]
    n_blk = w_ref.shape[1] // PROJ_BLK
    for cb in range(n_blk):
        p = jnp.dot(h, w_ref[:, cb * PROJ_BLK:(cb + 1) * PROJ_BLK], preferred_element_type=F32)
        if cb < 2:
            parts = []
            for j in range(PROJ_BLK // LANES):
                pj = p[:, j * LANES:(j + 1) * LANES]
                parts.append(pj * cos + pltpu.roll(pj, LANES - ROT_DIMS // 2, axis=1) * sa
                             + pltpu.roll(pj, ROT_DIMS // 2, axis=1) * sb)
            p = jnp.concatenate(parts, axis=1)
        o_ref[:, cb * PROJ_BLK:(cb + 1) * PROJ_BLK] = p.astype(BF16)


def _inproj(x2, shift, scale, w_ext, cos_t, sa_t, sb_t, seq):
    t, d = x2.shape
    n = w_ext.shape[1]
    tm = TM_PROJ
    tpb = seq // tm
    return pl.pallas_call(
        _inproj_body,
        out_shape=jax.ShapeDtypeStruct((t, n), BF16),
        grid=(t // tm,),
        in_specs=[pl.BlockSpec((tm, d), lambda i: (i, 0)),
                  pl.BlockSpec((1, 1, d), lambda i: (i // tpb, 0, 0)),
                  pl.BlockSpec((1, 1, d), lambda i: (i // tpb, 0, 0)),
                  pl.BlockSpec((d, n), lambda i: (0, 0)),
                  pl.BlockSpec((tm, LANES), lambda i: (i % tpb, 0)),
                  pl.BlockSpec((tm, LANES), lambda i: (i % tpb, 0)),
                  pl.BlockSpec((tm, LANES), lambda i: (i % tpb, 0))],
        out_specs=pl.BlockSpec((tm, n), lambda i: (i, 0)),
        compiler_params=_cparams(("arbitrary",)),
        name="inproj",
    )(x2, shift, scale, w_ext, cos_t, sa_t, sb_t)


def _swa_body(sink_ref, q_ref, k_ref, v_ref, g_ref, o_ref, *, seq):
    n = pl.program_id(1)
    blk = WINDOW
    kw_len = 3 * blk
    start = pl.multiple_of(jnp.clip(n * blk - blk, 0, seq - kw_len), blk)
    kw = k_ref[pl.ds(start, kw_len), :]
    vw = v_ref[pl.ds(start, kw_len), :]
    q = q_ref[...]
    qpos = n * blk + lax.broadcasted_iota(I32, (blk, 1), 0)
    kpos = start + lax.broadcasted_iota(I32, (1, kw_len), 1)
    allowed = jnp.abs(qpos - kpos) <= WINDOW
    gw = A_GROUP * HEAD_DIM
    grp = lax.broadcasted_iota(I32, (1, gw), 1) // HEAD_DIM
    outs = []
    for h in range(A_KV_HEADS):
        qh = q[:, h * gw:(h + 1) * gw]
        kh = kw[:, h * gw:(h + 1) * gw]
        vh = vw[:, h * gw:(h + 1) * gw]
        acc = jnp.zeros((blk, gw), F32)
        for g in range(A_GROUP):
            qg = jnp.where(grp == g, qh, jnp.zeros_like(qh))
            s = lax.dot_general(qg, kh, (((1,), (1,)), ((), ())), preferred_element_type=F32)
            s = jnp.where(allowed, s, NEG)
            sk = sink_ref[h * A_GROUP + g]
            m = jnp.maximum(jnp.max(s, axis=-1, keepdims=True), sk)
            p = jnp.exp(s - m)
            denom = jnp.sum(p, axis=-1, keepdims=True) + jnp.exp(sk - m)
            pn = (p / denom).astype(BF16)
            of = jnp.dot(pn, vh, preferred_element_type=F32)
            acc = jnp.where(grp == g, of, acc)
        outs.append(acc)
    oa = jnp.concatenate(outs, axis=1)
    o_ref[...] = (_rms(oa) * g_ref[...]).astype(BF16)


def _swa(proj, sink, g_a, bsz, seq):
    t = proj.shape[0]
    blk = WINDOW
    nq = seq // blk
    grid_spec = pltpu.PrefetchScalarGridSpec(
        num_scalar_prefetch=1, grid=(bsz, nq),
        in_specs=[pl.BlockSpec((blk, PROJ_BLK), lambda b, n, s: (b * nq + n, 0)),
                  pl.BlockSpec((seq, PROJ_BLK), lambda b, n, s: (b, 1)),
                  pl.BlockSpec((seq, PROJ_BLK), lambda b, n, s: (b, 2)),
                  pl.BlockSpec((1, A_Q_W), lambda b, n, s: (0, 0))],
        out_specs=pl.BlockSpec((blk, A_Q_W), lambda b, n, s: (b * nq + n, 0)))
    return pl.pallas_call(
        functools.partial(_swa_body, seq=seq),
        out_shape=jax.ShapeDtypeStruct((t, A_Q_W), BF16),
        grid_spec=grid_spec,
        compiler_params=_cparams(("arbitrary", "arbitrary")),
        name="swa",
    )(sink, proj, proj, proj, g_a)


def _na_body(q_ref, k_ref, v_ref, bias_ref, g_ref, o_ref, *, rows, kr):
    r = pl.program_id(1)
    rs = jnp.clip(r - kr // 2, 0, rows - kr)
    start = pl.multiple_of(rs * GRID_W, GRID_W)
    kw = k_ref[pl.ds(start, kr * GRID_W), :]
    vw = v_ref[pl.ds(start, kr * GRID_W), :]
    q = q_ref[...]
    half = lax.broadcasted_iota(I32, (1, LANES), 1) // HEAD_DIM
    outs = []
    for pr in range(B_W // LANES):
        qp = q[:, pr * LANES:(pr + 1) * LANES]
        kp = kw[:, pr * LANES:(pr + 1) * LANES]
        vp = vw[:, pr * LANES:(pr + 1) * LANES]
        op = jnp.zeros((GRID_W, LANES), F32)
        for j in range(LANES // HEAD_DIM):
            hh = pr * (LANES // HEAD_DIM) + j
            qm = jnp.where(half == j, qp, jnp.zeros_like(qp))
            s = lax.dot_general(qm, kp, (((1,), (1,)), ((), ())), preferred_element_type=F32)
            s = s + bias_ref[hh, 0]
            m = jnp.max(s, axis=-1, keepdims=True)
            p = jnp.exp(s - m)
            pn = (p / jnp.sum(p, axis=-1, keepdims=True)).astype(BF16)
            of = jnp.dot(pn, vp, preferred_element_type=F32)
            op = jnp.where(half == j, of, op)
        outs.append(op)
    ob = jnp.concatenate(outs, axis=1)
    o_ref[...] = (_rms(ob) * g_ref[...]).astype(BF16)


def _natten(proj, bias_t, g_b, bsz, seq):
    t = proj.shape[0]
    rows = seq // GRID_W
    kr = min(NA_ROWS_MAX, rows)

    def bias_idx(b, r):
        return (0, r - jnp.clip(r - kr // 2, 0, rows - kr), 0, 0)

    return pl.pallas_call(
        functools.partial(_na_body, rows=rows, kr=kr),
        out_shape=jax.ShapeDtypeStruct((t, B_W), BF16),
        grid=(bsz, rows),
        in_specs=[pl.BlockSpec((GRID_W, PROJ_BLK), lambda b, r: (b * rows + r, 3)),
                  pl.BlockSpec((seq, PROJ_BLK), lambda b, r: (b, 4)),
                  pl.BlockSpec((seq, PROJ_BLK), lambda b, r: (b, 5)),
                  pl.BlockSpec((B_HEADS, 1, GRID_W, kr * GRID_W), bias_idx),
                  pl.BlockSpec((1, B_W), lambda b, r: (0, 0))],
        out_specs=pl.BlockSpec((GRID_W, B_W), lambda b, r: (b * rows + r, 0)),
        compiler_params=_cparams(("arbitrary", "arbitrary")),
        name="natten",
    )(proj, proj, proj, bias_t, g_b)


def _na_bias_table(rpb, rows):
    kr = min(NA_ROWS_MAX, rows)
    kc = NA_COLS
    col = jnp.arange(GRID_W)
    col_start = jnp.clip(col - kc // 2, 0, GRID_W - kc)
    col_mask = (col[None, :] >= col_start[:, None]) & (col[None, :] < col_start[:, None] + kc)
    col_off = jnp.clip(col[None, :] - col[:, None], -(kc - 1), kc - 1) + (NA_COLS - 1)
    case = jnp.arange(kr)
    row_off = jnp.arange(kr)[None, :] - case[:, None] + (NA_ROWS_MAX - 1)
    tbl = rpb[:, row_off[:, None, :, None], col_off[None, :, None, :]]
    tbl = jnp.where(col_mask[None, None, :, None, :], tbl, NEG)
    return tbl.reshape(rpb.shape[0], kr, GRID_W, kr * GRID_W).astype(F32)


def _outproj_body(x_ref, ma_ref, mb_ref, wo_ref, gm_ref, sf_ref, cf_ref, wr_ref, br_ref,
                  x1_ref, h2_ref, ri_ref, gt_ref, cnt_ref, carry):
    i = pl.program_id(0)

    @pl.when(i == 0)
    def _():
        carry[...] = jnp.zeros_like(carry)

    tm = x_ref.shape[0]
    mix = (jnp.dot(ma_ref[...], wo_ref[:A_Q_W, :], preferred_element_type=F32)
           + jnp.dot(mb_ref[...], wo_ref[A_Q_W:, :], preferred_element_type=F32))
    x1 = x_ref[...] + gm_ref[0] * mix
    x1_ref[...] = x1
    h = _rms(x1) * (1.0 + cf_ref[0]) + sf_ref[0]
    h2_ref[...] = h
    logits = jnp.dot(h, wr_ref[...], preferred_element_type=F32,
                     precision=lax.Precision.HIGHEST) + br_ref[...]
    lane = lax.broadcasted_iota(I32, (tm, LANES), 1)
    work = logits
    vals, idxs = [], []
    for _ in range(TOP_K):
        m = jnp.max(work, axis=-1, keepdims=True)
        ik = jnp.min(jnp.where(work == m, lane, LANES), axis=-1, keepdims=True)
        vals.append(m)
        idxs.append(ik)
        work = jnp.where(lane == ik, -jnp.inf, work)
    es = [jnp.exp(v - vals[0]) for v in vals]
    den = es[0] + es[1] + es[2] + es[3]
    mh = jnp.zeros((tm, LANES), F32)
    for ik in idxs:
        mh = mh + (lane == ik).astype(F32)
    tri = (lax.broadcasted_iota(I32, (tm, tm), 0) > lax.broadcasted_iota(I32, (tm, tm), 1))
    cnt = jnp.dot(tri.astype(BF16), mh.astype(BF16), preferred_element_type=F32) + carry[...]
    ri = jnp.zeros((tm, LANES), I32)
    gt = jnp.zeros((tm, LANES), F32)
    for k in range(TOP_K):
        rank = jnp.sum(jnp.where(lane == idxs[k], cnt, 0.0), axis=-1, keepdims=True)
        ri = jnp.where(lane == k, idxs[k], ri)
        ri = jnp.where(lane == TOP_K + k, rank.astype(I32), ri)
        gt = jnp.where(lane == k, es[k] / den, gt)
    ri_ref[...] = ri
    gt_ref[...] = gt
    carry[...] = carry[...] + jnp.sum(mh, axis=0, keepdims=True)
    cnt_ref[...] = carry[...]


def _outproj(x2, mix_a, mix_b, w_out, gate_m, shift_f, scale_f, w_r, b_r, seq):
    t, d = x2.shape
    tm = TM_PROJ
    tpb = seq // tm
    mod_spec = pl.BlockSpec((1, 1, d), lambda i: (i // tpb, 0, 0))
    return pl.pallas_call(
        _outproj_body,
        out_shape=(jax.ShapeDtypeStruct((t, d), F32), jax.ShapeDtypeStruct((t, d), F32),
                   jax.ShapeDtypeStruct((t, LANES), I32), jax.ShapeDtypeStruct((t, LANES), F32),
                   jax.ShapeDtypeStruct((1, LANES), F32)),
        grid=(t // tm,),
        in_specs=[pl.BlockSpec((tm, d), lambda i: (i, 0)),
                  pl.BlockSpec((tm, A_Q_W), lambda i: (i, 0)),
                  pl.BlockSpec((tm, B_W), lambda i: (i, 0)),
                  pl.BlockSpec((A_Q_W + B_W, d), lambda i: (0, 0)),
                  mod_spec, mod_spec, mod_spec,
                  pl.BlockSpec((d, LANES), lambda i: (0, 0)),
                  pl.BlockSpec((1, LANES), lambda i: (0, 0))],
        out_specs=(pl.BlockSpec((tm, d), lambda i: (i, 0)),
                   pl.BlockSpec((tm, d), lambda i: (i, 0)),
                   pl.BlockSpec((tm, LANES), lambda i: (i, 0)),
                   pl.BlockSpec((tm, LANES), lambda i: (i, 0)),
                   pl.BlockSpec((1, LANES), lambda i: (0, 0))),
        scratch_shapes=[pltpu.VMEM((1, LANES), F32)],
        compiler_params=_cparams(("arbitrary",)),
        name="outproj",
    )(x2, mix_a, mix_b, w_out, gate_m, shift_f, scale_f, w_r, b_r)


def _experts_body(be_ref, tok0_ref, tokn_ref, dstp_ref, dstc_ref,
                  h_hbm, wgu_ref, bgu_ref, wd_ref, bd_ref, out_hbm,
                  xbuf, ybuf, gsem, ssem, *, n_real, n_blk):
    bm = MOE_BM
    d_ff = wd_ref.shape[1]
    j = pl.program_id(0)
    slot = j % 2
    other = 1 - slot

    def gather_copy(t, i, s):
        return pltpu.make_async_copy(h_hbm.at[pl.ds(t, 1), :], xbuf.at[s, pl.ds(i, 1), :], gsem.at[s])

    def scatter_copy(i, dst, s):
        return pltpu.make_async_copy(ybuf.at[s, pl.ds(i, 1), :], out_hbm.at[pl.ds(dst, 1), :], ssem.at[s])

    def issue_gather(tok_ref, s):
        for i in range(bm):
            gather_copy(tok_ref[0, 0, i], i, s).start()

    def wait_gather(s):
        pltpu.make_async_copy(h_hbm.at[pl.ds(0, bm), :], xbuf.at[s], gsem.at[s]).wait()

    def wait_scatter(s):
        pltpu.make_async_copy(ybuf.at[s], out_hbm.at[pl.ds(0, bm), :], ssem.at[s]).wait()

    @pl.when(j == 0)
    def _():
        ybuf[...] = jnp.zeros_like(ybuf)
        for s in range(2):
            init = pltpu.make_async_copy(ybuf.at[s], out_hbm.at[pl.ds(n_real + s * bm, bm), :], ssem.at[s])
            init.start()
            init.wait()
        issue_gather(tok0_ref, 0)

    wait_gather(slot)

    @pl.when(j >= 1)
    def _():
        wait_scatter(slot)

    issue_gather(tokn_ref, other)
    for i in range(bm):
        dst = jnp.where(j == 0, n_real + bm + i, dstp_ref[0, 0, i])
        scatter_copy(i, dst, other).start()

    xs = xbuf[slot].astype(BF16)
    acts = []
    n_chunk = 4
    cw = d_ff // n_chunk
    for c in range(n_chunk):
        gate = jnp.dot(xs, wgu_ref[0, :, c * cw:(c + 1) * cw], preferred_element_type=F32)
        gate = gate + bgu_ref[0, :, c * cw:(c + 1) * cw]
        up = jnp.dot(xs, wgu_ref[0, :, d_ff + c * cw:d_ff + (c + 1) * cw], preferred_element_type=F32)
        up = up + bgu_ref[0, :, d_ff + c * cw:d_ff + (c + 1) * cw]
        gate = jnp.minimum(gate, SWIGLU_LIMIT)
        up = jnp.clip(up, -SWIGLU_LIMIT, SWIGLU_LIMIT)
        glu = gate / (1.0 + jnp.exp(-SWIGLU_ALPHA * gate))
        acts.append(((up + 1.0) * glu).astype(BF16))
    act = jnp.concatenate(acts, axis=1)
    ybuf[slot] = jnp.dot(act, wd_ref[0], preferred_element_type=F32) + bd_ref[0]

    @pl.when(j == n_blk - 1)
    def _():
        for i in range(bm):
            scatter_copy(i, dstc_ref[0, 0, i], slot).start()
        wait_scatter(slot)
        wait_scatter(other)
        wait_gather(other)


def _experts(blk_expert, slot_tok, slot_dst, h2, wgu, bgu, wd, bd, n_real):
    bm = MOE_BM
    n_blk = slot_tok.shape[0]
    d = h2.shape[1]
    d_ff = wd.shape[1]
    smem = functools.partial(pl.BlockSpec, (1, 1, bm), memory_space=pltpu.SMEM)
    last = n_blk - 1
    grid_spec = pltpu.PrefetchScalarGridSpec(
        num_scalar_prefetch=1, grid=(n_blk,),
        in_specs=[smem(lambda j, be: (j, 0, 0)),
                  smem(lambda j, be: (jnp.minimum(j + 1, last), 0, 0)),
                  smem(lambda j, be: (jnp.maximum(j - 1, 0), 0, 0)),
                  smem(lambda j, be: (j, 0, 0)),
                  pl.BlockSpec(memory_space=pl.ANY),
                  pl.BlockSpec((1, d, 2 * d_ff), lambda j, be: (be[j], 0, 0)),
                  pl.BlockSpec((1, 1, 2 * d_ff), lambda j, be: (be[j], 0, 0)),
                  pl.BlockSpec((1, d_ff, d), lambda j, be: (be[j], 0, 0)),
                  pl.BlockSpec((1, 1, d), lambda j, be: (be[j], 0, 0))],
        out_specs=pl.BlockSpec(memory_space=pl.ANY),
        scratch_shapes=[pltpu.VMEM((2, bm, d), F32), pltpu.VMEM((2, bm, d), F32),
                        pltpu.SemaphoreType.DMA((2,)), pltpu.SemaphoreType.DMA((2,))])
    return pl.pallas_call(
        functools.partial(_experts_body, n_real=n_real, n_blk=n_blk),
        out_shape=jax.ShapeDtypeStruct((n_real + 2 * bm, d), F32),
        grid_spec=grid_spec,
        compiler_params=_cparams(("arbitrary",)),
        name="experts",
    )(blk_expert, slot_tok, slot_tok, slot_dst, slot_dst, h2, wgu, bgu, wd, bd)


def _combine_body(x1_ref, y0_ref, y1_ref, y2_ref, y3_ref, gt_ref, gf_ref, gfin_ref, o_ref):
    gt = gt_ref[...]
    y = (gt[:, 0:1] * y0_ref[...] + gt[:, 1:2] * y1_ref[...]
         + gt[:, 2:3] * y2_ref[...] + gt[:, 3:4] * y3_ref[...])
    x2 = x1_ref[...] + gf_ref[0] * y
    o_ref[...] = _rms(x2) * gfin_ref[...]


def _combine(x1, ys, gt, gate_f, g_final, seq):
    t, d = x1.shape
    tm = TM_COMB
    tpb = seq // tm
    nt = t // tm
    y_specs = [pl.BlockSpec((tm, d), functools.partial(lambda i, k: (k * nt + i, 0), k=k))
               for k in range(TOP_K)]
    return pl.pallas_call(
        _combine_body,
        out_shape=jax.ShapeDtypeStruct((t, d), F32),
        grid=(nt,),
        in_specs=[pl.BlockSpec((tm, d), lambda i: (i, 0))] + y_specs + [
            pl.BlockSpec((tm, LANES), lambda i: (i, 0)),
            pl.BlockSpec((1, 1, d), lambda i: (i // tpb, 0, 0)),
            pl.BlockSpec((1, d), lambda i: (0, 0))],
        out_specs=pl.BlockSpec((tm, d), lambda i: (i, 0)),
        compiler_params=_cparams(("arbitrary",)),
        name="combine",
    )(x1, ys, ys, ys, ys, gt, gate_f, g_final)


def _rope_tables(seq):
    half = ROT_DIMS // 2
    pos = jnp.arange(seq, dtype=F32)
    inv_freq = ROPE_THETA ** (-jnp.arange(0, ROT_DIMS, 2, dtype=F32) / ROT_DIMS)
    ang = pos[:, None] * inv_freq[None, :]
    cos, sin = jnp.cos(ang), jnp.sin(ang)
    ones = jnp.ones((seq, HEAD_DIM - ROT_DIMS), F32)
    zeros = jnp.zeros((seq, HEAD_DIM - ROT_DIMS), F32)
    zh = jnp.zeros((seq, half), F32)
    rep = LANES // HEAD_DIM
    cos_t = jnp.tile(jnp.concatenate([cos, cos, ones], axis=1), (1, rep))
    sa_t = jnp.tile(jnp.concatenate([-sin, zh, zeros], axis=1), (1, rep))
    sb_t = jnp.tile(jnp.concatenate([zh, sin, zeros], axis=1), (1, rep))
    return cos_t, sa_t, sb_t


def _widen_in_proj(w_in):
    d = w_in.shape[0]
    scale = HEAD_DIM ** -0.5
    o = 0
    wqa = w_in[:, o:o + A_Q_W] * scale; o += A_Q_W
    wka = w_in[:, o:o + A_KV_W]; o += A_KV_W
    wva = w_in[:, o:o + A_KV_W]; o += A_KV_W
    wqb = w_in[:, o:o + B_W] * scale; o += B_W
    wkb = w_in[:, o:o + B_W]; o += B_W
    wvb = w_in[:, o:o + B_W]

    def widen(w):
        w = w.reshape(d, A_KV_HEADS, 1, HEAD_DIM)
        return jnp.broadcast_to(w, (d, A_KV_HEADS, A_GROUP, HEAD_DIM)).reshape(d, A_Q_W)

    return jnp.concatenate([wqa, widen(wka), widen(wva), wqb, wkb, wvb], axis=1).astype(BF16)


def kernel(x, c, w_ada, b_ada, w_in, sink, rpb, g_out_a, g_out_b, w_out, w_router, b_router,
           w_gate_up, b_gate_up, w_down, b_down, g_final):
    bsz, seq, d = x.shape
    t = bsz * seq
    depth = w_ada.shape[0]
    assert depth == 1, "the final norm is fused into the layer combine"
    bm = MOE_BM
    n_assign = t * TOP_K
    n_slots = n_assign + N_EXPERTS * bm
    n_blk = n_slots // bm

    cos_t, sa_t, sb_t = _rope_tables(seq)
    x2 = x.reshape(t, d)
    for l in range(depth):
        mod = _ada(c, w_ada[l], b_ada[l][None, :])
        shift_m, scale_m, gate_m, shift_f, scale_f, gate_f = [
            m.reshape(bsz, 1, d) for m in jnp.split(mod, 6, axis=-1)]

        proj = _inproj(x2, shift_m, scale_m, _widen_in_proj(w_in[l]), cos_t, sa_t, sb_t, seq)
        mix_a = _swa(proj, sink[l], g_out_a[l][None, :], bsz, seq)
        mix_b = _natten(proj, _na_bias_table(rpb[l], seq // GRID_W), g_out_b[l][None, :], bsz, seq)

        w_r = jnp.pad(w_router[l], ((0, 0), (0, LANES - N_EXPERTS)))
        b_r = jnp.pad(b_router[l], (0, LANES - N_EXPERTS), constant_values=NEG)[None, :]
        x1, h2, ri, gt, cnt = _outproj(x2, mix_a, mix_b, w_out[l].astype(BF16),
                                       gate_m, shift_f, scale_f, w_r, b_r, seq)

        top_idx = ri[:, :TOP_K]
        rank = ri[:, TOP_K:2 * TOP_K]
        counts = cnt[0, :N_EXPERTS].astype(I32)
        padded = ((counts + bm - 1) // bm) * bm
        pends = jnp.cumsum(padded)
        pstarts = pends - padded
        onehot = top_idx[:, :, None] == jnp.arange(N_EXPERTS, dtype=I32)[None, None, :]
        dest = jnp.sum(jnp.where(onehot, pstarts[None, None, :], 0), axis=-1) + rank
        slot_a = jnp.full((n_slots,), -1, I32).at[dest.T.reshape(-1)].set(
            jnp.arange(n_assign, dtype=I32), unique_indices=True)
        sid = jnp.arange(n_slots, dtype=I32)
        slot_tok = jnp.where(slot_a >= 0, slot_a % t, 0).reshape(n_blk, 1, bm)
        slot_dst = jnp.where(slot_a >= 0, slot_a, n_assign + sid % (2 * bm)).reshape(n_blk, 1, bm)
        blk_expert = jnp.minimum(
            jnp.searchsorted(pends, jnp.arange(n_blk, dtype=I32) * bm, side='right'),
            N_EXPERTS - 1).astype(I32)

        ys = _experts(blk_expert, slot_tok, slot_dst, h2,
                      w_gate_up[l].astype(BF16), b_gate_up[l][:, None, :],
                      w_down[l].astype(BF16), b_down[l][:, None, :], n_assign)
        x2 = _combine(x1, ys, gt, gate_f, g_final[None, :], seq)
    return x2.reshape(bsz, seq, d)
```

```python
import functools

import jax
import jax.numpy as jnp
from jax import lax
from jax.experimental import pallas as pl
from jax.experimental.pallas import tpu as pltpu

F32 = jnp.float32
BF16 = jnp.bfloat16
I32 = jnp.int32

HEAD_DIM = 64
A_Q_HEADS = 8
A_KV_HEADS = 2
A_GROUP = A_Q_HEADS // A_KV_HEADS
B_HEADS = 8
A_Q_W = A_Q_HEADS * HEAD_DIM
A_KV_W = A_KV_HEADS * HEAD_DIM
B_W = B_HEADS * HEAD_DIM
WINDOW = 128
ROT_DIMS = HEAD_DIM // 4
ROPE_THETA = 500000.0
GRID_W = 64
NA_ROWS_MAX = 8
NA_COLS = 16
N_EXPERTS = 32
TOP_K = 4
SWIGLU_LIMIT = 7.0
SWIGLU_ALPHA = 1.702
EPS = 1e-5
NEG = -1e30

LANES = 128
ROW_TILE = 8
PROJ_BLK = 512
TM_PROJ = 512
TM_COMB = 256
MOE_BM = 256
VMEM_LIMIT = 48 << 20


def _cparams(sem):
    return pltpu.CompilerParams(dimension_semantics=sem, vmem_limit_bytes=VMEM_LIMIT)


def _rms(x):
    return x * lax.rsqrt(jnp.mean(x * x, axis=-1, keepdims=True) + EPS)


def _ada_body(c_ref, w_ref, b_ref, o_ref):
    c = c_ref[...]
    ca = c / (1.0 + jnp.exp(-c))
    o_ref[...] = jnp.dot(ca, w_ref[...], preferred_element_type=F32,
                         precision=lax.Precision.HIGHEST) + b_ref[...]


def _ada(c, w, b):
    bsz, d = c.shape
    n = w.shape[1]
    bn = n // 4
    return pl.pallas_call(
        _ada_body,
        out_shape=jax.ShapeDtypeStruct((bsz, n), F32),
        grid=(4,),
        in_specs=[pl.BlockSpec((bsz, d), lambda i: (0, 0)),
                  pl.BlockSpec((d, bn), lambda i: (0, i)),
                  pl.BlockSpec((1, bn), lambda i: (0, i))],
        out_specs=pl.BlockSpec((bsz, bn), lambda i: (0, i)),
        compiler_params=_cparams(("arbitrary",)),
        name="ada",
    )(c, w, b)


def _inproj_body(x_ref, sh_ref, sc_ref, w_ref, cos_ref, sa_ref, sb_ref, o_ref):
    h = (_rms(x_ref[...]) * (1.0 + sc_ref[0]) + sh_ref[0]).astype(BF16)
    cos = cos_ref[...]
    sa = sa_ref[...]
    sb = sb_ref[...]
    n_blk = w_ref.shape[1] // PROJ_BLK
    for cb in range(n_blk):
        p = jnp.dot(h, w_ref[:, cb * PROJ_BLK:(cb + 1) * PROJ_BLK], preferred_element_type=F32)
        if cb < 2:
            parts = []
            for j in range(PROJ_BLK // LANES):
                pj = p[:, j * LANES:(j + 1) * LANES]
                parts.append(pj * cos + pltpu.roll(pj, LANES - ROT_DIMS // 2, axis=1) * sa
                             + pltpu.roll(pj, ROT_DIMS // 2, axis=1) * sb)
            p = jnp.concatenate(parts, axis=1)
        o_ref[:, cb * PROJ_BLK:(cb + 1) * PROJ_BLK] = p.astype(BF16)


def _inproj(x2, shift, scale, w_ext, cos_t, sa_t, sb_t, seq):
    t, d = x2.shape
    n = w_ext.shape[1]
    tm = TM_PROJ
    tpb = seq // tm
    return pl.pallas_call(
        _inproj_body,
        out_shape=jax.ShapeDtypeStruct((t, n), BF16),
        grid=(t // tm,),
        in_specs=[pl.BlockSpec((tm, d), lambda i: (i, 0)),
                  pl.BlockSpec((1, 1, d), lambda i: (i // tpb, 0, 0)),
                  pl.BlockSpec((1, 1, d), lambda i: (i // tpb, 0, 0)),
                  pl.BlockSpec((d, n), lambda i: (0, 0)),
                  pl.BlockSpec((tm, LANES), lambda i: (i % tpb, 0)),
                  pl.BlockSpec((tm, LANES), lambda i: (i % tpb, 0)),
                  pl.BlockSpec((tm, LANES), lambda i: (i % tpb, 0))],
        out_specs=pl.BlockSpec((tm, n), lambda i: (i, 0)),
        compiler_params=_cparams(("arbitrary",)),
        name="inproj",
    )(x2, shift, scale, w_ext, cos_t, sa_t, sb_t)


def _swa_body(sink_ref, q_ref, k_ref, v_ref, g_ref, o_ref, *, seq):
    n = pl.program_id(1)
    blk = WINDOW
    kw_len = 3 * blk
    start = pl.multiple_of(jnp.clip(n * blk - blk, 0, seq - kw_len), blk)
    kw = k_ref[pl.ds(start, kw_len), :]
    vw = v_ref[pl.ds(start, kw_len), :]
    q = q_ref[...]
    qpos = n * blk + lax.broadcasted_iota(I32, (blk, 1), 0)
    kpos = start + lax.broadcasted_iota(I32, (1, kw_len), 1)
    allowed = jnp.abs(qpos - kpos) <= WINDOW
    gw = A_GROUP * HEAD_DIM
    grp = lax.broadcasted_iota(I32, (1, gw), 1) // HEAD_DIM
    outs = []
    for h in range(A_KV_HEADS):
        qh = q[:, h * gw:(h + 1) * gw]
        kh = kw[:, h * gw:(h + 1) * gw]
        vh = vw[:, h * gw:(h + 1) * gw]
        acc = jnp.zeros((blk, gw), F32)
        for g in range(A_GROUP):
            qg = jnp.where(grp == g, qh, jnp.zeros_like(qh))
            s = lax.dot_general(qg, kh, (((1,), (1,)), ((), ())), preferred_element_type=F32)
            s = jnp.where(allowed, s, NEG)
            sk = sink_ref[h * A_GROUP + g]
            m = jnp.maximum(jnp.max(s, axis=-1, keepdims=True), sk)
            p = jnp.exp(s - m)
            denom = jnp.sum(p, axis=-1, keepdims=True) + jnp.exp(sk - m)
            pn = (p / denom).astype(BF16)
            of = jnp.dot(pn, vh, preferred_element_type=F32)
            acc = jnp.where(grp == g, of, acc)
        outs.append(acc)
    oa = jnp.concatenate(outs, axis=1)
    o_ref[...] = (_rms(oa) * g_ref[...]).astype(BF16)


def _swa(proj, sink, g_a, bsz, seq):
    t = proj.shape[0]
    blk = WINDOW
    nq = seq // blk
    grid_spec = pltpu.PrefetchScalarGridSpec(
        num_scalar_prefetch=1, grid=(bsz, nq),
        in_specs=[pl.BlockSpec((blk, PROJ_BLK), lambda b, n, s: (b * nq + n, 0)),
                  pl.BlockSpec((seq, PROJ_BLK), lambda b, n, s: (b, 1)),
                  pl.BlockSpec((seq, PROJ_BLK), lambda b, n, s: (b, 2)),
                  pl.BlockSpec((1, A_Q_W), lambda b, n, s: (0, 0))],
        out_specs=pl.BlockSpec((blk, A_Q_W), lambda b, n, s: (b * nq + n, 0)))
    return pl.pallas_call(
        functools.partial(_swa_body, seq=seq),
        out_shape=jax.ShapeDtypeStruct((t, A_Q_W), BF16),
        grid_spec=grid_spec,
        compiler_params=_cparams(("arbitrary", "arbitrary")),
        name="swa",
    )(sink, proj, proj, proj, g_a)


def _na_body(q_ref, k_ref, v_ref, bias_ref, g_ref, o_ref, *, rows, kr):
    r = pl.program_id(1)
    rs = jnp.clip(r - kr // 2, 0, rows - kr)
    start = pl.multiple_of(rs * GRID_W, GRID_W)
    kw = k_ref[pl.ds(start, kr * GRID_W), :]
    vw = v_ref[pl.ds(start, kr * GRID_W), :]
    q = q_ref[...]
    half = lax.broadcasted_iota(I32, (1, LANES), 1) // HEAD_DIM
    outs = []
    for pr in range(B_W // LANES):
        qp = q[:, pr * LANES:(pr + 1) * LANES]
        kp = kw[:, pr * LANES:(pr + 1) * LANES]
        vp = vw[:, pr * LANES:(pr + 1) * LANES]
        op = jnp.zeros((GRID_W, LANES), F32)
        for j in range(LANES // HEAD_DIM):
            hh = pr * (LANES // HEAD_DIM) + j
            qm = jnp.where(half == j, qp, jnp.zeros_like(qp))
            s = lax.dot_general(qm, kp, (((1,), (1,)), ((), ())), preferred_element_type=F32)
            s = s + bias_ref[hh, 0]
            m = jnp.max(s, axis=-1, keepdims=True)
            p = jnp.exp(s - m)
            pn = (p / jnp.sum(p, axis=-1, keepdims=True)).astype(BF16)
            of = jnp.dot(pn, vp, preferred_element_type=F32)
            op = jnp.where(half == j, of, op)
        outs.append(op)
    ob = jnp.concatenate(outs, axis=1)
    o_ref[...] = (_rms(ob) * g_ref[...]).astype(BF16)


def _natten(proj, bias_t, g_b, bsz, seq):
    t = proj.shape[0]
    rows = seq // GRID_W
    kr = min(NA_ROWS_MAX, rows)

    def bias_idx(b, r):
        return (0, r - jnp.clip(r - kr // 2, 0, rows - kr), 0, 0)

    return pl.pallas_call(
        functools.partial(_na_body, rows=rows, kr=kr),
        out_shape=jax.ShapeDtypeStruct((t, B_W), BF16),
        grid=(bsz, rows),
        in_specs=[pl.BlockSpec((GRID_W, PROJ_BLK), lambda b, r: (b * rows + r, 3)),
                  pl.BlockSpec((seq, PROJ_BLK), lambda b, r: (b, 4)),
                  pl.BlockSpec((seq, PROJ_BLK), lambda b, r: (b, 5)),
                  pl.BlockSpec((B_HEADS, 1, GRID_W, kr * GRID_W), bias_idx),
                  pl.BlockSpec((1, B_W), lambda b, r: (0, 0))],
        out_specs=pl.BlockSpec((GRID_W, B_W), lambda b, r: (b * rows + r, 0)),
        compiler_params=_cparams(("arbitrary", "arbitrary")),
        name="natten",
    )(proj, proj, proj, bias_t, g_b)


def _na_bias_table(rpb, rows):
    kr = min(NA_ROWS_MAX, rows)
    kc = NA_COLS
    col = jnp.arange(GRID_W)
    col_start = jnp.clip(col - kc // 2, 0, GRID_W - kc)
    col_mask = (col[None, :] >= col_start[:, None]) & (col[None, :] < col_start[:, None] + kc)
    col_off = jnp.clip(col[None, :] - col[:, None], -(kc - 1), kc - 1) + (NA_COLS - 1)
    rows_sel = jnp.stack([rpb[:, NA_ROWS_MAX - 1 - cs:NA_ROWS_MAX - 1 - cs + kr, :]
                          for cs in range(kr)], axis=1)
    pick = (col_off[None, :, :] == jnp.arange(2 * kc - 1)[:, None, None]).astype(F32)
    tbl = jnp.einsum('hcjr,rqk->hcqjk', rows_sel, pick, precision=lax.Precision.HIGHEST)
    tbl = jnp.where(col_mask[None, None, :, None, :], tbl, NEG)
    return tbl.reshape(rpb.shape[0], kr, GRID_W, kr * GRID_W).astype(F32)


def _outproj_body(x_ref, ma_ref, mb_ref, wo_ref, gm_ref, sf_ref, cf_ref, wr_ref, br_ref,
                  x1_ref, h2_ref, ri_ref, gt_ref, cnt_ref, carry):
    i = pl.program_id(0)

    @pl.when(i == 0)
    def _():
        carry[...] = jnp.zeros_like(carry)

    tm = x_ref.shape[0]
    mix = (jnp.dot(ma_ref[...], wo_ref[:A_Q_W, :], preferred_element_type=F32)
           + jnp.dot(mb_ref[...], wo_ref[A_Q_W:, :], preferred_element_type=F32))
    x1 = x_ref[...] + gm_ref[0] * mix
    x1_ref[...] = x1
    h = _rms(x1) * (1.0 + cf_ref[0]) + sf_ref[0]
    for c in range(ROW_TILE):
        h2_ref[pl.ds(c, tm, stride=ROW_TILE), :] = h[:, c * LANES:(c + 1) * LANES]
    logits = jnp.dot(h, wr_ref[...], preferred_element_type=F32,
                     precision=lax.Precision.HIGHEST) + br_ref[...]
    lane = lax.broadcasted_iota(I32, (tm, LANES), 1)
    work = logits
    vals, idxs = [], []
    for _ in range(TOP_K):
        m = jnp.max(work, axis=-1, keepdims=True)
        ik = jnp.min(jnp.where(work == m, lane, LANES), axis=-1, keepdims=True)
        vals.append(m)
        idxs.append(ik)
        work = jnp.where(lane == ik, -jnp.inf, work)
    es = [jnp.exp(v - vals[0]) for v in vals]
    den = es[0] + es[1] + es[2] + es[3]
    mh = jnp.zeros((tm, LANES), F32)
    for ik in idxs:
        mh = mh + (lane == ik).astype(F32)
    tri = (lax.broadcasted_iota(I32, (tm, tm), 0) > lax.broadcasted_iota(I32, (tm, tm), 1))
    cnt = jnp.dot(tri.astype(BF16), mh.astype(BF16), preferred_element_type=F32) + carry[...]
    ri = jnp.zeros((tm, LANES), I32)
    gt = jnp.zeros((tm, LANES), F32)
    for k in range(TOP_K):
        rank = jnp.sum(jnp.where(lane == idxs[k], cnt, 0.0), axis=-1, keepdims=True)
        ri = jnp.where(lane == k, idxs[k], ri)
        ri = jnp.where(lane == TOP_K + k, rank.astype(I32), ri)
        gt = jnp.where(lane == k, es[k] / den, gt)
    ri_ref[...] = ri
    gt_ref[...] = gt
    carry[...] = carry[...] + jnp.sum(mh, axis=0, keepdims=True)
    cnt_ref[...] = carry[...]


def _outproj(x2, mix_a, mix_b, w_out, gate_m, shift_f, scale_f, w_r, b_r, seq):
    t, d = x2.shape
    tm = TM_PROJ
    tpb = seq // tm
    mod_spec = pl.BlockSpec((1, 1, d), lambda i: (i // tpb, 0, 0))
    return pl.pallas_call(
        _outproj_body,
        out_shape=(jax.ShapeDtypeStruct((t, d), F32), jax.ShapeDtypeStruct((t * ROW_TILE, LANES), F32),
                   jax.ShapeDtypeStruct((t, LANES), I32), jax.ShapeDtypeStruct((t, LANES), F32),
                   jax.ShapeDtypeStruct((1, LANES), F32)),
        grid=(t // tm,),
        in_specs=[pl.BlockSpec((tm, d), lambda i: (i, 0)),
                  pl.BlockSpec((tm, A_Q_W), lambda i: (i, 0)),
                  pl.BlockSpec((tm, B_W), lambda i: (i, 0)),
                  pl.BlockSpec((A_Q_W + B_W, d), lambda i: (0, 0)),
                  mod_spec, mod_spec, mod_spec,
                  pl.BlockSpec((d, LANES), lambda i: (0, 0)),
                  pl.BlockSpec((1, LANES), lambda i: (0, 0))],
        out_specs=(pl.BlockSpec((tm, d), lambda i: (i, 0)),
                   pl.BlockSpec((tm * ROW_TILE, LANES), lambda i: (i, 0)),
                   pl.BlockSpec((tm, LANES), lambda i: (i, 0)),
                   pl.BlockSpec((tm, LANES), lambda i: (i, 0)),
                   pl.BlockSpec((1, LANES), lambda i: (0, 0))),
        scratch_shapes=[pltpu.VMEM((1, LANES), F32)],
        compiler_params=_cparams(("arbitrary",)),
        name="outproj",
    )(x2, mix_a, mix_b, w_out, gate_m, shift_f, scale_f, w_r, b_r)


def _experts_body(be_ref, tok0_ref, tokn_ref, dstp_ref, dstc_ref,
                  h_hbm, wgu_ref, bgu_ref, wd_ref, bd_ref, out_hbm,
                  xbuf, ybuf, xs_ref, gsem, ssem, *, n_real, n_blk):
    bm = MOE_BM
    rt = ROW_TILE
    d_ff = wd_ref.shape[1]
    j = pl.program_id(0)
    slot = j % 2
    other = 1 - slot

    def gather_copy(row0, i, s):
        return pltpu.make_async_copy(h_hbm.at[pl.ds(pl.multiple_of(row0, rt), rt), :],
                                     xbuf.at[s, pl.ds(i * rt, rt), :], gsem.at[s])

    def scatter_copy(i, row0, s):
        return pltpu.make_async_copy(ybuf.at[s, pl.ds(i * rt, rt), :],
                                     out_hbm.at[pl.ds(pl.multiple_of(row0, rt), rt), :], ssem.at[s])

    def issue_gather(tok_ref, s):
        for i in range(bm):
            gather_copy(tok_ref[0, 0, i], i, s).start()

    def wait_gather(s):
        pltpu.make_async_copy(h_hbm.at[pl.ds(0, bm * rt), :], xbuf.at[s], gsem.at[s]).wait()

    def wait_scatter(s):
        pltpu.make_async_copy(ybuf.at[s], out_hbm.at[pl.ds(0, bm * rt), :], ssem.at[s]).wait()

    @pl.when(j == 0)
    def _():
        ybuf[...] = jnp.zeros_like(ybuf)
        for s in range(2):
            init = pltpu.make_async_copy(
                ybuf.at[s], out_hbm.at[pl.ds((n_real + s * bm) * rt, bm * rt), :], ssem.at[s])
            init.start()
            init.wait()
        issue_gather(tok0_ref, 0)

    wait_gather(slot)

    @pl.when(j >= 1)
    def _():
        wait_scatter(slot)

    xs_ref[...] = jnp.concatenate(
        [xbuf[slot, pl.ds(c, bm, stride=rt), :] for c in range(rt)], axis=1).astype(BF16)

    issue_gather(tokn_ref, other)
    for i in range(bm):
        row0 = jnp.where(j == 0, (n_real + bm + i) * rt, dstp_ref[0, 0, i])
        scatter_copy(i, row0, other).start()

    xs = xs_ref[...]
    acts = []
    n_chunk = 4
    cw = d_ff // n_chunk
    for c in range(n_chunk):
        gate = jnp.dot(xs, wgu_ref[0, :, c * cw:(c + 1) * cw], preferred_element_type=F32)
        gate = gate + bgu_ref[0, :, c * cw:(c + 1) * cw]
        up = jnp.dot(xs, wgu_ref[0, :, d_ff + c * cw:d_ff + (c + 1) * cw], preferred_element_type=F32)
        up = up + bgu_ref[0, :, d_ff + c * cw:d_ff + (c + 1) * cw]
        gate = jnp.minimum(gate, SWIGLU_LIMIT)
        up = jnp.clip(up, -SWIGLU_LIMIT, SWIGLU_LIMIT)
        glu = gate / (1.0 + jnp.exp(-SWIGLU_ALPHA * gate))
        acts.append(((up + 1.0) * glu).astype(BF16))
    act = jnp.concatenate(acts, axis=1)
    ys = jnp.dot(act, wd_ref[0], preferred_element_type=F32) + bd_ref[0]
    for c in range(rt):
        ybuf[slot, pl.ds(c, bm, stride=rt), :] = ys[:, c * LANES:(c + 1) * LANES]

    @pl.when(j == n_blk - 1)
    def _():
        for i in range(bm):
            scatter_copy(i, dstc_ref[0, 0, i], slot).start()
        wait_scatter(slot)
        wait_scatter(other)
        wait_gather(other)


def _experts(blk_expert, slot_tok, slot_dst, h2, wgu, bgu, wd, bd, n_real):
    bm = MOE_BM
    rt = ROW_TILE
    n_blk = slot_tok.shape[0]
    d = wd.shape[2]
    d_ff = wd.shape[1]
    assert d == rt * LANES and h2.shape[1] == LANES
    smem = functools.partial(pl.BlockSpec, (1, 1, bm), memory_space=pltpu.SMEM)
    last = n_blk - 1
    grid_spec = pltpu.PrefetchScalarGridSpec(
        num_scalar_prefetch=1, grid=(n_blk,),
        in_specs=[smem(lambda j, be: (j, 0, 0)),
                  smem(lambda j, be: (jnp.minimum(j + 1, last), 0, 0)),
                  smem(lambda j, be: (jnp.maximum(j - 1, 0), 0, 0)),
                  smem(lambda j, be: (j, 0, 0)),
                  pl.BlockSpec(memory_space=pl.ANY),
                  pl.BlockSpec((1, d, 2 * d_ff), lambda j, be: (be[j], 0, 0)),
                  pl.BlockSpec((1, 1, 2 * d_ff), lambda j, be: (be[j], 0, 0)),
                  pl.BlockSpec((1, d_ff, d), lambda j, be: (be[j], 0, 0)),
                  pl.BlockSpec((1, 1, d), lambda j, be: (be[j], 0, 0))],
        out_specs=pl.BlockSpec(memory_space=pl.ANY),
        scratch_shapes=[pltpu.VMEM((2, bm * rt, LANES), F32), pltpu.VMEM((2, bm * rt, LANES), F32),
                        pltpu.VMEM((bm, d), BF16),
                        pltpu.SemaphoreType.DMA((2,)), pltpu.SemaphoreType.DMA((2,))])
    return pl.pallas_call(
        functools.partial(_experts_body, n_real=n_real, n_blk=n_blk),
        out_shape=jax.ShapeDtypeStruct(((n_real + 2 * bm) * rt, LANES), F32),
        grid_spec=grid_spec,
        compiler_params=_cparams(("arbitrary",)),
        name="experts",
    )(blk_expert, slot_tok, slot_tok, slot_dst, slot_dst, h2, wgu, bgu, wd, bd)


def _combine_body(x1_ref, y0_ref, y1_ref, y2_ref, y3_ref, gt_ref, gf_ref, gfin_ref, o_ref):
    gt = gt_ref[...]
    tm = x1_ref.shape[0]
    y_refs = (y0_ref, y1_ref, y2_ref, y3_ref)
    parts = []
    for c in range(ROW_TILE):
        acc = gt[:, 0:1] * y_refs[0][pl.ds(c, tm, stride=ROW_TILE), :]
        for k in range(1, TOP_K):
            acc = acc + gt[:, k:k + 1] * y_refs[k][pl.ds(c, tm, stride=ROW_TILE), :]
        parts.append(acc)
    y = jnp.concatenate(parts, axis=1)
    x2 = x1_ref[...] + gf_ref[0] * y
    o_ref[...] = _rms(x2) * gfin_ref[...]


def _combine(x1, ys, gt, gate_f, g_final, seq):
    t, d = x1.shape
    tm = TM_COMB
    tpb = seq // tm
    nt = t // tm
    y_specs = [pl.BlockSpec((tm * ROW_TILE, LANES), functools.partial(lambda i, k: (k * nt + i, 0), k=k))
               for k in range(TOP_K)]
    return pl.pallas_call(
        _combine_body,
        out_shape=jax.ShapeDtypeStruct((t, d), F32),
        grid=(nt,),
        in_specs=[pl.BlockSpec((tm, d), lambda i: (i, 0))] + y_specs + [
            pl.BlockSpec((tm, LANES), lambda i: (i, 0)),
            pl.BlockSpec((1, 1, d), lambda i: (i // tpb, 0, 0)),
            pl.BlockSpec((1, d), lambda i: (0, 0))],
        out_specs=pl.BlockSpec((tm, d), lambda i: (i, 0)),
        compiler_params=_cparams(("arbitrary",)),
        name="combine",
    )(x1, ys, ys, ys, ys, gt, gate_f, g_final)


def _rope_tables(seq):
    half = ROT_DIMS // 2
    pos = jnp.arange(seq, dtype=F32)
    inv_freq = ROPE_THETA ** (-jnp.arange(0, ROT_DIMS, 2, dtype=F32) / ROT_DIMS)
    ang = pos[:, None] * inv_freq[None, :]
    cos, sin = jnp.cos(ang), jnp.sin(ang)
    ones = jnp.ones((seq, HEAD_DIM - ROT_DIMS), F32)
    zeros = jnp.zeros((seq, HEAD_DIM - ROT_DIMS), F32)
    zh = jnp.zeros((seq, half), F32)
    rep = LANES // HEAD_DIM
    cos_t = jnp.tile(jnp.concatenate([cos, cos, ones], axis=1), (1, rep))
    sa_t = jnp.tile(jnp.concatenate([-sin, zh, zeros], axis=1), (1, rep))
    sb_t = jnp.tile(jnp.concatenate([zh, sin, zeros], axis=1), (1, rep))
    return cos_t, sa_t, sb_t


def _widen_in_proj(w_in):
    d = w_in.shape[0]
    scale = HEAD_DIM ** -0.5
    o = 0
    wqa = w_in[:, o:o + A_Q_W] * scale; o += A_Q_W
    wka = w_in[:, o:o + A_KV_W]; o += A_KV_W
    wva = w_in[:, o:o + A_KV_W]; o += A_KV_W
    wqb = w_in[:, o:o + B_W] * scale; o += B_W
    wkb = w_in[:, o:o + B_W]; o += B_W
    wvb = w_in[:, o:o + B_W]

    def widen(w):
        w = w.reshape(d, A_KV_HEADS, 1, HEAD_DIM)
        return jnp.broadcast_to(w, (d, A_KV_HEADS, A_GROUP, HEAD_DIM)).reshape(d, A_Q_W)

    return jnp.concatenate([wqa, widen(wka), widen(wva), wqb, wkb, wvb], axis=1).astype(BF16)


def kernel(x, c, w_ada, b_ada, w_in, sink, rpb, g_out_a, g_out_b, w_out, w_router, b_router,
           w_gate_up, b_gate_up, w_down, b_down, g_final):
    bsz, seq, d = x.shape
    t = bsz * seq
    depth = w_ada.shape[0]
    assert depth == 1, "the final norm is fused into the layer combine"
    bm = MOE_BM
    n_assign = t * TOP_K
    n_slots = n_assign + N_EXPERTS * bm
    n_blk = n_slots // bm

    cos_t, sa_t, sb_t = _rope_tables(seq)
    x2 = x.reshape(t, d)
    for l in range(depth):
        mod = _ada(c, w_ada[l], b_ada[l][None, :])
        shift_m, scale_m, gate_m, shift_f, scale_f, gate_f = [
            m.reshape(bsz, 1, d) for m in jnp.split(mod, 6, axis=-1)]

        proj = _inproj(x2, shift_m, scale_m, _widen_in_proj(w_in[l]), cos_t, sa_t, sb_t, seq)
        mix_a = _swa(proj, sink[l], g_out_a[l][None, :], bsz, seq)
        mix_b = _natten(proj, _na_bias_table(rpb[l], seq // GRID_W), g_out_b[l][None, :], bsz, seq)

        w_r = jnp.pad(w_router[l], ((0, 0), (0, LANES - N_EXPERTS)))
        b_r = jnp.pad(b_router[l], (0, LANES - N_EXPERTS), constant_values=NEG)[None, :]
        x1, h2, ri, gt, cnt = _outproj(x2, mix_a, mix_b, w_out[l].astype(BF16),
                                       gate_m, shift_f, scale_f, w_r, b_r, seq)

        top_idx = ri[:, :TOP_K]
        rank = ri[:, TOP_K:2 * TOP_K]
        counts = cnt[0, :N_EXPERTS].astype(I32)
        padded = ((counts + bm - 1) // bm) * bm
        pends = jnp.cumsum(padded)
        pstarts = pends - padded
        onehot = top_idx[:, :, None] == jnp.arange(N_EXPERTS, dtype=I32)[None, None, :]
        dest = jnp.sum(jnp.where(onehot, pstarts[None, None, :], 0), axis=-1) + rank
        slot_a = jnp.full((n_slots,), -1, I32).at[dest.T.reshape(-1)].set(
            jnp.arange(n_assign, dtype=I32), unique_indices=True)
        sid = jnp.arange(n_slots, dtype=I32)
        slot_tok = (jnp.where(slot_a >= 0, slot_a % t, 0) * ROW_TILE).reshape(n_blk, 1, bm)
        slot_dst = (jnp.where(slot_a >= 0, slot_a, n_assign + sid % (2 * bm)) * ROW_TILE).reshape(n_blk, 1, bm)
        blk_first = jnp.arange(n_blk, dtype=I32) * bm
        blk_expert = jnp.minimum(
            jnp.sum((pends[None, :] <= blk_first[:, None]).astype(I32), axis=1), N_EXPERTS - 1)

        ys = _experts(blk_expert, slot_tok, slot_dst, h2,
                      w_gate_up[l].astype(BF16), b_gate_up[l][:, None, :],
                      w_down[l].astype(BF16), b_down[l][:, None, :], n_assign)
        x2 = _combine(x1, ys, gt, gate_f, g_final[None, :], seq)
    return x2.reshape(bsz, seq, d)
```

```python
import functools

import jax
import jax.numpy as jnp
from jax import lax
from jax.experimental import pallas as pl
from jax.experimental.pallas import tpu as pltpu

F32 = jnp.float32
BF16 = jnp.bfloat16
I32 = jnp.int32

HEAD_DIM = 64
A_Q_HEADS = 8
A_KV_HEADS = 2
A_GROUP = A_Q_HEADS // A_KV_HEADS
B_HEADS = 8
A_Q_W = A_Q_HEADS * HEAD_DIM
A_KV_W = A_KV_HEADS * HEAD_DIM
B_W = B_HEADS * HEAD_DIM
WINDOW = 128
ROT_DIMS = HEAD_DIM // 4
ROPE_THETA = 500000.0
GRID_W = 64
NA_ROWS_MAX = 8
NA_COLS = 16
N_EXPERTS = 32
TOP_K = 4
SWIGLU_LIMIT = 7.0
SWIGLU_ALPHA = 1.702
EPS = 1e-5
NEG = -1e30

LANES = 128
ROW_TILE = 8
PROJ_BLK = 512
TM_PROJ = 512
TM_COMB = 256
MOE_BM = 256
NA_ROWS_PER_STEP = 8
NA_ROW_UNROLL = 4
VMEM_LIMIT = 48 << 20


def _cparams(sem):
    return pltpu.CompilerParams(dimension_semantics=sem, vmem_limit_bytes=VMEM_LIMIT)


def _rms(x):
    return x * lax.rsqrt(jnp.mean(x * x, axis=-1, keepdims=True) + EPS)


def _ada_body(c_ref, w_ref, b_ref, o_ref):
    c = c_ref[...]
    ca = c / (1.0 + jnp.exp(-c))
    o_ref[...] = jnp.dot(ca, w_ref[...], preferred_element_type=F32,
                         precision=lax.Precision.HIGHEST) + b_ref[...]


def _ada(c, w, b):
    bsz, d = c.shape
    n = w.shape[1]
    bn = n // 4
    return pl.pallas_call(
        _ada_body,
        out_shape=jax.ShapeDtypeStruct((bsz, n), F32),
        grid=(4,),
        in_specs=[pl.BlockSpec((bsz, d), lambda i: (0, 0)),
                  pl.BlockSpec((d, bn), lambda i: (0, i)),
                  pl.BlockSpec((1, bn), lambda i: (0, i))],
        out_specs=pl.BlockSpec((bsz, bn), lambda i: (0, i)),
        compiler_params=_cparams(("arbitrary",)),
        name="ada",
    )(c, w, b)


def _inproj_body(x_ref, sh_ref, sc_ref, w_ref, cos_ref, sa_ref, sb_ref, o_ref):
    h = (_rms(x_ref[...]) * (1.0 + sc_ref[0]) + sh_ref[0]).astype(BF16)
    cos = cos_ref[...]
    sa = sa_ref[...]
    sb = sb_ref[...]
    n_blk = w_ref.shape[1] // PROJ_BLK
    for cb in range(n_blk):
        p = jnp.dot(h, w_ref[:, cb * PROJ_BLK:(cb + 1) * PROJ_BLK], preferred_element_type=F32)
        if cb < 2:
            parts = []
            for j in range(PROJ_BLK // LANES):
                pj = p[:, j * LANES:(j + 1) * LANES]
                parts.append(pj * cos + pltpu.roll(pj, LANES - ROT_DIMS // 2, axis=1) * sa
                             + pltpu.roll(pj, ROT_DIMS // 2, axis=1) * sb)
            p = jnp.concatenate(parts, axis=1)
        o_ref[:, cb * PROJ_BLK:(cb + 1) * PROJ_BLK] = p.astype(BF16)


def _inproj(x2, shift, scale, w_ext, cos_t, sa_t, sb_t, seq):
    t, d = x2.shape
    n = w_ext.shape[1]
    tm = TM_PROJ
    tpb = seq // tm
    return pl.pallas_call(
        _inproj_body,
        out_shape=jax.ShapeDtypeStruct((t, n), BF16),
        grid=(t // tm,),
        in_specs=[pl.BlockSpec((tm, d), lambda i: (i, 0)),
                  pl.BlockSpec((1, 1, d), lambda i: (i // tpb, 0, 0)),
                  pl.BlockSpec((1, 1, d), lambda i: (i // tpb, 0, 0)),
                  pl.BlockSpec((d, n), lambda i: (0, 0)),
                  pl.BlockSpec((tm, LANES), lambda i: (i % tpb, 0)),
                  pl.BlockSpec((tm, LANES), lambda i: (i % tpb, 0)),
                  pl.BlockSpec((tm, LANES), lambda i: (i % tpb, 0))],
        out_specs=pl.BlockSpec((tm, n), lambda i: (i, 0)),
        compiler_params=_cparams(("arbitrary",)),
        name="inproj",
    )(x2, shift, scale, w_ext, cos_t, sa_t, sb_t)


def _swa_body(sink_ref, q_ref, k_ref, v_ref, g_ref, o_ref, *, seq):
    n = pl.program_id(1)
    blk = WINDOW
    kw_len = 3 * blk
    start = pl.multiple_of(jnp.clip(n * blk - blk, 0, seq - kw_len), blk)
    kw = k_ref[pl.ds(start, kw_len), :]
    vw = v_ref[pl.ds(start, kw_len), :]
    q = q_ref[...]
    gw = A_GROUP * HEAD_DIM
    srows = A_GROUP * blk
    row = lax.broadcasted_iota(I32, (srows, 1), 0)
    qpos = n * blk + row % blk
    kpos = start + lax.broadcasted_iota(I32, (1, kw_len), 1)
    allowed = jnp.abs(qpos - kpos) <= WINDOW
    grp = lax.broadcasted_iota(I32, (1, gw), 1) // HEAD_DIM
    outs = []
    for h in range(A_KV_HEADS):
        qh = q[:, h * gw:(h + 1) * gw]
        kh = kw[:, h * gw:(h + 1) * gw]
        vh = vw[:, h * gw:(h + 1) * gw]
        qm = jnp.concatenate([jnp.where(grp == g, qh, jnp.zeros_like(qh)) for g in range(A_GROUP)], axis=0)
        sk = jnp.full((srows, 1), sink_ref[h * A_GROUP], F32)
        for g in range(1, A_GROUP):
            sk = jnp.where(row // blk == g, sink_ref[h * A_GROUP + g], sk)
        s = lax.dot_general(qm, kh, (((1,), (1,)), ((), ())), preferred_element_type=F32)
        s = jnp.where(allowed, s, NEG)
        m = jnp.maximum(jnp.max(s, axis=-1, keepdims=True), sk)
        p = jnp.exp(s - m)
        denom = jnp.sum(p, axis=-1, keepdims=True) + jnp.exp(sk - m)
        pn = (p / denom).astype(BF16)
        res = jnp.dot(pn, vh, preferred_element_type=F32)
        acc = res[0:blk]
        for g in range(1, A_GROUP):
            acc = jnp.where(grp == g, res[g * blk:(g + 1) * blk], acc)
        outs.append(acc)
    oa = jnp.concatenate(outs, axis=1)
    o_ref[...] = (_rms(oa) * g_ref[...]).astype(BF16)


def _swa(proj, sink, g_a, bsz, seq):
    t = proj.shape[0]
    blk = WINDOW
    nq = seq // blk
    grid_spec = pltpu.PrefetchScalarGridSpec(
        num_scalar_prefetch=1, grid=(bsz, nq),
        in_specs=[pl.BlockSpec((blk, PROJ_BLK), lambda b, n, s: (b * nq + n, 0)),
                  pl.BlockSpec((seq, PROJ_BLK), lambda b, n, s: (b, 1)),
                  pl.BlockSpec((seq, PROJ_BLK), lambda b, n, s: (b, 2)),
                  pl.BlockSpec((1, A_Q_W), lambda b, n, s: (0, 0))],
        out_specs=pl.BlockSpec((blk, A_Q_W), lambda b, n, s: (b * nq + n, 0)))
    return pl.pallas_call(
        functools.partial(_swa_body, seq=seq),
        out_shape=jax.ShapeDtypeStruct((t, A_Q_W), BF16),
        grid_spec=grid_spec,
        compiler_params=_cparams(("arbitrary", "arbitrary")),
        name="swa",
    )(sink, proj, proj, proj, g_a)


def _na_body(q_ref, k_ref, v_ref, bias_ref, g_ref, o_ref, *, rows, kr, rb):
    blk = pl.program_id(1)
    per = LANES // HEAD_DIM
    n_pair = B_W // LANES
    half = lax.broadcasted_iota(I32, (1, LANES), 1) // HEAD_DIM
    gain = g_ref[...]

    def one_row(i, carry):
        r = blk * rb + i
        rs = jnp.clip(r - kr // 2, 0, rows - kr)
        start = pl.multiple_of(rs * GRID_W, GRID_W)
        kw = k_ref[pl.ds(start, kr * GRID_W), :]
        vw = v_ref[pl.ds(start, kr * GRID_W), :]
        q = q_ref[pl.ds(pl.multiple_of(i * GRID_W, GRID_W), GRID_W), :]
        s_parts = []
        for pr in range(n_pair):
            qp = q[:, pr * LANES:(pr + 1) * LANES]
            qm = jnp.concatenate([jnp.where(half == j, qp, jnp.zeros_like(qp)) for j in range(per)], axis=0)
            s_parts.append(lax.dot_general(qm, kw[:, pr * LANES:(pr + 1) * LANES],
                                           (((1,), (1,)), ((), ())), preferred_element_type=F32))
        s = jnp.concatenate(s_parts, axis=0) + bias_ref[r - rs]
        m = jnp.max(s, axis=-1, keepdims=True)
        p = jnp.exp(s - m)
        pn = (p / jnp.sum(p, axis=-1, keepdims=True)).astype(BF16)
        o_parts = []
        for pr in range(n_pair):
            res = jnp.dot(pn[pr * per * GRID_W:(pr + 1) * per * GRID_W], vw[:, pr * LANES:(pr + 1) * LANES],
                          preferred_element_type=F32)
            op = res[0:GRID_W]
            for j in range(1, per):
                op = jnp.where(half == j, res[j * GRID_W:(j + 1) * GRID_W], op)
            o_parts.append(op)
        ob = jnp.concatenate(o_parts, axis=1)
        o_ref[pl.ds(pl.multiple_of(i * GRID_W, GRID_W), GRID_W), :] = (_rms(ob) * gain).astype(BF16)
        return carry

    def row_group(ig, carry):
        for u in range(NA_ROW_UNROLL):
            one_row(NA_ROW_UNROLL * ig + u, carry)
        return carry

    lax.fori_loop(0, rb // NA_ROW_UNROLL, row_group, 0)


def _natten(proj, bias_t, g_b, bsz, seq):
    t = proj.shape[0]
    rows = seq // GRID_W
    kr = min(NA_ROWS_MAX, rows)
    rb = NA_ROWS_PER_STEP
    nb = rows // rb
    return pl.pallas_call(
        functools.partial(_na_body, rows=rows, kr=kr, rb=rb),
        out_shape=jax.ShapeDtypeStruct((t, B_W), BF16),
        grid=(bsz, nb),
        in_specs=[pl.BlockSpec((rb * GRID_W, PROJ_BLK), lambda b, r: (b * nb + r, 3)),
                  pl.BlockSpec((seq, PROJ_BLK), lambda b, r: (b, 4)),
                  pl.BlockSpec((seq, PROJ_BLK), lambda b, r: (b, 5)),
                  pl.BlockSpec((kr, B_HEADS * GRID_W, kr * GRID_W), lambda b, r: (0, 0, 0)),
                  pl.BlockSpec((1, B_W), lambda b, r: (0, 0))],
        out_specs=pl.BlockSpec((rb * GRID_W, B_W), lambda b, r: (b * nb + r, 0)),
        compiler_params=_cparams(("arbitrary", "arbitrary")),
        name="natten",
    )(proj, proj, proj, bias_t, g_b)


def _na_bias_table(rpb, rows):
    kr = min(NA_ROWS_MAX, rows)
    kc = NA_COLS
    col = jnp.arange(GRID_W)
    col_start = jnp.clip(col - kc // 2, 0, GRID_W - kc)
    col_mask = (col[None, :] >= col_start[:, None]) & (col[None, :] < col_start[:, None] + kc)
    col_off = jnp.clip(col[None, :] - col[:, None], -(kc - 1), kc - 1) + (NA_COLS - 1)
    rows_sel = jnp.stack([rpb[:, NA_ROWS_MAX - 1 - cs:NA_ROWS_MAX - 1 - cs + kr, :]
                          for cs in range(kr)], axis=0)
    pick = (col_off[None, :, :] == jnp.arange(2 * kc - 1)[:, None, None]).astype(F32)
    tbl = jnp.einsum('chjr,rqk->chqjk', rows_sel, pick, precision=lax.Precision.HIGHEST)
    tbl = jnp.where(col_mask[None, None, :, None, :], tbl, NEG)
    return tbl.reshape(kr, rpb.shape[0] * GRID_W, kr * GRID_W).astype(F32)


def _outproj_body(x_ref, ma_ref, mb_ref, wo_ref, gm_ref, sf_ref, cf_ref, wr_ref, br_ref,
                  x1_ref, h2_ref, ri_ref, gt_ref, cnt_ref, carry):
    i = pl.program_id(0)

    @pl.when(i == 0)
    def _():
        carry[...] = jnp.zeros_like(carry)

    tm = x_ref.shape[0]
    mix = (jnp.dot(ma_ref[...], wo_ref[:A_Q_W, :], preferred_element_type=F32)
           + jnp.dot(mb_ref[...], wo_ref[A_Q_W:, :], preferred_element_type=F32))
    x1 = x_ref[...] + gm_ref[0] * mix
    x1_ref[...] = x1
    h = _rms(x1) * (1.0 + cf_ref[0]) + sf_ref[0]
    for c in range(ROW_TILE):
        h2_ref[pl.ds(c, tm, stride=ROW_TILE), :] = h[:, c * LANES:(c + 1) * LANES]
    logits = jnp.dot(h, wr_ref[...], preferred_element_type=F32,
                     precision=lax.Precision.HIGHEST) + br_ref[...]
    lane = lax.broadcasted_iota(I32, (tm, LANES), 1)
    work = logits
    vals, idxs = [], []
    for _ in range(TOP_K):
        m = jnp.max(work, axis=-1, keepdims=True)
        ik = jnp.min(jnp.where(work == m, lane, LANES), axis=-1, keepdims=True)
        vals.append(m)
        idxs.append(ik)
        work = jnp.where(lane == ik, -jnp.inf, work)
    es = [jnp.exp(v - vals[0]) for v in vals]
    den = es[0] + es[1] + es[2] + es[3]
    mh = jnp.zeros((tm, LANES), F32)
    for ik in idxs:
        mh = mh + (lane == ik).astype(F32)
    tri = (lax.broadcasted_iota(I32, (tm, tm), 0) > lax.broadcasted_iota(I32, (tm, tm), 1))
    cnt = jnp.dot(tri.astype(BF16), mh.astype(BF16), preferred_element_type=F32) + carry[...]
    ri = jnp.zeros((tm, LANES), I32)
    gt = jnp.zeros((tm, LANES), F32)
    for k in range(TOP_K):
        rank = jnp.sum(jnp.where(lane == idxs[k], cnt, 0.0), axis=-1, keepdims=True)
        ri = jnp.where(lane == k, idxs[k], ri)
        ri = jnp.where(lane == TOP_K + k, rank.astype(I32), ri)
        gt = jnp.where(lane == k, es[k] / den, gt)
    ri_ref[...] = ri
    gt_ref[...] = gt
    carry[...] = carry[...] + jnp.sum(mh, axis=0, keepdims=True)
    cnt_ref[...] = carry[...]


def _outproj(x2, mix_a, mix_b, w_out, gate_m, shift_f, scale_f, w_r, b_r, seq):
    t, d = x2.shape
    tm = TM_PROJ
    tpb = seq // tm
    mod_spec = pl.BlockSpec((1, 1, d), lambda i: (i // tpb, 0, 0))
    return pl.pallas_call(
        _outproj_body,
        out_shape=(jax.ShapeDtypeStruct((t, d), F32), jax.ShapeDtypeStruct((t * ROW_TILE, LANES), F32),
                   jax.ShapeDtypeStruct((t, LANES), I32), jax.ShapeDtypeStruct((t, LANES), F32),
                   jax.ShapeDtypeStruct((1, LANES), F32)),
        grid=(t // tm,),
        in_specs=[pl.BlockSpec((tm, d), lambda i: (i, 0)),
                  pl.BlockSpec((tm, A_Q_W), lambda i: (i, 0)),
                  pl.BlockSpec((tm, B_W), lambda i: (i, 0)),
                  pl.BlockSpec((A_Q_W + B_W, d), lambda i: (0, 0)),
                  mod_spec, mod_spec, mod_spec,
                  pl.BlockSpec((d, LANES), lambda i: (0, 0)),
                  pl.BlockSpec((1, LANES), lambda i: (0, 0))],
        out_specs=(pl.BlockSpec((tm, d), lambda i: (i, 0)),
                   pl.BlockSpec((tm * ROW_TILE, LANES), lambda i: (i, 0)),
                   pl.BlockSpec((tm, LANES), lambda i: (i, 0)),
                   pl.BlockSpec((tm, LANES), lambda i: (i, 0)),
                   pl.BlockSpec((1, LANES), lambda i: (0, 0))),
        scratch_shapes=[pltpu.VMEM((1, LANES), F32)],
        compiler_params=_cparams(("arbitrary",)),
        name="outproj",
    )(x2, mix_a, mix_b, w_out, gate_m, shift_f, scale_f, w_r, b_r)


def _experts_body(be_ref, tok0_ref, tokn_ref, dstp_ref, dstc_ref,
                  h_hbm, wgu_ref, bgu_ref, wd_ref, bd_ref, out_hbm,
                  xbuf, ybuf, xs_ref, gsem, ssem, *, n_real, n_blk):
    bm = MOE_BM
    rt = ROW_TILE
    d_ff = wd_ref.shape[1]
    j = pl.program_id(0)
    slot = j % 2
    other = 1 - slot

    def gather_copy(row0, i, s):
        return pltpu.make_async_copy(h_hbm.at[pl.ds(pl.multiple_of(row0, rt), rt), :],
                                     xbuf.at[s, pl.ds(i * rt, rt), :], gsem.at[s])

    def scatter_copy(i, row0, s):
        return pltpu.make_async_copy(ybuf.at[s, pl.ds(i * rt, rt), :],
                                     out_hbm.at[pl.ds(pl.multiple_of(row0, rt), rt), :], ssem.at[s])

    def issue_gather(tok_ref, s):
        for i in range(bm):
            gather_copy(tok_ref[0, 0, i], i, s).start()

    def wait_gather(s):
        pltpu.make_async_copy(h_hbm.at[pl.ds(0, bm * rt), :], xbuf.at[s], gsem.at[s]).wait()

    def wait_scatter(s):
        pltpu.make_async_copy(ybuf.at[s], out_hbm.at[pl.ds(0, bm * rt), :], ssem.at[s]).wait()

    @pl.when(j == 0)
    def _():
        ybuf[...] = jnp.zeros_like(ybuf)
        for s in range(2):
            init = pltpu.make_async_copy(
                ybuf.at[s], out_hbm.at[pl.ds((n_real + s * bm) * rt, bm * rt), :], ssem.at[s])
            init.start()
            init.wait()
        issue_gather(tok0_ref, 0)

    wait_gather(slot)

    @pl.when(j >= 1)
    def _():
        wait_scatter(slot)

    xs_ref[...] = jnp.concatenate(
        [xbuf[slot, pl.ds(c, bm, stride=rt), :] for c in range(rt)], axis=1).astype(BF16)

    issue_gather(tokn_ref, other)
    for i in range(bm):
        row0 = jnp.where(j == 0, (n_real + bm + i) * rt, dstp_ref[0, 0, i])
        scatter_copy(i, row0, other).start()

    xs = xs_ref[...]
    acts = []
    n_chunk = 4
    cw = d_ff // n_chunk
    for c in range(n_chunk):
        gate = jnp.dot(xs, wgu_ref[0, :, c * cw:(c + 1) * cw], preferred_element_type=F32)
        gate = gate + bgu_ref[0, :, c * cw:(c + 1) * cw]
        up = jnp.dot(xs, wgu_ref[0, :, d_ff + c * cw:d_ff + (c + 1) * cw], preferred_element_type=F32)
        up = up + bgu_ref[0, :, d_ff + c * cw:d_ff + (c + 1) * cw]
        gate = jnp.minimum(gate, SWIGLU_LIMIT)
        up = jnp.clip(up, -SWIGLU_LIMIT, SWIGLU_LIMIT)
        glu = gate / (1.0 + jnp.exp(-SWIGLU_ALPHA * gate))
        acts.append(((up + 1.0) * glu).astype(BF16))
    act = jnp.concatenate(acts, axis=1)
    ys = jnp.dot(act, wd_ref[0], preferred_element_type=F32) + bd_ref[0]
    for c in range(rt):
        ybuf[slot, pl.ds(c, bm, stride=rt), :] = ys[:, c * LANES:(c + 1) * LANES]

    @pl.when(j == n_blk - 1)
    def _():
        for i in range(bm):
            scatter_copy(i, dstc_ref[0, 0, i], slot).start()
        wait_scatter(slot)
        wait_scatter(other)
        wait_gather(other)


def _experts(blk_expert, slot_tok, slot_dst, h2, wgu, bgu, wd, bd, n_real):
    bm = MOE_BM
    rt = ROW_TILE
    n_blk = slot_tok.shape[0]
    d = wd.shape[2]
    d_ff = wd.shape[1]
    assert d == rt * LANES and h2.shape[1] == LANES
    smem = functools.partial(pl.BlockSpec, (1, 1, bm), memory_space=pltpu.SMEM)
    last = n_blk - 1
    grid_spec = pltpu.PrefetchScalarGridSpec(
        num_scalar_prefetch=1, grid=(n_blk,),
        in_specs=[smem(lambda j, be: (j, 0, 0)),
                  smem(lambda j, be: (jnp.minimum(j + 1, last), 0, 0)),
                  smem(lambda j, be: (jnp.maximum(j - 1, 0), 0, 0)),
                  smem(lambda j, be: (j, 0, 0)),
                  pl.BlockSpec(memory_space=pl.ANY),
                  pl.BlockSpec((1, d, 2 * d_ff), lambda j, be: (be[j], 0, 0)),
                  pl.BlockSpec((1, 1, 2 * d_ff), lambda j, be: (be[j], 0, 0)),
                  pl.BlockSpec((1, d_ff, d), lambda j, be: (be[j], 0, 0)),
                  pl.BlockSpec((1, 1, d), lambda j, be: (be[j], 0, 0))],
        out_specs=pl.BlockSpec(memory_space=pl.ANY),
        scratch_shapes=[pltpu.VMEM((2, bm * rt, LANES), F32), pltpu.VMEM((2, bm * rt, LANES), F32),
                        pltpu.VMEM((bm, d), BF16),
                        pltpu.SemaphoreType.DMA((2,)), pltpu.SemaphoreType.DMA((2,))])
    return pl.pallas_call(
        functools.partial(_experts_body, n_real=n_real, n_blk=n_blk),
        out_shape=jax.ShapeDtypeStruct(((n_real + 2 * bm) * rt, LANES), F32),
        grid_spec=grid_spec,
        compiler_params=_cparams(("arbitrary",)),
        name="experts",
    )(blk_expert, slot_tok, slot_tok, slot_dst, slot_dst, h2, wgu, bgu, wd, bd)


def _combine_body(x1_ref, y0_ref, y1_ref, y2_ref, y3_ref, gt_ref, gf_ref, gfin_ref, o_ref):
    gt = gt_ref[...]
    tm = x1_ref.shape[0]
    y_refs = (y0_ref, y1_ref, y2_ref, y3_ref)
    parts = []
    for c in range(ROW_TILE):
        acc = gt[:, 0:1] * y_refs[0][pl.ds(c, tm, stride=ROW_TILE), :]
        for k in range(1, TOP_K):
            acc = acc + gt[:, k:k + 1] * y_refs[k][pl.ds(c, tm, stride=ROW_TILE), :]
        parts.append(acc)
    y = jnp.concatenate(parts, axis=1)
    x2 = x1_ref[...] + gf_ref[0] * y
    o_ref[...] = _rms(x2) * gfin_ref[...]


def _combine(x1, ys, gt, gate_f, g_final, seq):
    t, d = x1.shape
    tm = TM_COMB
    tpb = seq // tm
    nt = t // tm
    y_specs = [pl.BlockSpec((tm * ROW_TILE, LANES), functools.partial(lambda i, k: (k * nt + i, 0), k=k))
               for k in range(TOP_K)]
    return pl.pallas_call(
        _combine_body,
        out_shape=jax.ShapeDtypeStruct((t, d), F32),
        grid=(nt,),
        in_specs=[pl.BlockSpec((tm, d), lambda i: (i, 0))] + y_specs + [
            pl.BlockSpec((tm, LANES), lambda i: (i, 0)),
            pl.BlockSpec((1, 1, d), lambda i: (i // tpb, 0, 0)),
            pl.BlockSpec((1, d), lambda i: (0, 0))],
        out_specs=pl.BlockSpec((tm, d), lambda i: (i, 0)),
        compiler_params=_cparams(("arbitrary",)),
        name="combine",
    )(x1, ys, ys, ys, ys, gt, gate_f, g_final)


def _rope_tables(seq):
    half = ROT_DIMS // 2
    pos = jnp.arange(seq, dtype=F32)
    inv_freq = ROPE_THETA ** (-jnp.arange(0, ROT_DIMS, 2, dtype=F32) / ROT_DIMS)
    ang = pos[:, None] * inv_freq[None, :]
    cos, sin = jnp.cos(ang), jnp.sin(ang)
    ones = jnp.ones((seq, HEAD_DIM - ROT_DIMS), F32)
    zeros = jnp.zeros((seq, HEAD_DIM - ROT_DIMS), F32)
    zh = jnp.zeros((seq, half), F32)
    rep = LANES // HEAD_DIM
    cos_t = jnp.tile(jnp.concatenate([cos, cos, ones], axis=1), (1, rep))
    sa_t = jnp.tile(jnp.concatenate([-sin, zh, zeros], axis=1), (1, rep))
    sb_t = jnp.tile(jnp.concatenate([zh, sin, zeros], axis=1), (1, rep))
    return cos_t, sa_t, sb_t


def _widen_in_proj(w_in):
    d = w_in.shape[0]
    scale = HEAD_DIM ** -0.5
    o = 0
    wqa = w_in[:, o:o + A_Q_W] * scale; o += A_Q_W
    wka = w_in[:, o:o + A_KV_W]; o += A_KV_W
    wva = w_in[:, o:o + A_KV_W]; o += A_KV_W
    wqb = w_in[:, o:o + B_W] * scale; o += B_W
    wkb = w_in[:, o:o + B_W]; o += B_W
    wvb = w_in[:, o:o + B_W]

    def widen(w):
        w = w.reshape(d, A_KV_HEADS, 1, HEAD_DIM)
        return jnp.broadcast_to(w, (d, A_KV_HEADS, A_GROUP, HEAD_DIM)).reshape(d, A_Q_W)

    return jnp.concatenate([wqa, widen(wka), widen(wva), wqb, wkb, wvb], axis=1).astype(BF16)


def kernel(x, c, w_ada, b_ada, w_in, sink, rpb, g_out_a, g_out_b, w_out, w_router, b_router,
           w_gate_up, b_gate_up, w_down, b_down, g_final):
    bsz, seq, d = x.shape
    t = bsz * seq
    depth = w_ada.shape[0]
    assert depth == 1, "the final norm is fused into the layer combine"
    bm = MOE_BM
    n_assign = t * TOP_K
    n_slots = n_assign + N_EXPERTS * bm
    n_blk = n_slots // bm

    cos_t, sa_t, sb_t = _rope_tables(seq)
    x2 = x.reshape(t, d)
    for l in range(depth):
        mod = _ada(c, w_ada[l], b_ada[l][None, :])
        shift_m, scale_m, gate_m, shift_f, scale_f, gate_f = [
            m.reshape(bsz, 1, d) for m in jnp.split(mod, 6, axis=-1)]

        proj = _inproj(x2, shift_m, scale_m, _widen_in_proj(w_in[l]), cos_t, sa_t, sb_t, seq)
        mix_a = _swa(proj, sink[l], g_out_a[l][None, :], bsz, seq)
        mix_b = _natten(proj, _na_bias_table(rpb[l], seq // GRID_W), g_out_b[l][None, :], bsz, seq)

        w_r = jnp.pad(w_router[l], ((0, 0), (0, LANES - N_EXPERTS)))
        b_r = jnp.pad(b_router[l], (0, LANES - N_EXPERTS), constant_values=NEG)[None, :]
        x1, h2, ri, gt, cnt = _outproj(x2, mix_a, mix_b, w_out[l].astype(BF16),
                                       gate_m, shift_f, scale_f, w_r, b_r, seq)

        top_idx = ri[:, :TOP_K]
        rank = ri[:, TOP_K:2 * TOP_K]
        counts = cnt[0, :N_EXPERTS].astype(I32)
        padded = ((counts + bm - 1) // bm) * bm
        pends = jnp.cumsum(padded)
        pstarts = pends - padded
        onehot = top_idx[:, :, None] == jnp.arange(N_EXPERTS, dtype=I32)[None, None, :]
        dest = jnp.sum(jnp.where(onehot, pstarts[None, None, :], 0), axis=-1) + rank
        slot_a = jnp.full((n_slots,), -1, I32).at[dest.T.reshape(-1)].set(
            jnp.arange(n_assign, dtype=I32), unique_indices=True)
        sid = jnp.arange(n_slots, dtype=I32)
        slot_tok = (jnp.where(slot_a >= 0, slot_a % t, 0) * ROW_TILE).reshape(n_blk, 1, bm)
        slot_dst = (jnp.where(slot_a >= 0, slot_a, n_assign + sid % (2 * bm)) * ROW_TILE).reshape(n_blk, 1, bm)
        blk_first = jnp.arange(n_blk, dtype=I32) * bm
        blk_expert = jnp.minimum(
            jnp.sum((pends[None, :] <= blk_first[:, None]).astype(I32), axis=1), N_EXPERTS - 1)

        ys = _experts(blk_expert, slot_tok, slot_dst, h2,
                      w_gate_up[l].astype(BF16), b_gate_up[l][:, None, :],
                      w_down[l].astype(BF16), b_down[l][:, None, :], n_assign)
        x2 = _combine(x1, ys, gt, gate_f, g_final[None, :], seq)
    return x2.reshape(bsz, seq, d)
```

```python
import functools

import jax
import jax.numpy as jnp
from jax import lax
from jax.experimental import pallas as pl
from jax.experimental.pallas import tpu as pltpu

F32 = jnp.float32
BF16 = jnp.bfloat16
I32 = jnp.int32

HEAD_DIM = 64
A_Q_HEADS = 8
A_KV_HEADS = 2
A_GROUP = A_Q_HEADS // A_KV_HEADS
B_HEADS = 8
A_Q_W = A_Q_HEADS * HEAD_DIM
A_KV_W = A_KV_HEADS * HEAD_DIM
B_W = B_HEADS * HEAD_DIM
WINDOW = 128
ROT_DIMS = HEAD_DIM // 4
ROPE_THETA = 500000.0
GRID_W = 64
NA_ROWS_MAX = 8
NA_COLS = 16
N_EXPERTS = 32
TOP_K = 4
SWIGLU_LIMIT = 7.0
SWIGLU_ALPHA = 1.702
EPS = 1e-5
NEG = -1e30

LANES = 128
ROW_TILE = 8
PROJ_BLK = 512
TM_PROJ = 512
SORT_TM = 256
MOE_BM = 256
NA_ROWS_PER_STEP = 8
NA_ROW_UNROLL = 4
VMEM_LIMIT = 48 << 20


def _cparams(sem):
    return pltpu.CompilerParams(dimension_semantics=sem, vmem_limit_bytes=VMEM_LIMIT)


def _rms(x):
    return x * lax.rsqrt(jnp.mean(x * x, axis=-1, keepdims=True) + EPS)


def _ada_body(c_ref, w_ref, b_ref, o_ref):
    c = c_ref[...]
    ca = c / (1.0 + jnp.exp(-c))
    o_ref[...] = jnp.dot(ca, w_ref[...], preferred_element_type=F32,
                         precision=lax.Precision.HIGHEST) + b_ref[...]


def _ada(c, w, b):
    bsz, d = c.shape
    n = w.shape[1]
    bn = n // 4
    return pl.pallas_call(
        _ada_body,
        out_shape=jax.ShapeDtypeStruct((bsz, n), F32),
        grid=(4,),
        in_specs=[pl.BlockSpec((bsz, d), lambda i: (0, 0)),
                  pl.BlockSpec((d, bn), lambda i: (0, i)),
                  pl.BlockSpec((1, bn), lambda i: (0, i))],
        out_specs=pl.BlockSpec((bsz, bn), lambda i: (0, i)),
        compiler_params=_cparams(("arbitrary",)),
        name="ada",
    )(c, w, b)


def _inproj_body(x_ref, sh_ref, sc_ref, w_ref, cos_ref, sa_ref, sb_ref, o_ref):
    h = (_rms(x_ref[...]) * (1.0 + sc_ref[0]) + sh_ref[0]).astype(BF16)
    cos = cos_ref[...]
    sa = sa_ref[...]
    sb = sb_ref[...]
    n_blk = w_ref.shape[1] // PROJ_BLK
    for cb in range(n_blk):
        p = jnp.dot(h, w_ref[:, cb * PROJ_BLK:(cb + 1) * PROJ_BLK], preferred_element_type=F32)
        if cb < 2:
            parts = []
            for j in range(PROJ_BLK // LANES):
                pj = p[:, j * LANES:(j + 1) * LANES]
                parts.append(pj * cos + pltpu.roll(pj, LANES - ROT_DIMS // 2, axis=1) * sa
                             + pltpu.roll(pj, ROT_DIMS // 2, axis=1) * sb)
            p = jnp.concatenate(parts, axis=1)
        o_ref[:, cb * PROJ_BLK:(cb + 1) * PROJ_BLK] = p.astype(BF16)


def _inproj(x2, shift, scale, w_ext, cos_t, sa_t, sb_t, seq):
    t, d = x2.shape
    n = w_ext.shape[1]
    tm = TM_PROJ
    tpb = seq // tm
    return pl.pallas_call(
        _inproj_body,
        out_shape=jax.ShapeDtypeStruct((t, n), BF16),
        grid=(t // tm,),
        in_specs=[pl.BlockSpec((tm, d), lambda i: (i, 0)),
                  pl.BlockSpec((1, 1, d), lambda i: (i // tpb, 0, 0)),
                  pl.BlockSpec((1, 1, d), lambda i: (i // tpb, 0, 0)),
                  pl.BlockSpec((d, n), lambda i: (0, 0)),
                  pl.BlockSpec((tm, LANES), lambda i: (i % tpb, 0)),
                  pl.BlockSpec((tm, LANES), lambda i: (i % tpb, 0)),
                  pl.BlockSpec((tm, LANES), lambda i: (i % tpb, 0))],
        out_specs=pl.BlockSpec((tm, n), lambda i: (i, 0)),
        compiler_params=_cparams(("arbitrary",)),
        name="inproj",
    )(x2, shift, scale, w_ext, cos_t, sa_t, sb_t)


def _swa_body(sink_ref, q_ref, k_ref, v_ref, g_ref, o_ref, *, seq):
    n = pl.program_id(1)
    blk = WINDOW
    kw_len = 3 * blk
    start = pl.multiple_of(jnp.clip(n * blk - blk, 0, seq - kw_len), blk)
    kw = k_ref[pl.ds(start, kw_len), :]
    vw = v_ref[pl.ds(start, kw_len), :]
    q = q_ref[...]
    gw = A_GROUP * HEAD_DIM
    srows = A_GROUP * blk
    row = lax.broadcasted_iota(I32, (srows, 1), 0)
    qpos = n * blk + row % blk
    kpos = start + lax.broadcasted_iota(I32, (1, kw_len), 1)
    allowed = jnp.abs(qpos - kpos) <= WINDOW
    grp = lax.broadcasted_iota(I32, (1, gw), 1) // HEAD_DIM
    outs = []
    for h in range(A_KV_HEADS):
        qh = q[:, h * gw:(h + 1) * gw]
        kh = kw[:, h * gw:(h + 1) * gw]
        vh = vw[:, h * gw:(h + 1) * gw]
        qm = jnp.concatenate([jnp.where(grp == g, qh, jnp.zeros_like(qh)) for g in range(A_GROUP)], axis=0)
        sk = jnp.full((srows, 1), sink_ref[h * A_GROUP], F32)
        for g in range(1, A_GROUP):
            sk = jnp.where(row // blk == g, sink_ref[h * A_GROUP + g], sk)
        s = lax.dot_general(qm, kh, (((1,), (1,)), ((), ())), preferred_element_type=F32)
        s = jnp.where(allowed, s, NEG)
        m = jnp.maximum(jnp.max(s, axis=-1, keepdims=True), sk)
        p = jnp.exp(s - m)
        denom = jnp.sum(p, axis=-1, keepdims=True) + jnp.exp(sk - m)
        pn = (p / denom).astype(BF16)
        res = jnp.dot(pn, vh, preferred_element_type=F32)
        acc = res[0:blk]
        for g in range(1, A_GROUP):
            acc = jnp.where(grp == g, res[g * blk:(g + 1) * blk], acc)
        outs.append(acc)
    oa = jnp.concatenate(outs, axis=1)
    o_ref[...] = (_rms(oa) * g_ref[...]).astype(BF16)


def _swa(proj, sink, g_a, bsz, seq):
    t = proj.shape[0]
    blk = WINDOW
    nq = seq // blk
    grid_spec = pltpu.PrefetchScalarGridSpec(
        num_scalar_prefetch=1, grid=(bsz, nq),
        in_specs=[pl.BlockSpec((blk, PROJ_BLK), lambda b, n, s: (b * nq + n, 0)),
                  pl.BlockSpec((seq, PROJ_BLK), lambda b, n, s: (b, 1)),
                  pl.BlockSpec((seq, PROJ_BLK), lambda b, n, s: (b, 2)),
                  pl.BlockSpec((1, A_Q_W), lambda b, n, s: (0, 0))],
        out_specs=pl.BlockSpec((blk, A_Q_W), lambda b, n, s: (b * nq + n, 0)))
    return pl.pallas_call(
        functools.partial(_swa_body, seq=seq),
        out_shape=jax.ShapeDtypeStruct((t, A_Q_W), BF16),
        grid_spec=grid_spec,
        compiler_params=_cparams(("arbitrary", "arbitrary")),
        name="swa",
    )(sink, proj, proj, proj, g_a)


def _na_body(q_ref, k_ref, v_ref, bias_ref, g_ref, o_ref, *, rows, kr, rb):
    blk = pl.program_id(1)
    per = LANES // HEAD_DIM
    n_pair = B_W // LANES
    half = lax.broadcasted_iota(I32, (1, LANES), 1) // HEAD_DIM
    gain = g_ref[...]

    def one_row(i, carry):
        r = blk * rb + i
        rs = jnp.clip(r - kr // 2, 0, rows - kr)
        start = pl.multiple_of(rs * GRID_W, GRID_W)
        kw = k_ref[pl.ds(start, kr * GRID_W), :]
        vw = v_ref[pl.ds(start, kr * GRID_W), :]
        q = q_ref[pl.ds(pl.multiple_of(i * GRID_W, GRID_W), GRID_W), :]
        s_parts = []
        for pr in range(n_pair):
            qp = q[:, pr * LANES:(pr + 1) * LANES]
            qm = jnp.concatenate([jnp.where(half == j, qp, jnp.zeros_like(qp)) for j in range(per)], axis=0)
            s_parts.append(lax.dot_general(qm, kw[:, pr * LANES:(pr + 1) * LANES],
                                           (((1,), (1,)), ((), ())), preferred_element_type=F32))
        s = jnp.concatenate(s_parts, axis=0) + bias_ref[r - rs]
        m = jnp.max(s, axis=-1, keepdims=True)
        p = jnp.exp(s - m)
        pn = (p / jnp.sum(p, axis=-1, keepdims=True)).astype(BF16)
        o_parts = []
        for pr in range(n_pair):
            res = jnp.dot(pn[pr * per * GRID_W:(pr + 1) * per * GRID_W], vw[:, pr * LANES:(pr + 1) * LANES],
                          preferred_element_type=F32)
            op = res[0:GRID_W]
            for j in range(1, per):
                op = jnp.where(half == j, res[j * GRID_W:(j + 1) * GRID_W], op)
            o_parts.append(op)
        ob = jnp.concatenate(o_parts, axis=1)
        o_ref[pl.ds(pl.multiple_of(i * GRID_W, GRID_W), GRID_W), :] = (_rms(ob) * gain).astype(BF16)
        return carry

    def row_group(ig, carry):
        for u in range(NA_ROW_UNROLL):
            one_row(NA_ROW_UNROLL * ig + u, carry)
        return carry

    lax.fori_loop(0, rb // NA_ROW_UNROLL, row_group, 0)


def _natten(proj, bias_t, g_b, bsz, seq):
    t = proj.shape[0]
    rows = seq // GRID_W
    kr = min(NA_ROWS_MAX, rows)
    rb = NA_ROWS_PER_STEP
    nb = rows // rb
    return pl.pallas_call(
        functools.partial(_na_body, rows=rows, kr=kr, rb=rb),
        out_shape=jax.ShapeDtypeStruct((t, B_W), BF16),
        grid=(bsz, nb),
        in_specs=[pl.BlockSpec((rb * GRID_W, PROJ_BLK), lambda b, r: (b * nb + r, 3)),
                  pl.BlockSpec((seq, PROJ_BLK), lambda b, r: (b, 4)),
                  pl.BlockSpec((seq, PROJ_BLK), lambda b, r: (b, 5)),
                  pl.BlockSpec((kr, B_HEADS * GRID_W, kr * GRID_W), lambda b, r: (0, 0, 0)),
                  pl.BlockSpec((1, B_W), lambda b, r: (0, 0))],
        out_specs=pl.BlockSpec((rb * GRID_W, B_W), lambda b, r: (b * nb + r, 0)),
        compiler_params=_cparams(("arbitrary", "arbitrary")),
        name="natten",
    )(proj, proj, proj, bias_t, g_b)


def _na_bias_table(rpb, rows):
    kr = min(NA_ROWS_MAX, rows)
    kc = NA_COLS
    col = jnp.arange(GRID_W)
    col_start = jnp.clip(col - kc // 2, 0, GRID_W - kc)
    col_mask = (col[None, :] >= col_start[:, None]) & (col[None, :] < col_start[:, None] + kc)
    col_off = jnp.clip(col[None, :] - col[:, None], -(kc - 1), kc - 1) + (NA_COLS - 1)
    rows_sel = jnp.stack([rpb[:, NA_ROWS_MAX - 1 - cs:NA_ROWS_MAX - 1 - cs + kr, :]
                          for cs in range(kr)], axis=0)
    pick = (col_off[None, :, :] == jnp.arange(2 * kc - 1)[:, None, None]).astype(F32)
    tbl = jnp.einsum('chjr,rqk->chqjk', rows_sel, pick, precision=lax.Precision.HIGHEST)
    tbl = jnp.where(col_mask[None, None, :, None, :], tbl, NEG)
    return tbl.reshape(kr, rpb.shape[0] * GRID_W, kr * GRID_W).astype(F32)


def _outproj_body(x_ref, ma_ref, mb_ref, wo_ref, gm_ref, sf_ref, cf_ref, wr_ref, br_ref,
                  x1_ref, hs_ref, pg_ref, pgt_ref, tc_ref, to_ref):
    tm = x_ref.shape[0]
    n_pos = TOP_K * tm
    mix = (jnp.dot(ma_ref[...], wo_ref[:A_Q_W, :], preferred_element_type=F32)
           + jnp.dot(mb_ref[...], wo_ref[A_Q_W:, :], preferred_element_type=F32))
    x1 = x_ref[...] + gm_ref[0] * mix
    x1_ref[...] = x1
    h = _rms(x1) * (1.0 + cf_ref[0]) + sf_ref[0]
    h_hi = h.astype(BF16)
    h_lo = (h - h_hi.astype(F32)).astype(BF16)
    l_hi = jnp.dot(h_hi, wr_ref[...], preferred_element_type=F32)
    logits = (l_hi[:, :LANES] + l_hi[:, LANES:]
              + jnp.dot(h_lo, wr_ref[:, :LANES], preferred_element_type=F32) + br_ref[...])
    lane = lax.broadcasted_iota(I32, (tm, LANES), 1)
    work = logits
    vals, idxs = [], []
    for _ in range(TOP_K):
        m = jnp.max(work, axis=-1, keepdims=True)
        ik = jnp.min(jnp.where(work == m, lane, LANES), axis=-1, keepdims=True)
        vals.append(m)
        idxs.append(ik)
        work = jnp.where(lane == ik, -jnp.inf, work)
    es = [jnp.exp(v - vals[0]) for v in vals]
    den = es[0] + es[1] + es[2] + es[3]
    mh = jnp.zeros((tm, LANES), F32)
    for ik in idxs:
        mh = mh + (lane == ik).astype(F32)
    mhb = mh.astype(BF16)
    tri = (lax.broadcasted_iota(I32, (tm, tm), 0) > lax.broadcasted_iota(I32, (tm, tm), 1))
    earlier = jnp.dot(tri.astype(BF16), mhb, preferred_element_type=F32)
    lower = (lax.broadcasted_iota(I32, (LANES, LANES), 0) < lax.broadcasted_iota(I32, (LANES, LANES), 1))
    below = jnp.dot(mhb, lower.astype(BF16), preferred_element_type=F32)
    toff = jnp.sum(below, axis=0, keepdims=True)
    cnt = jnp.sum(mh, axis=0, keepdims=True)
    posf = earlier + toff
    pg = jnp.zeros((tm, LANES), F32)
    for k in range(TOP_K):
        pos_k = jnp.sum(jnp.where(lane == idxs[k], posf, 0.0), axis=-1, keepdims=True)
        pg = jnp.where(lane == k, pos_k, pg)
        pg = jnp.where(lane == TOP_K + k, es[k] / den, pg)
    pg_ref[...] = pg
    pgt = pg.T[:ROW_TILE]
    pgt_ref[0] = pgt
    tc_ref[0] = cnt
    to_ref[0] = toff
    pos_iota = lax.broadcasted_iota(I32, (n_pos, tm), 0)
    hit = pos_iota == pgt[0:1, :].astype(I32)
    for k in range(1, TOP_K):
        hit = hit | (pos_iota == pgt[k:k + 1, :].astype(I32))
    srt = jnp.dot(jnp.where(hit, 1.0, 0.0).astype(BF16), h_hi, preferred_element_type=F32)
    for c in range(ROW_TILE):
        hs_ref[pl.ds(c, n_pos, stride=ROW_TILE), :] = srt[:, c * LANES:(c + 1) * LANES]


def _outproj(x2, mix_a, mix_b, w_out, gate_m, shift_f, scale_f, w_r, b_r, seq):
    t, d = x2.shape
    tm = SORT_TM
    nt = t // tm
    tpb = seq // tm
    n_pos = TOP_K * tm
    mod_spec = pl.BlockSpec((1, 1, d), lambda i: (i // tpb, 0, 0))
    return pl.pallas_call(
        _outproj_body,
        out_shape=(jax.ShapeDtypeStruct((t, d), F32),
                   jax.ShapeDtypeStruct((t * TOP_K * ROW_TILE, LANES), F32),
                   jax.ShapeDtypeStruct((t, LANES), F32),
                   jax.ShapeDtypeStruct((nt, ROW_TILE, tm), F32),
                   jax.ShapeDtypeStruct((nt, 1, LANES), F32),
                   jax.ShapeDtypeStruct((nt, 1, LANES), F32)),
        grid=(nt,),
        in_specs=[pl.BlockSpec((tm, d), lambda i: (i, 0)),
                  pl.BlockSpec((tm, A_Q_W), lambda i: (i, 0)),
                  pl.BlockSpec((tm, B_W), lambda i: (i, 0)),
                  pl.BlockSpec((A_Q_W + B_W, d), lambda i: (0, 0)),
                  mod_spec, mod_spec, mod_spec,
                  pl.BlockSpec((d, 2 * LANES), lambda i: (0, 0)),
                  pl.BlockSpec((1, LANES), lambda i: (0, 0))],
        out_specs=(pl.BlockSpec((tm, d), lambda i: (i, 0)),
                   pl.BlockSpec((n_pos * ROW_TILE, LANES), lambda i: (i, 0)),
                   pl.BlockSpec((tm, LANES), lambda i: (i, 0)),
                   pl.BlockSpec((1, ROW_TILE, tm), lambda i: (i, 0, 0)),
                   pl.BlockSpec((1, 1, LANES), lambda i: (i, 0, 0)),
                   pl.BlockSpec((1, 1, LANES), lambda i: (i, 0, 0))),
        compiler_params=_cparams(("arbitrary",)),
        name="outproj",
    )(x2, mix_a, mix_b, w_out, gate_m, shift_f, scale_f, w_r, b_r)


def _segment_copies(seg_ref, n_seg, first, last, src_hbm, dst_buf, sem):
    def body(i, carry):
        size = pl.multiple_of(seg_ref[0, 0, 2 * n_seg + i], ROW_TILE)

        @pl.when(size > 0)
        def _():
            src = pl.multiple_of(seg_ref[0, 0, i], ROW_TILE)
            dst = pl.multiple_of(seg_ref[0, 0, n_seg + i], ROW_TILE)
            pltpu.make_async_copy(src_hbm.at[pl.ds(src, size), :], dst_buf.at[pl.ds(dst, size), :], sem).start()
        return carry

    lax.fori_loop(first, last, body, 0)


def _experts_body(be_ref, vl_ref, seg0_ref, segn_ref, hs_hbm, wgu_ref, bgu_ref, wd_ref, bd_ref, o_ref,
                  xbuf, xs_ref, gsem, *, n_blk, n_seg):
    bm = MOE_BM
    rt = ROW_TILE
    d_ff = wd_ref.shape[1]
    j = pl.program_id(0)
    slot = j % 2
    other = 1 - slot
    valid = vl_ref[j]

    def issue(seg_ref, s):
        _segment_copies(seg_ref, n_seg, seg_ref[0, 0, 3 * n_seg], seg_ref[0, 0, 3 * n_seg + 1],
                        hs_hbm, xbuf.at[s], gsem.at[s])

    @pl.when(j == 0)
    def _():
        xbuf[...] = jnp.zeros_like(xbuf)
        issue(seg0_ref, 0)

    @pl.when(valid > 0)
    def _():
        rows = pl.multiple_of(valid * rt, rt)
        pltpu.make_async_copy(hs_hbm.at[pl.ds(0, rows), :], xbuf.at[slot, pl.ds(0, rows), :],
                              gsem.at[slot]).wait()

    @pl.when(j + 1 < n_blk)
    def _():
        issue(segn_ref, other)

    @pl.when(valid == 0)
    def _():
        o_ref[...] = jnp.zeros_like(o_ref)

    @pl.when(valid > 0)
    def _():
        xs_ref[...] = jnp.concatenate(
            [xbuf[slot, pl.ds(c, bm, stride=rt), :] for c in range(rt)], axis=1).astype(BF16)
        xs = xs_ref[...]
        acts = []
        n_chunk = 4
        cw = d_ff // n_chunk
        for c in range(n_chunk):
            gate = jnp.dot(xs, wgu_ref[0, :, c * cw:(c + 1) * cw], preferred_element_type=F32)
            gate = gate + bgu_ref[0, :, c * cw:(c + 1) * cw]
            up = jnp.dot(xs, wgu_ref[0, :, d_ff + c * cw:d_ff + (c + 1) * cw], preferred_element_type=F32)
            up = up + bgu_ref[0, :, d_ff + c * cw:d_ff + (c + 1) * cw]
            gate = jnp.minimum(gate, SWIGLU_LIMIT)
            up = jnp.clip(up, -SWIGLU_LIMIT, SWIGLU_LIMIT)
            glu = gate / (1.0 + jnp.exp(-SWIGLU_ALPHA * gate))
            acts.append(((up + 1.0) * glu).astype(BF16))
        act = jnp.concatenate(acts, axis=1)
        ys = jnp.dot(act, wd_ref[0], preferred_element_type=F32) + bd_ref[0]
        for c in range(rt):
            o_ref[pl.ds(c, bm, stride=rt), :] = ys[:, c * LANES:(c + 1) * LANES]


def _experts(blk_expert, blk_valid, seg, hs, wgu, bgu, wd, bd, n_seg):
    bm = MOE_BM
    rt = ROW_TILE
    n_blk, _, seg_w = seg.shape
    d = wd.shape[2]
    d_ff = wd.shape[1]
    assert d == rt * LANES and hs.shape[1] == LANES
    smem = functools.partial(pl.BlockSpec, (1, 1, seg_w), memory_space=pltpu.SMEM)
    last = n_blk - 1
    grid_spec = pltpu.PrefetchScalarGridSpec(
        num_scalar_prefetch=2, grid=(n_blk,),
        in_specs=[smem(lambda j, be, vl: (j, 0, 0)),
                  smem(lambda j, be, vl: (jnp.minimum(j + 1, last), 0, 0)),
                  pl.BlockSpec(memory_space=pl.ANY),
                  pl.BlockSpec((1, d, 2 * d_ff), lambda j, be, vl: (be[j], 0, 0)),
                  pl.BlockSpec((1, 1, 2 * d_ff), lambda j, be, vl: (be[j], 0, 0)),
                  pl.BlockSpec((1, d_ff, d), lambda j, be, vl: (be[j], 0, 0)),
                  pl.BlockSpec((1, 1, d), lambda j, be, vl: (be[j], 0, 0))],
        out_specs=pl.BlockSpec((bm * rt, LANES), lambda j, be, vl: (j, 0)),
        scratch_shapes=[pltpu.VMEM((2, bm * rt, LANES), F32), pltpu.VMEM((bm, d), BF16),
                        pltpu.SemaphoreType.DMA((2,))])
    return pl.pallas_call(
        functools.partial(_experts_body, n_blk=n_blk, n_seg=n_seg),
        out_shape=jax.ShapeDtypeStruct((n_blk * bm * rt, LANES), F32),
        grid_spec=grid_spec,
        compiler_params=_cparams(("arbitrary",)),
        name="experts",
    )(blk_expert, blk_valid, seg, seg, hs, wgu, bgu, wd, bd)


def _combine_body(seg0_ref, segn_ref, x1_ref, pg_ref, pgt_ref, gf_ref, gfin_ref, ys_hbm, o_ref,
                  ybuf, sem, *, nt):
    tm = x1_ref.shape[0]
    n_pos = TOP_K * tm
    rt = ROW_TILE
    i = pl.program_id(0)
    slot = i % 2
    other = 1 - slot

    def issue(seg_ref, s):
        _segment_copies(seg_ref, N_EXPERTS, 0, N_EXPERTS, ys_hbm, ybuf.at[s], sem.at[s])

    @pl.when(i == 0)
    def _():
        issue(seg0_ref, 0)

    pltpu.make_async_copy(ys_hbm.at[pl.ds(0, n_pos * rt), :], ybuf.at[slot], sem.at[slot]).wait()

    @pl.when(i + 1 < nt)
    def _():
        issue(segn_ref, other)

    pg = pg_ref[...]
    pgt = pgt_ref[0]
    pos_iota = lax.broadcasted_iota(I32, (n_pos, tm), 0)
    gs = jnp.zeros((n_pos, 1), F32)
    for k in range(TOP_K):
        own = pos_iota == pgt[k:k + 1, :].astype(I32)
        gs = gs + jnp.sum(jnp.where(own, pgt[TOP_K + k:TOP_K + k + 1, :], 0.0), axis=-1, keepdims=True)
    lane_pos = lax.broadcasted_iota(I32, (tm, n_pos), 1)
    hit = lane_pos == pg[:, 0:1].astype(I32)
    for k in range(1, TOP_K):
        hit = hit | (lane_pos == pg[:, k:k + 1].astype(I32))
    ysg = jnp.concatenate([ybuf[slot, pl.ds(c, n_pos, stride=rt), :] for c in range(rt)], axis=1) * gs
    y = jnp.dot(jnp.where(hit, 1.0, 0.0).astype(BF16), ysg.astype(BF16), preferred_element_type=F32)
    x2 = x1_ref[...] + gf_ref[0] * y
    o_ref[...] = _rms(x2) * gfin_ref[...]


def _combine(seg, x1, pg, pgt, gate_f, g_final, ys, seq):
    t, d = x1.shape
    tm = SORT_TM
    tpb = seq // tm
    nt = t // tm
    n_pos = TOP_K * tm
    seg_w = seg.shape[2]
    smem = functools.partial(pl.BlockSpec, (1, 1, seg_w), memory_space=pltpu.SMEM)
    return pl.pallas_call(
        functools.partial(_combine_body, nt=nt),
        out_shape=jax.ShapeDtypeStruct((t, d), F32),
        grid=(nt,),
        in_specs=[smem(lambda i: (i, 0, 0)),
                  smem(lambda i: (jnp.minimum(i + 1, nt - 1), 0, 0)),
                  pl.BlockSpec((tm, d), lambda i: (i, 0)),
                  pl.BlockSpec((tm, LANES), lambda i: (i, 0)),
                  pl.BlockSpec((1, ROW_TILE, tm), lambda i: (i, 0, 0)),
                  pl.BlockSpec((1, 1, d), lambda i: (i // tpb, 0, 0)),
                  pl.BlockSpec((1, d), lambda i: (0, 0)),
                  pl.BlockSpec(memory_space=pl.ANY)],
        out_specs=pl.BlockSpec((tm, d), lambda i: (i, 0)),
        scratch_shapes=[pltpu.VMEM((2, n_pos * ROW_TILE, LANES), F32), pltpu.SemaphoreType.DMA((2,))],
        compiler_params=_cparams(("arbitrary",)),
        name="combine",
    )(seg, seg, x1, pg, pgt, gate_f, g_final, ys)


def _rope_tables(seq):
    half = ROT_DIMS // 2
    pos = jnp.arange(seq, dtype=F32)
    inv_freq = ROPE_THETA ** (-jnp.arange(0, ROT_DIMS, 2, dtype=F32) / ROT_DIMS)
    ang = pos[:, None] * inv_freq[None, :]
    cos, sin = jnp.cos(ang), jnp.sin(ang)
    ones = jnp.ones((seq, HEAD_DIM - ROT_DIMS), F32)
    zeros = jnp.zeros((seq, HEAD_DIM - ROT_DIMS), F32)
    zh = jnp.zeros((seq, half), F32)
    rep = LANES // HEAD_DIM
    cos_t = jnp.tile(jnp.concatenate([cos, cos, ones], axis=1), (1, rep))
    sa_t = jnp.tile(jnp.concatenate([-sin, zh, zeros], axis=1), (1, rep))
    sb_t = jnp.tile(jnp.concatenate([zh, sin, zeros], axis=1), (1, rep))
    return cos_t, sa_t, sb_t


def _widen_in_proj(w_in):
    d = w_in.shape[0]
    scale = HEAD_DIM ** -0.5
    o = 0
    wqa = w_in[:, o:o + A_Q_W] * scale; o += A_Q_W
    wka = w_in[:, o:o + A_KV_W]; o += A_KV_W
    wva = w_in[:, o:o + A_KV_W]; o += A_KV_W
    wqb = w_in[:, o:o + B_W] * scale; o += B_W
    wkb = w_in[:, o:o + B_W]; o += B_W
    wvb = w_in[:, o:o + B_W]

    def widen(w):
        w = w.reshape(d, A_KV_HEADS, 1, HEAD_DIM)
        return jnp.broadcast_to(w, (d, A_KV_HEADS, A_GROUP, HEAD_DIM)).reshape(d, A_Q_W)

    return jnp.concatenate([wqa, widen(wka), widen(wva), wqb, wkb, wvb], axis=1).astype(BF16)


def _round_up(n, m):
    return (n + m - 1) // m * m


def _moe_tables(tile_cnt, tile_off, n_blk):
    bm = MOE_BM
    nt = tile_cnt.shape[0]
    n_pos = TOP_K * SORT_TM
    cum = jnp.cumsum(tile_cnt, axis=0) - tile_cnt
    counts = jnp.sum(tile_cnt, axis=0)
    padded = ((counts + bm - 1) // bm) * bm
    pends = jnp.cumsum(padded)
    pstarts = pends - padded
    blk_first = jnp.arange(n_blk, dtype=I32) * bm
    blk_expert = jnp.minimum(
        jnp.sum((pends[None, :] <= blk_first[:, None]).astype(I32), axis=1), N_EXPERTS - 1)
    r0 = blk_first - pstarts[blk_expert]
    blk_valid = jnp.clip(counts[blk_expert] - r0, 0, bm)
    cum_b = cum.T[blk_expert]
    cnt_b = tile_cnt.T[blk_expert]
    off_b = tile_off.T[blk_expert]
    lo = jnp.maximum(r0[:, None], cum_b)
    hi = jnp.minimum(r0[:, None] + bm, cum_b + cnt_b)
    size = jnp.maximum(hi - lo, 0)
    tile_ids = jnp.arange(nt, dtype=I32)[None, :]
    src = tile_ids * n_pos + off_b + (lo - cum_b)
    dst = jnp.where(size > 0, lo - r0[:, None], 0)
    first = jnp.min(jnp.where(size > 0, tile_ids, nt), axis=1)
    last = jnp.max(jnp.where(size > 0, tile_ids + 1, 0), axis=1)
    first = jnp.minimum(first, last)
    seg_w = _round_up(3 * nt + 2, LANES)
    exp_seg = jnp.concatenate(
        [src * ROW_TILE, dst * ROW_TILE, size * ROW_TILE, first[:, None], last[:, None],
         jnp.zeros((n_blk, seg_w - 3 * nt - 2), I32)], axis=1).reshape(n_blk, 1, seg_w)
    comb_seg = jnp.concatenate(
        [(pstarts[None, :] + cum) * ROW_TILE, tile_off * ROW_TILE, tile_cnt * ROW_TILE,
         jnp.zeros((nt, LANES - 3 * N_EXPERTS), I32)], axis=1).reshape(nt, 1, LANES)
    return blk_expert, blk_valid, exp_seg, comb_seg


def kernel(x, c, w_ada, b_ada, w_in, sink, rpb, g_out_a, g_out_b, w_out, w_router, b_router,
           w_gate_up, b_gate_up, w_down, b_down, g_final):
    bsz, seq, d = x.shape
    t = bsz * seq
    depth = w_ada.shape[0]
    assert depth == 1, "the final norm is fused into the layer combine"
    bm = MOE_BM
    n_blk = (t * TOP_K) // bm + N_EXPERTS
    nt = t // SORT_TM

    cos_t, sa_t, sb_t = _rope_tables(seq)
    x2 = x.reshape(t, d)
    for l in range(depth):
        mod = _ada(c, w_ada[l], b_ada[l][None, :])
        shift_m, scale_m, gate_m, shift_f, scale_f, gate_f = [
            m.reshape(bsz, 1, d) for m in jnp.split(mod, 6, axis=-1)]

        proj = _inproj(x2, shift_m, scale_m, _widen_in_proj(w_in[l]), cos_t, sa_t, sb_t, seq)
        mix_a = _swa(proj, sink[l], g_out_a[l][None, :], bsz, seq)
        mix_b = _natten(proj, _na_bias_table(rpb[l], seq // GRID_W), g_out_b[l][None, :], bsz, seq)

        w_r = jnp.pad(w_router[l], ((0, 0), (0, LANES - N_EXPERTS)))
        w_r_hi = w_r.astype(BF16)
        w_r_lo = (w_r - w_r_hi.astype(F32)).astype(BF16)
        b_r = jnp.pad(b_router[l], (0, LANES - N_EXPERTS), constant_values=NEG)[None, :]
        x1, hs, pg, pgt, tile_cnt, tile_off = _outproj(
            x2, mix_a, mix_b, w_out[l].astype(BF16), gate_m, shift_f, scale_f,
            jnp.concatenate([w_r_hi, w_r_lo], axis=1), b_r, seq)

        blk_expert, blk_valid, exp_seg, comb_seg = _moe_tables(
            tile_cnt[:, 0, :N_EXPERTS].astype(I32), tile_off[:, 0, :N_EXPERTS].astype(I32), n_blk)
        ys = _experts(blk_expert, blk_valid, exp_seg, hs,
                      w_gate_up[l].astype(BF16), b_gate_up[l][:, None, :],
                      w_down[l].astype(BF16), b_down[l][:, None, :], nt)
        x2 = _combine(comb_seg, x1, pg, pgt, gate_f, g_final[None, :], ys, seq)
    return x2.reshape(bsz, seq, d)
```

```python
import functools

import jax
import jax.numpy as jnp
from jax import lax
from jax.experimental import pallas as pl
from jax.experimental.pallas import tpu as pltpu

F32 = jnp.float32
BF16 = jnp.bfloat16
I32 = jnp.int32

HEAD_DIM = 64
A_Q_HEADS = 8
A_KV_HEADS = 2
A_GROUP = A_Q_HEADS // A_KV_HEADS
B_HEADS = 8
A_Q_W = A_Q_HEADS * HEAD_DIM
A_KV_W = A_KV_HEADS * HEAD_DIM
B_W = B_HEADS * HEAD_DIM
WINDOW = 128
ROT_DIMS = HEAD_DIM // 4
ROPE_THETA = 500000.0
GRID_W = 64
NA_ROWS_MAX = 8
NA_COLS = 16
N_EXPERTS = 32
TOP_K = 4
SWIGLU_LIMIT = 7.0
SWIGLU_ALPHA = 1.702
EPS = 1e-5
NEG = -1e30

LANES = 128
ROW_TILE = 8
PROJ_BLK = 512
TM_PROJ = 512
SORT_TM = 256
SORT_TILES_PER_STEP = 2
MOE_BM = 512
SWA_BLOCKS_PER_STEP = 2
NA_ROWS_PER_STEP = 8
NA_ROW_UNROLL = 4
VMEM_LIMIT = 48 << 20
VMEM_LIMIT_EXPERTS = 58 << 20


def _cparams(sem, limit=VMEM_LIMIT):
    return pltpu.CompilerParams(dimension_semantics=sem, vmem_limit_bytes=limit)


def _rms(x):
    return x * lax.rsqrt(jnp.mean(x * x, axis=-1, keepdims=True) + EPS)


def _ada_body(c_ref, w_ref, b_ref, o_ref):
    c = c_ref[...]
    ca = c / (1.0 + jnp.exp(-c))
    o_ref[...] = jnp.dot(ca, w_ref[...], preferred_element_type=F32,
                         precision=lax.Precision.HIGHEST) + b_ref[...]


def _ada(c, w, b):
    bsz, d = c.shape
    n = w.shape[1]
    bn = n // 4
    return pl.pallas_call(
        _ada_body,
        out_shape=jax.ShapeDtypeStruct((bsz, n), F32),
        grid=(4,),
        in_specs=[pl.BlockSpec((bsz, d), lambda i: (0, 0)),
                  pl.BlockSpec((d, bn), lambda i: (0, i)),
                  pl.BlockSpec((1, bn), lambda i: (0, i))],
        out_specs=pl.BlockSpec((bsz, bn), lambda i: (0, i)),
        compiler_params=_cparams(("arbitrary",)),
        name="ada",
    )(c, w, b)


def _inproj_body(x_ref, sh_ref, sc_ref, w_ref, cos_ref, sa_ref, sb_ref, o_ref):
    h = (_rms(x_ref[...]) * (1.0 + sc_ref[0]) + sh_ref[0]).astype(BF16)
    cos = cos_ref[...]
    sa = sa_ref[...]
    sb = sb_ref[...]
    n_blk = w_ref.shape[1] // PROJ_BLK
    for cb in range(n_blk):
        p = jnp.dot(h, w_ref[:, cb * PROJ_BLK:(cb + 1) * PROJ_BLK], preferred_element_type=F32)
        if cb < 2:
            parts = []
            for j in range(PROJ_BLK // LANES):
                pj = p[:, j * LANES:(j + 1) * LANES]
                parts.append(pj * cos + pltpu.roll(pj, LANES - ROT_DIMS // 2, axis=1) * sa
                             + pltpu.roll(pj, ROT_DIMS // 2, axis=1) * sb)
            p = jnp.concatenate(parts, axis=1)
        o_ref[:, cb * PROJ_BLK:(cb + 1) * PROJ_BLK] = p.astype(BF16)


def _inproj(x2, shift, scale, w_ext, cos_t, sa_t, sb_t, seq):
    t, d = x2.shape
    n = w_ext.shape[1]
    tm = TM_PROJ
    tpb = seq // tm
    return pl.pallas_call(
        _inproj_body,
        out_shape=jax.ShapeDtypeStruct((t, n), BF16),
        grid=(t // tm,),
        in_specs=[pl.BlockSpec((tm, d), lambda i: (i, 0)),
                  pl.BlockSpec((1, 1, d), lambda i: (i // tpb, 0, 0)),
                  pl.BlockSpec((1, 1, d), lambda i: (i // tpb, 0, 0)),
                  pl.BlockSpec((d, n), lambda i: (0, 0)),
                  pl.BlockSpec((tm, LANES), lambda i: (i % tpb, 0)),
                  pl.BlockSpec((tm, LANES), lambda i: (i % tpb, 0)),
                  pl.BlockSpec((tm, LANES), lambda i: (i % tpb, 0))],
        out_specs=pl.BlockSpec((tm, n), lambda i: (i, 0)),
        compiler_params=_cparams(("arbitrary",)),
        name="inproj",
    )(x2, shift, scale, w_ext, cos_t, sa_t, sb_t)


def _swa_body(sink_ref, q_ref, k_ref, v_ref, g_ref, o_ref, *, seq):
    step = pl.program_id(1)
    blk = WINDOW
    kw_len = 3 * blk
    gw = A_GROUP * HEAD_DIM
    srows = A_GROUP * blk
    row = lax.broadcasted_iota(I32, (srows, 1), 0)
    grp = lax.broadcasted_iota(I32, (1, gw), 1) // HEAD_DIM
    blocks = range(SWA_BLOCKS_PER_STEP)
    units = [(bi, h) for bi in blocks for h in range(A_KV_HEADS)]
    kws, vws, alloweds = [], [], []
    for bi in blocks:
        n = step * SWA_BLOCKS_PER_STEP + bi
        start = pl.multiple_of(jnp.clip(n * blk - blk, 0, seq - kw_len), blk)
        kws.append(k_ref[pl.ds(start, kw_len), :])
        vws.append(v_ref[pl.ds(start, kw_len), :])
        qpos = n * blk + row % blk
        kpos = start + lax.broadcasted_iota(I32, (1, kw_len), 1)
        alloweds.append(jnp.abs(qpos - kpos) <= WINDOW)
    sks = []
    for h in range(A_KV_HEADS):
        sk = jnp.full((srows, 1), sink_ref[h * A_GROUP], F32)
        for g in range(1, A_GROUP):
            sk = jnp.where(row // blk == g, sink_ref[h * A_GROUP + g], sk)
        sks.append(sk)
    qms = []
    for bi, h in units:
        qh = q_ref[bi * blk:(bi + 1) * blk, h * gw:(h + 1) * gw]
        qms.append(jnp.concatenate(
            [jnp.where(grp == g, qh, jnp.zeros_like(qh)) for g in range(A_GROUP)], axis=0))
    ss = [lax.dot_general(qms[u], kws[bi][:, h * gw:(h + 1) * gw], (((1,), (1,)), ((), ())),
                          preferred_element_type=F32) for u, (bi, h) in enumerate(units)]
    ss = [jnp.where(alloweds[bi], ss[u], NEG) for u, (bi, h) in enumerate(units)]
    ms = [jnp.maximum(jnp.max(ss[u], axis=-1, keepdims=True), sks[h]) for u, (bi, h) in enumerate(units)]
    ps = [jnp.exp(ss[u] - ms[u]) for u in range(len(units))]
    invs = [1.0 / (jnp.sum(ps[u], axis=-1, keepdims=True) + jnp.exp(sks[h] - ms[u]))
            for u, (bi, h) in enumerate(units)]
    pns = [(ps[u] * invs[u]).astype(BF16) for u in range(len(units))]
    ress = [jnp.dot(pns[u], vws[bi][:, h * gw:(h + 1) * gw], preferred_element_type=F32)
            for u, (bi, h) in enumerate(units)]
    accs = []
    for res in ress:
        acc = res[0:blk]
        for g in range(1, A_GROUP):
            acc = jnp.where(grp == g, res[g * blk:(g + 1) * blk], acc)
        accs.append(acc)
    for bi in blocks:
        oa = jnp.concatenate(accs[bi * A_KV_HEADS:(bi + 1) * A_KV_HEADS], axis=1)
        o_ref[bi * blk:(bi + 1) * blk, :] = (_rms(oa) * g_ref[...]).astype(BF16)


def _swa(proj, sink, g_a, bsz, seq):
    t = proj.shape[0]
    blk = WINDOW * SWA_BLOCKS_PER_STEP
    nq = seq // blk
    grid_spec = pltpu.PrefetchScalarGridSpec(
        num_scalar_prefetch=1, grid=(bsz, nq),
        in_specs=[pl.BlockSpec((blk, PROJ_BLK), lambda b, n, s: (b * nq + n, 0)),
                  pl.BlockSpec((seq, PROJ_BLK), lambda b, n, s: (b, 1)),
                  pl.BlockSpec((seq, PROJ_BLK), lambda b, n, s: (b, 2)),
                  pl.BlockSpec((1, A_Q_W), lambda b, n, s: (0, 0))],
        out_specs=pl.BlockSpec((blk, A_Q_W), lambda b, n, s: (b * nq + n, 0)))
    return pl.pallas_call(
        functools.partial(_swa_body, seq=seq),
        out_shape=jax.ShapeDtypeStruct((t, A_Q_W), BF16),
        grid_spec=grid_spec,
        compiler_params=_cparams(("arbitrary", "arbitrary")),
        name="swa",
    )(sink, proj, proj, proj, g_a)


def _na_body(q_ref, k_ref, v_ref, bias_ref, g_ref, o_ref, *, rows, kr, rb):
    blk = pl.program_id(1)
    per = LANES // HEAD_DIM
    n_pair = B_W // LANES
    half = lax.broadcasted_iota(I32, (1, LANES), 1) // HEAD_DIM
    gain = g_ref[...]

    def one_row(i, carry):
        r = blk * rb + i
        rs = jnp.clip(r - kr // 2, 0, rows - kr)
        start = pl.multiple_of(rs * GRID_W, GRID_W)
        kw = k_ref[pl.ds(start, kr * GRID_W), :]
        vw = v_ref[pl.ds(start, kr * GRID_W), :]
        q = q_ref[pl.ds(pl.multiple_of(i * GRID_W, GRID_W), GRID_W), :]
        s_parts = []
        for pr in range(n_pair):
            qp = q[:, pr * LANES:(pr + 1) * LANES]
            qm = jnp.concatenate([jnp.where(half == j, qp, jnp.zeros_like(qp)) for j in range(per)], axis=0)
            s_parts.append(lax.dot_general(qm, kw[:, pr * LANES:(pr + 1) * LANES],
                                           (((1,), (1,)), ((), ())), preferred_element_type=F32))
        s = jnp.concatenate(s_parts, axis=0) + bias_ref[r - rs]
        m = jnp.max(s, axis=-1, keepdims=True)
        p = jnp.exp(s - m)
        pn = (p * (1.0 / jnp.sum(p, axis=-1, keepdims=True))).astype(BF16)
        o_parts = []
        for pr in range(n_pair):
            res = jnp.dot(pn[pr * per * GRID_W:(pr + 1) * per * GRID_W], vw[:, pr * LANES:(pr + 1) * LANES],
                          preferred_element_type=F32)
            op = res[0:GRID_W]
            for j in range(1, per):
                op = jnp.where(half == j, res[j * GRID_W:(j + 1) * GRID_W], op)
            o_parts.append(op)
        ob = jnp.concatenate(o_parts, axis=1)
        o_ref[pl.ds(pl.multiple_of(i * GRID_W, GRID_W), GRID_W), :] = (_rms(ob) * gain).astype(BF16)
        return carry

    def row_group(ig, carry):
        for u in range(NA_ROW_UNROLL):
            one_row(NA_ROW_UNROLL * ig + u, carry)
        return carry

    lax.fori_loop(0, rb // NA_ROW_UNROLL, row_group, 0)


def _natten(proj, bias_t, g_b, bsz, seq):
    t = proj.shape[0]
    rows = seq // GRID_W
    kr = min(NA_ROWS_MAX, rows)
    rb = NA_ROWS_PER_STEP
    nb = rows // rb
    return pl.pallas_call(
        functools.partial(_na_body, rows=rows, kr=kr, rb=rb),
        out_shape=jax.ShapeDtypeStruct((t, B_W), BF16),
        grid=(bsz, nb),
        in_specs=[pl.BlockSpec((rb * GRID_W, PROJ_BLK), lambda b, r: (b * nb + r, 3)),
                  pl.BlockSpec((seq, PROJ_BLK), lambda b, r: (b, 4)),
                  pl.BlockSpec((seq, PROJ_BLK), lambda b, r: (b, 5)),
                  pl.BlockSpec((kr, B_HEADS * GRID_W, kr * GRID_W), lambda b, r: (0, 0, 0)),
                  pl.BlockSpec((1, B_W), lambda b, r: (0, 0))],
        out_specs=pl.BlockSpec((rb * GRID_W, B_W), lambda b, r: (b * nb + r, 0)),
        compiler_params=_cparams(("arbitrary", "arbitrary")),
        name="natten",
    )(proj, proj, proj, bias_t, g_b)


def _na_bias_table(rpb, rows):
    kr = min(NA_ROWS_MAX, rows)
    kc = NA_COLS
    col = jnp.arange(GRID_W)
    col_start = jnp.clip(col - kc // 2, 0, GRID_W - kc)
    col_mask = (col[None, :] >= col_start[:, None]) & (col[None, :] < col_start[:, None] + kc)
    col_off = jnp.clip(col[None, :] - col[:, None], -(kc - 1), kc - 1) + (NA_COLS - 1)
    rows_sel = jnp.stack([rpb[:, NA_ROWS_MAX - 1 - cs:NA_ROWS_MAX - 1 - cs + kr, :]
                          for cs in range(kr)], axis=0)
    pick = (col_off[None, :, :] == jnp.arange(2 * kc - 1)[:, None, None]).astype(F32)
    tbl = jnp.einsum('chjr,rqk->chqjk', rows_sel, pick, precision=lax.Precision.HIGHEST)
    tbl = jnp.where(col_mask[None, None, :, None, :], tbl, NEG)
    return tbl.reshape(kr, rpb.shape[0] * GRID_W, kr * GRID_W).astype(F32)


def _outproj_body(x_ref, ma_ref, mb_ref, wo_ref, gm_ref, sf_ref, cf_ref, wr_ref, br_ref,
                  x1_ref, hs_ref, pg_ref, pgt_ref, tc_ref, to_ref):
    tm = SORT_TM
    tiles = range(SORT_TILES_PER_STEP)
    tt = tm * SORT_TILES_PER_STEP
    n_pos = TOP_K * tm
    mix = (jnp.dot(ma_ref[...], wo_ref[:A_Q_W, :], preferred_element_type=F32)
           + jnp.dot(mb_ref[...], wo_ref[A_Q_W:, :], preferred_element_type=F32))
    x1 = x_ref[...] + gm_ref[0] * mix
    x1_ref[...] = x1
    h = _rms(x1) * (1.0 + cf_ref[0]) + sf_ref[0]
    h_hi = h.astype(BF16)
    h_lo = (h - h_hi.astype(F32)).astype(BF16)
    w_r = wr_ref[...]
    w_hi = w_r.astype(BF16)
    w_lo = (w_r - w_hi.astype(F32)).astype(BF16)
    l_hi = jnp.dot(h_hi, jnp.concatenate([w_hi, w_lo], axis=1), preferred_element_type=F32)
    logits = (l_hi[:, :LANES] + l_hi[:, LANES:]
              + jnp.dot(h_lo, w_hi, preferred_element_type=F32) + br_ref[...])
    lane = lax.broadcasted_iota(I32, (tt, LANES), 1)
    work = logits
    vals, idxs = [], []
    for _ in range(TOP_K):
        m = jnp.max(work, axis=-1, keepdims=True)
        ik = jnp.min(jnp.where(work == m, lane, LANES), axis=-1, keepdims=True)
        vals.append(m)
        idxs.append(ik)
        work = jnp.where(lane == ik, -jnp.inf, work)
    es = [jnp.exp(v - vals[0]) for v in vals]
    den = es[0] + es[1] + es[2] + es[3]
    mh = jnp.zeros((tt, LANES), F32)
    for ik in idxs:
        mh = mh + (lane == ik).astype(F32)
    mhb = mh.astype(BF16)
    tri = (lax.broadcasted_iota(I32, (tm, tm), 0) > lax.broadcasted_iota(I32, (tm, tm), 1)).astype(BF16)
    lower = (lax.broadcasted_iota(I32, (LANES, LANES), 0)
             < lax.broadcasted_iota(I32, (LANES, LANES), 1)).astype(BF16)
    mh_t = [mhb[ti * tm:(ti + 1) * tm] for ti in tiles]
    earlier = [jnp.dot(tri, mh_t[ti], preferred_element_type=F32) for ti in tiles]
    below = [jnp.dot(mh_t[ti], lower, preferred_element_type=F32) for ti in tiles]
    toff = [jnp.sum(below[ti], axis=0, keepdims=True) for ti in tiles]
    cnt = [jnp.sum(mh[ti * tm:(ti + 1) * tm], axis=0, keepdims=True) for ti in tiles]
    posf = jnp.concatenate([earlier[ti] + toff[ti] for ti in tiles], axis=0)
    pg = jnp.zeros((tt, LANES), F32)
    for k in range(TOP_K):
        pos_k = jnp.sum(jnp.where(lane == idxs[k], posf, 0.0), axis=-1, keepdims=True)
        pg = jnp.where(lane == k, pos_k, pg)
        pg = jnp.where(lane == TOP_K + k, es[k] / den, pg)
    pg_ref[...] = pg
    pgt_all = pg.T[:ROW_TILE]
    pgt = [pgt_all[:, ti * tm:(ti + 1) * tm] for ti in tiles]
    for ti in tiles:
        pgt_ref[ti] = pgt[ti]
        tc_ref[ti] = cnt[ti]
        to_ref[ti] = toff[ti]
    pos_iota = lax.broadcasted_iota(I32, (n_pos, tm), 0)
    hits = []
    for ti in tiles:
        hit = pos_iota == pgt[ti][0:1, :].astype(I32)
        for k in range(1, TOP_K):
            hit = hit | (pos_iota == pgt[ti][k:k + 1, :].astype(I32))
        hits.append(jnp.where(hit, 1.0, 0.0).astype(BF16))
    srt = [jnp.dot(hits[ti], h_hi[ti * tm:(ti + 1) * tm], preferred_element_type=F32) for ti in tiles]
    for ti in tiles:
        for c in range(ROW_TILE):
            hs_ref[pl.ds(ti * n_pos * ROW_TILE + c, n_pos, stride=ROW_TILE), :] = (
                srt[ti][:, c * LANES:(c + 1) * LANES])


def _outproj(x2, mix_a, mix_b, w_out, gate_m, shift_f, scale_f, w_r, b_r, seq):
    t, d = x2.shape
    tm = SORT_TM
    nt = t // tm
    ts = SORT_TILES_PER_STEP
    tpb = seq // (ts * tm)
    n_pos = TOP_K * tm
    mod_spec = pl.BlockSpec((1, 1, d), lambda i: (i // tpb, 0, 0))
    return pl.pallas_call(
        _outproj_body,
        out_shape=(jax.ShapeDtypeStruct((t, d), F32),
                   jax.ShapeDtypeStruct((t * TOP_K * ROW_TILE, LANES), F32),
                   jax.ShapeDtypeStruct((t, LANES), F32),
                   jax.ShapeDtypeStruct((nt, ROW_TILE, tm), F32),
                   jax.ShapeDtypeStruct((nt, 1, LANES), F32),
                   jax.ShapeDtypeStruct((nt, 1, LANES), F32)),
        grid=(nt // ts,),
        in_specs=[pl.BlockSpec((ts * tm, d), lambda i: (i, 0)),
                  pl.BlockSpec((ts * tm, A_Q_W), lambda i: (i, 0)),
                  pl.BlockSpec((ts * tm, B_W), lambda i: (i, 0)),
                  pl.BlockSpec((A_Q_W + B_W, d), lambda i: (0, 0)),
                  mod_spec, mod_spec, mod_spec,
                  pl.BlockSpec((d, LANES), lambda i: (0, 0)),
                  pl.BlockSpec((1, LANES), lambda i: (0, 0))],
        out_specs=(pl.BlockSpec((ts * tm, d), lambda i: (i, 0)),
                   pl.BlockSpec((ts * n_pos * ROW_TILE, LANES), lambda i: (i, 0)),
                   pl.BlockSpec((ts * tm, LANES), lambda i: (i, 0)),
                   pl.BlockSpec((ts, ROW_TILE, tm), lambda i: (i, 0, 0)),
                   pl.BlockSpec((ts, 1, LANES), lambda i: (i, 0, 0)),
                   pl.BlockSpec((ts, 1, LANES), lambda i: (i, 0, 0))),
        compiler_params=_cparams(("arbitrary",)),
        name="outproj",
    )(x2, mix_a, mix_b, w_out, gate_m, shift_f, scale_f, w_r, b_r)


def _segment_copies(seg_ref, n_seg, first, last, src_hbm, dst_buf, sem):
    def body(i, carry):
        size = pl.multiple_of(seg_ref[0, 0, 2 * n_seg + i], ROW_TILE)

        @pl.when(size > 0)
        def _():
            src = pl.multiple_of(seg_ref[0, 0, i], ROW_TILE)
            dst = pl.multiple_of(seg_ref[0, 0, n_seg + i], ROW_TILE)
            pltpu.make_async_copy(src_hbm.at[pl.ds(src, size), :], dst_buf.at[pl.ds(dst, size), :], sem).start()
        return carry

    lax.fori_loop(first, last, body, 0)


def _experts_body(be_ref, vl_ref, seg0_ref, segn_ref, hs_hbm, wgu32_ref, bgu_ref, wd32_ref, bd_ref, o_ref,
                  xbuf, xs_ref, wgu_ref, wd_ref, gsem, *, n_blk, n_seg):
    bm = MOE_BM
    rt = ROW_TILE
    d_ff = wd_ref.shape[0]
    j = pl.program_id(0)
    slot = j % 2
    other = 1 - slot
    valid = vl_ref[j]

    def issue(seg_ref, s):
        _segment_copies(seg_ref, n_seg, seg_ref[0, 0, 3 * n_seg], seg_ref[0, 0, 3 * n_seg + 1],
                        hs_hbm, xbuf.at[s], gsem.at[s])

    @pl.when(j == 0)
    def _():
        xbuf[...] = jnp.zeros_like(xbuf)
        issue(seg0_ref, 0)

    @pl.when((j == 0) | (be_ref[j] != be_ref[jnp.maximum(j - 1, 0)]))
    def _():
        wgu_ref[...] = wgu32_ref[0].astype(BF16)
        wd_ref[...] = wd32_ref[0].astype(BF16)

    @pl.when(valid > 0)
    def _():
        rows = pl.multiple_of(valid * rt, rt)
        pltpu.make_async_copy(hs_hbm.at[pl.ds(0, rows), :], xbuf.at[slot, pl.ds(0, rows), :],
                              gsem.at[slot]).wait()

    @pl.when(j + 1 < n_blk)
    def _():
        issue(segn_ref, other)

    @pl.when(valid == 0)
    def _():
        o_ref[...] = jnp.zeros_like(o_ref)

    @pl.when(valid > 0)
    def _():
        xs_ref[...] = jnp.concatenate(
            [xbuf[slot, pl.ds(c, bm, stride=rt), :] for c in range(rt)], axis=1).astype(BF16)
        xs = xs_ref[...]
        acts = []
        n_chunk = 4
        cw = d_ff // n_chunk
        for c in range(n_chunk):
            gate = jnp.dot(xs, wgu_ref[:, c * cw:(c + 1) * cw], preferred_element_type=F32)
            gate = gate + bgu_ref[0, :, c * cw:(c + 1) * cw]
            up = jnp.dot(xs, wgu_ref[:, d_ff + c * cw:d_ff + (c + 1) * cw], preferred_element_type=F32)
            up = up + bgu_ref[0, :, d_ff + c * cw:d_ff + (c + 1) * cw]
            gate = jnp.minimum(gate, SWIGLU_LIMIT)
            up = jnp.clip(up, -SWIGLU_LIMIT, SWIGLU_LIMIT)
            glu = gate / (1.0 + jnp.exp(-SWIGLU_ALPHA * gate))
            acts.append(((up + 1.0) * glu).astype(BF16))
        act = jnp.concatenate(acts, axis=1)
        ys = jnp.dot(act, wd_ref[...], preferred_element_type=F32) + bd_ref[0]
        for c in range(rt):
            o_ref[pl.ds(c, bm, stride=rt), :] = ys[:, c * LANES:(c + 1) * LANES]


def _experts(blk_expert, blk_valid, seg, hs, wgu, bgu, wd, bd, n_seg):
    bm = MOE_BM
    rt = ROW_TILE
    n_blk, _, seg_w = seg.shape
    d = wd.shape[2]
    d_ff = wd.shape[1]
    assert d == rt * LANES and hs.shape[1] == LANES
    smem = functools.partial(pl.BlockSpec, (1, 1, seg_w), memory_space=pltpu.SMEM)
    last = n_blk - 1
    grid_spec = pltpu.PrefetchScalarGridSpec(
        num_scalar_prefetch=2, grid=(n_blk,),
        in_specs=[smem(lambda j, be, vl: (j, 0, 0)),
                  smem(lambda j, be, vl: (jnp.minimum(j + 1, last), 0, 0)),
                  pl.BlockSpec(memory_space=pl.ANY),
                  pl.BlockSpec((1, d, 2 * d_ff), lambda j, be, vl: (be[j], 0, 0)),
                  pl.BlockSpec((1, 1, 2 * d_ff), lambda j, be, vl: (be[j], 0, 0)),
                  pl.BlockSpec((1, d_ff, d), lambda j, be, vl: (be[j], 0, 0)),
                  pl.BlockSpec((1, 1, d), lambda j, be, vl: (be[j], 0, 0))],
        out_specs=pl.BlockSpec((bm * rt, LANES), lambda j, be, vl: (j, 0)),
        scratch_shapes=[pltpu.VMEM((2, bm * rt, LANES), F32), pltpu.VMEM((bm, d), BF16),
                        pltpu.VMEM((d, 2 * d_ff), BF16), pltpu.VMEM((d_ff, d), BF16),
                        pltpu.SemaphoreType.DMA((2,))])
    return pl.pallas_call(
        functools.partial(_experts_body, n_blk=n_blk, n_seg=n_seg),
        out_shape=jax.ShapeDtypeStruct((n_blk * bm * rt, LANES), F32),
        grid_spec=grid_spec,
        compiler_params=_cparams(("arbitrary",), VMEM_LIMIT_EXPERTS),
        name="experts",
    )(blk_expert, blk_valid, seg, seg, hs, wgu, bgu, wd, bd)


def _combine_body(seg0_ref, segn_ref, x1_ref, pg_ref, pgt_ref, gf_ref, gfin_ref, ys_hbm, o_ref,
                  ybuf, sem, *, nt):
    tm = x1_ref.shape[0]
    n_pos = TOP_K * tm
    rt = ROW_TILE
    i = pl.program_id(0)
    slot = i % 2
    other = 1 - slot

    def issue(seg_ref, s):
        _segment_copies(seg_ref, N_EXPERTS, 0, N_EXPERTS, ys_hbm, ybuf.at[s], sem.at[s])

    @pl.when(i == 0)
    def _():
        issue(seg0_ref, 0)

    pltpu.make_async_copy(ys_hbm.at[pl.ds(0, n_pos * rt), :], ybuf.at[slot], sem.at[slot]).wait()

    @pl.when(i + 1 < nt)
    def _():
        issue(segn_ref, other)

    pg = pg_ref[...]
    pgt = pgt_ref[0]
    pos_iota = lax.broadcasted_iota(I32, (n_pos, tm), 0)
    gs = jnp.zeros((n_pos, 1), F32)
    for k in range(TOP_K):
        own = pos_iota == pgt[k:k + 1, :].astype(I32)
        gs = gs + jnp.sum(jnp.where(own, pgt[TOP_K + k:TOP_K + k + 1, :], 0.0), axis=-1, keepdims=True)
    lane_pos = lax.broadcasted_iota(I32, (tm, n_pos), 1)
    hit = lane_pos == pg[:, 0:1].astype(I32)
    for k in range(1, TOP_K):
        hit = hit | (lane_pos == pg[:, k:k + 1].astype(I32))
    ysg = jnp.concatenate([ybuf[slot, pl.ds(c, n_pos, stride=rt), :] for c in range(rt)], axis=1) * gs
    y = jnp.dot(jnp.where(hit, 1.0, 0.0).astype(BF16), ysg.astype(BF16), preferred_element_type=F32)
    x2 = x1_ref[...] + gf_ref[0] * y
    o_ref[...] = _rms(x2) * gfin_ref[...]


def _combine(seg, x1, pg, pgt, gate_f, g_final, ys, seq):
    t, d = x1.shape
    tm = SORT_TM
    tpb = seq // tm
    nt = t // tm
    n_pos = TOP_K * tm
    seg_w = seg.shape[2]
    smem = functools.partial(pl.BlockSpec, (1, 1, seg_w), memory_space=pltpu.SMEM)
    return pl.pallas_call(
        functools.partial(_combine_body, nt=nt),
        out_shape=jax.ShapeDtypeStruct((t, d), F32),
        grid=(nt,),
        in_specs=[smem(lambda i: (i, 0, 0)),
                  smem(lambda i: (jnp.minimum(i + 1, nt - 1), 0, 0)),
                  pl.BlockSpec((tm, d), lambda i: (i, 0)),
                  pl.BlockSpec((tm, LANES), lambda i: (i, 0)),
                  pl.BlockSpec((1, ROW_TILE, tm), lambda i: (i, 0, 0)),
                  pl.BlockSpec((1, 1, d), lambda i: (i // tpb, 0, 0)),
                  pl.BlockSpec((1, d), lambda i: (0, 0)),
                  pl.BlockSpec(memory_space=pl.ANY)],
        out_specs=pl.BlockSpec((tm, d), lambda i: (i, 0)),
        scratch_shapes=[pltpu.VMEM((2, n_pos * ROW_TILE, LANES), F32), pltpu.SemaphoreType.DMA((2,))],
        compiler_params=_cparams(("arbitrary",)),
        name="combine",
    )(seg, seg, x1, pg, pgt, gate_f, g_final, ys)


def _rope_tables(seq):
    half = ROT_DIMS // 2
    pos = jnp.arange(seq, dtype=F32)
    inv_freq = ROPE_THETA ** (-jnp.arange(0, ROT_DIMS, 2, dtype=F32) / ROT_DIMS)
    ang = pos[:, None] * inv_freq[None, :]
    cos, sin = jnp.cos(ang), jnp.sin(ang)
    ones = jnp.ones((seq, HEAD_DIM - ROT_DIMS), F32)
    zeros = jnp.zeros((seq, HEAD_DIM - ROT_DIMS), F32)
    zh = jnp.zeros((seq, half), F32)
    rep = LANES // HEAD_DIM
    cos_t = jnp.tile(jnp.concatenate([cos, cos, ones], axis=1), (1, rep))
    sa_t = jnp.tile(jnp.concatenate([-sin, zh, zeros], axis=1), (1, rep))
    sb_t = jnp.tile(jnp.concatenate([zh, sin, zeros], axis=1), (1, rep))
    return cos_t, sa_t, sb_t


def _widen_in_proj(w_in):
    d = w_in.shape[0]
    scale = HEAD_DIM ** -0.5
    o = 0
    wqa = w_in[:, o:o + A_Q_W] * scale; o += A_Q_W
    wka = w_in[:, o:o + A_KV_W]; o += A_KV_W
    wva = w_in[:, o:o + A_KV_W]; o += A_KV_W
    wqb = w_in[:, o:o + B_W] * scale; o += B_W
    wkb = w_in[:, o:o + B_W]; o += B_W
    wvb = w_in[:, o:o + B_W]

    def widen(w):
        w = w.reshape(d, A_KV_HEADS, 1, HEAD_DIM)
        return jnp.broadcast_to(w, (d, A_KV_HEADS, A_GROUP, HEAD_DIM)).reshape(d, A_Q_W)

    return jnp.concatenate([wqa, widen(wka), widen(wva), wqb, wkb, wvb], axis=1).astype(BF16)


def _round_up(n, m):
    return (n + m - 1) // m * m


def _moe_tables(tile_cnt, tile_off, n_blk):
    bm = MOE_BM
    nt = tile_cnt.shape[0]
    n_pos = TOP_K * SORT_TM
    cum = jnp.cumsum(tile_cnt, axis=0) - tile_cnt
    counts = jnp.sum(tile_cnt, axis=0)
    padded = ((counts + bm - 1) // bm) * bm
    pends = jnp.cumsum(padded)
    pstarts = pends - padded
    blk_first = jnp.arange(n_blk, dtype=I32) * bm
    blk_expert = jnp.minimum(
        jnp.sum((pends[None, :] <= blk_first[:, None]).astype(I32), axis=1), N_EXPERTS - 1)
    r0 = blk_first - pstarts[blk_expert]
    blk_valid = jnp.clip(counts[blk_expert] - r0, 0, bm)
    cum_b = cum.T[blk_expert]
    cnt_b = tile_cnt.T[blk_expert]
    off_b = tile_off.T[blk_expert]
    lo = jnp.maximum(r0[:, None], cum_b)
    hi = jnp.minimum(r0[:, None] + bm, cum_b + cnt_b)
    size = jnp.maximum(hi - lo, 0)
    tile_ids = jnp.arange(nt, dtype=I32)[None, :]
    src = tile_ids * n_pos + off_b + (lo - cum_b)
    dst = jnp.where(size > 0, lo - r0[:, None], 0)
    first = jnp.min(jnp.where(size > 0, tile_ids, nt), axis=1)
    last = jnp.max(jnp.where(size > 0, tile_ids + 1, 0), axis=1)
    first = jnp.minimum(first, last)
    seg_w = _round_up(3 * nt + 2, LANES)
    exp_seg = jnp.concatenate(
        [src * ROW_TILE, dst * ROW_TILE, size * ROW_TILE, first[:, None], last[:, None],
         jnp.zeros((n_blk, seg_w - 3 * nt - 2), I32)], axis=1).reshape(n_blk, 1, seg_w)
    comb_seg = jnp.concatenate(
        [(pstarts[None, :] + cum) * ROW_TILE, tile_off * ROW_TILE, tile_cnt * ROW_TILE,
         jnp.zeros((nt, LANES - 3 * N_EXPERTS), I32)], axis=1).reshape(nt, 1, LANES)
    return blk_expert, blk_valid, exp_seg, comb_seg


def kernel(x, c, w_ada, b_ada, w_in, sink, rpb, g_out_a, g_out_b, w_out, w_router, b_router,
           w_gate_up, b_gate_up, w_down, b_down, g_final):
    bsz, seq, d = x.shape
    t = bsz * seq
    depth = w_ada.shape[0]
    assert depth == 1, "the final norm is fused into the layer combine"
    bm = MOE_BM
    n_blk = (t * TOP_K) // bm + N_EXPERTS
    nt = t // SORT_TM

    cos_t, sa_t, sb_t = _rope_tables(seq)
    x2 = x.reshape(t, d)
    for l in range(depth):
        mod = _ada(c, w_ada[l], b_ada[l][None, :])
        shift_m, scale_m, gate_m, shift_f, scale_f, gate_f = [
            m.reshape(bsz, 1, d) for m in jnp.split(mod, 6, axis=-1)]

        proj = _inproj(x2, shift_m, scale_m, _widen_in_proj(w_in[l]), cos_t, sa_t, sb_t, seq)
        mix_a = _swa(proj, sink[l], g_out_a[l][None, :], bsz, seq)
        mix_b = _natten(proj, _na_bias_table(rpb[l], seq // GRID_W), g_out_b[l][None, :], bsz, seq)

        w_r = jnp.pad(w_router[l], ((0, 0), (0, LANES - N_EXPERTS)))
        b_r = jnp.pad(b_router[l], (0, LANES - N_EXPERTS), constant_values=NEG)[None, :]
        x1, hs, pg, pgt, tile_cnt, tile_off = _outproj(
            x2, mix_a, mix_b, w_out[l].astype(BF16), gate_m, shift_f, scale_f, w_r, b_r, seq)

        blk_expert, blk_valid, exp_seg, comb_seg = _moe_tables(
            tile_cnt[:, 0, :N_EXPERTS].astype(I32), tile_off[:, 0, :N_EXPERTS].astype(I32), n_blk)
        ys = _experts(blk_expert, blk_valid, exp_seg, hs,
                      w_gate_up[l], b_gate_up[l][:, None, :],
                      w_down[l], b_down[l][:, None, :], nt)
        x2 = _combine(comb_seg, x1, pg, pgt, gate_f, g_final[None, :], ys, seq)
    return x2.reshape(bsz, seq, d)
```

```python
import functools

import jax
import jax.numpy as jnp
from jax import lax
from jax.experimental import pallas as pl
from jax.experimental.pallas import tpu as pltpu

F32 = jnp.float32
BF16 = jnp.bfloat16
I32 = jnp.int32

HEAD_DIM = 64
A_Q_HEADS = 8
A_KV_HEADS = 2
A_GROUP = A_Q_HEADS // A_KV_HEADS
B_HEADS = 8
A_Q_W = A_Q_HEADS * HEAD_DIM
A_KV_W = A_KV_HEADS * HEAD_DIM
B_W = B_HEADS * HEAD_DIM
WINDOW = 128
ROT_DIMS = HEAD_DIM // 4
ROPE_THETA = 500000.0
GRID_W = 64
NA_ROWS_MAX = 8
NA_COLS = 16
N_EXPERTS = 32
TOP_K = 4
SWIGLU_LIMIT = 7.0
SWIGLU_ALPHA = 1.702
EPS = 1e-5
NEG = -1e30

LANES = 128
ROW_TILE = 8
PROJ_BLK = 512
TM_PROJ = 512
SORT_TM = 256
SORT_TILES_PER_STEP = 2
MOE_BM = 1024
MOE_SUB = 256
SWA_BLOCKS_PER_STEP = 2
NA_ROWS_PER_STEP = 8
NA_ROW_UNROLL = 4
VMEM_LIMIT = 48 << 20
VMEM_LIMIT_EXPERTS = 58 << 20


def _cparams(sem, limit=VMEM_LIMIT):
    return pltpu.CompilerParams(dimension_semantics=sem, vmem_limit_bytes=limit)


def _rms(x):
    return x * lax.rsqrt(jnp.mean(x * x, axis=-1, keepdims=True) + EPS)


def _ada_body(c_ref, w_ref, b_ref, o_ref):
    c = c_ref[...]
    ca = c / (1.0 + jnp.exp(-c))
    o_ref[...] = jnp.dot(ca, w_ref[...], preferred_element_type=F32,
                         precision=lax.Precision.HIGHEST) + b_ref[...]


def _ada(c, w, b):
    bsz, d = c.shape
    n = w.shape[1]
    bn = n // 4
    return pl.pallas_call(
        _ada_body,
        out_shape=jax.ShapeDtypeStruct((bsz, n), F32),
        grid=(4,),
        in_specs=[pl.BlockSpec((bsz, d), lambda i: (0, 0)),
                  pl.BlockSpec((d, bn), lambda i: (0, i)),
                  pl.BlockSpec((1, bn), lambda i: (0, i))],
        out_specs=pl.BlockSpec((bsz, bn), lambda i: (0, i)),
        compiler_params=_cparams(("arbitrary",)),
        name="ada",
    )(c, w, b)


def _inproj_body(x_ref, sh_ref, sc_ref, w_ref, cos_ref, sa_ref, sb_ref, o_ref):
    h = (_rms(x_ref[...]) * (1.0 + sc_ref[0]) + sh_ref[0]).astype(BF16)
    cos = cos_ref[...]
    sa = sa_ref[...]
    sb = sb_ref[...]
    n_blk = w_ref.shape[1] // PROJ_BLK
    for cb in range(n_blk):
        p = jnp.dot(h, w_ref[:, cb * PROJ_BLK:(cb + 1) * PROJ_BLK], preferred_element_type=F32)
        if cb < 2:
            parts = []
            for j in range(PROJ_BLK // LANES):
                pj = p[:, j * LANES:(j + 1) * LANES]
                parts.append(pj * cos + pltpu.roll(pj, LANES - ROT_DIMS // 2, axis=1) * sa
                             + pltpu.roll(pj, ROT_DIMS // 2, axis=1) * sb)
            p = jnp.concatenate(parts, axis=1)
        o_ref[:, cb * PROJ_BLK:(cb + 1) * PROJ_BLK] = p.astype(BF16)


def _inproj(x2, shift, scale, w_ext, cos_t, sa_t, sb_t, seq):
    t, d = x2.shape
    n = w_ext.shape[1]
    tm = TM_PROJ
    tpb = seq // tm
    return pl.pallas_call(
        _inproj_body,
        out_shape=jax.ShapeDtypeStruct((t, n), BF16),
        grid=(t // tm,),
        in_specs=[pl.BlockSpec((tm, d), lambda i: (i, 0)),
                  pl.BlockSpec((1, 1, d), lambda i: (i // tpb, 0, 0)),
                  pl.BlockSpec((1, 1, d), lambda i: (i // tpb, 0, 0)),
                  pl.BlockSpec((d, n), lambda i: (0, 0)),
                  pl.BlockSpec((tm, LANES), lambda i: (i % tpb, 0)),
                  pl.BlockSpec((tm, LANES), lambda i: (i % tpb, 0)),
                  pl.BlockSpec((tm, LANES), lambda i: (i % tpb, 0))],
        out_specs=pl.BlockSpec((tm, n), lambda i: (i, 0)),
        compiler_params=_cparams(("arbitrary",)),
        name="inproj",
    )(x2, shift, scale, w_ext, cos_t, sa_t, sb_t)


def _swa_body(sink_ref, q_ref, k_ref, v_ref, g_ref, o_ref, *, seq):
    step = pl.program_id(1)
    blk = WINDOW
    kw_len = 3 * blk
    gw = A_GROUP * HEAD_DIM
    srows = A_GROUP * blk
    row = lax.broadcasted_iota(I32, (srows, 1), 0)
    grp = lax.broadcasted_iota(I32, (1, gw), 1) // HEAD_DIM
    blocks = range(SWA_BLOCKS_PER_STEP)
    units = [(bi, h) for bi in blocks for h in range(A_KV_HEADS)]
    kws, vws, alloweds = [], [], []
    for bi in blocks:
        n = step * SWA_BLOCKS_PER_STEP + bi
        start = pl.multiple_of(jnp.clip(n * blk - blk, 0, seq - kw_len), blk)
        kws.append(k_ref[pl.ds(start, kw_len), :])
        vws.append(v_ref[pl.ds(start, kw_len), :])
        qpos = n * blk + row % blk
        kpos = start + lax.broadcasted_iota(I32, (1, kw_len), 1)
        alloweds.append(jnp.abs(qpos - kpos) <= WINDOW)
    sks = []
    for h in range(A_KV_HEADS):
        sk = jnp.full((srows, 1), sink_ref[h * A_GROUP], F32)
        for g in range(1, A_GROUP):
            sk = jnp.where(row // blk == g, sink_ref[h * A_GROUP + g], sk)
        sks.append(sk)
    qms = []
    for bi, h in units:
        qh = q_ref[bi * blk:(bi + 1) * blk, h * gw:(h + 1) * gw]
        qms.append(jnp.concatenate(
            [jnp.where(grp == g, qh, jnp.zeros_like(qh)) for g in range(A_GROUP)], axis=0))
    ss = [lax.dot_general(qms[u], kws[bi][:, h * gw:(h + 1) * gw], (((1,), (1,)), ((), ())),
                          preferred_element_type=F32) for u, (bi, h) in enumerate(units)]
    ss = [jnp.where(alloweds[bi], ss[u], NEG) for u, (bi, h) in enumerate(units)]
    ms = [jnp.maximum(jnp.max(ss[u], axis=-1, keepdims=True), sks[h]) for u, (bi, h) in enumerate(units)]
    ps = [jnp.exp(ss[u] - ms[u]) for u in range(len(units))]
    invs = [1.0 / (jnp.sum(ps[u], axis=-1, keepdims=True) + jnp.exp(sks[h] - ms[u]))
            for u, (bi, h) in enumerate(units)]
    pns = [(ps[u] * invs[u]).astype(BF16) for u in range(len(units))]
    ress = [jnp.dot(pns[u], vws[bi][:, h * gw:(h + 1) * gw], preferred_element_type=F32)
            for u, (bi, h) in enumerate(units)]
    accs = []
    for res in ress:
        acc = res[0:blk]
        for g in range(1, A_GROUP):
            acc = jnp.where(grp == g, res[g * blk:(g + 1) * blk], acc)
        accs.append(acc)
    for bi in blocks:
        oa = jnp.concatenate(accs[bi * A_KV_HEADS:(bi + 1) * A_KV_HEADS], axis=1)
        o_ref[bi * blk:(bi + 1) * blk, :] = (_rms(oa) * g_ref[...]).astype(BF16)


def _swa(proj, sink, g_a, bsz, seq):
    t = proj.shape[0]
    blk = WINDOW * SWA_BLOCKS_PER_STEP
    nq = seq // blk
    grid_spec = pltpu.PrefetchScalarGridSpec(
        num_scalar_prefetch=1, grid=(bsz, nq),
        in_specs=[pl.BlockSpec((blk, PROJ_BLK), lambda b, n, s: (b * nq + n, 0)),
                  pl.BlockSpec((seq, PROJ_BLK), lambda b, n, s: (b, 1)),
                  pl.BlockSpec((seq, PROJ_BLK), lambda b, n, s: (b, 2)),
                  pl.BlockSpec((1, A_Q_W), lambda b, n, s: (0, 0))],
        out_specs=pl.BlockSpec((blk, A_Q_W), lambda b, n, s: (b * nq + n, 0)))
    return pl.pallas_call(
        functools.partial(_swa_body, seq=seq),
        out_shape=jax.ShapeDtypeStruct((t, A_Q_W), BF16),
        grid_spec=grid_spec,
        compiler_params=_cparams(("arbitrary", "arbitrary")),
        name="swa",
    )(sink, proj, proj, proj, g_a)


def _na_body(q_ref, k_ref, v_ref, bias_ref, g_ref, o_ref, *, rows, kr, rb):
    blk = pl.program_id(1)
    per = LANES // HEAD_DIM
    n_pair = B_W // LANES
    half = lax.broadcasted_iota(I32, (1, LANES), 1) // HEAD_DIM
    gain = g_ref[...]

    def one_row(i, carry):
        r = blk * rb + i
        rs = jnp.clip(r - kr // 2, 0, rows - kr)
        start = pl.multiple_of(rs * GRID_W, GRID_W)
        kw = k_ref[pl.ds(start, kr * GRID_W), :]
        vw = v_ref[pl.ds(start, kr * GRID_W), :]
        q = q_ref[pl.ds(pl.multiple_of(i * GRID_W, GRID_W), GRID_W), :]
        s_parts = []
        for pr in range(n_pair):
            qp = q[:, pr * LANES:(pr + 1) * LANES]
            qm = jnp.concatenate([jnp.where(half == j, qp, jnp.zeros_like(qp)) for j in range(per)], axis=0)
            s_parts.append(lax.dot_general(qm, kw[:, pr * LANES:(pr + 1) * LANES],
                                           (((1,), (1,)), ((), ())), preferred_element_type=F32))
        s = jnp.concatenate(s_parts, axis=0) + bias_ref[r - rs]
        m = jnp.max(s, axis=-1, keepdims=True)
        p = jnp.exp(s - m)
        pn = (p * (1.0 / jnp.sum(p, axis=-1, keepdims=True))).astype(BF16)
        o_parts = []
        for pr in range(n_pair):
            res = jnp.dot(pn[pr * per * GRID_W:(pr + 1) * per * GRID_W], vw[:, pr * LANES:(pr + 1) * LANES],
                          preferred_element_type=F32)
            op = res[0:GRID_W]
            for j in range(1, per):
                op = jnp.where(half == j, res[j * GRID_W:(j + 1) * GRID_W], op)
            o_parts.append(op)
        ob = jnp.concatenate(o_parts, axis=1)
        o_ref[pl.ds(pl.multiple_of(i * GRID_W, GRID_W), GRID_W), :] = (_rms(ob) * gain).astype(BF16)
        return carry

    def row_group(ig, carry):
        for u in range(NA_ROW_UNROLL):
            one_row(NA_ROW_UNROLL * ig + u, carry)
        return carry

    lax.fori_loop(0, rb // NA_ROW_UNROLL, row_group, 0)


def _natten(proj, bias_t, g_b, bsz, seq):
    t = proj.shape[0]
    rows = seq // GRID_W
    kr = min(NA_ROWS_MAX, rows)
    rb = NA_ROWS_PER_STEP
    nb = rows // rb
    return pl.pallas_call(
        functools.partial(_na_body, rows=rows, kr=kr, rb=rb),
        out_shape=jax.ShapeDtypeStruct((t, B_W), BF16),
        grid=(bsz, nb),
        in_specs=[pl.BlockSpec((rb * GRID_W, PROJ_BLK), lambda b, r: (b * nb + r, 3)),
                  pl.BlockSpec((seq, PROJ_BLK), lambda b, r: (b, 4)),
                  pl.BlockSpec((seq, PROJ_BLK), lambda b, r: (b, 5)),
                  pl.BlockSpec((kr, B_HEADS * GRID_W, kr * GRID_W), lambda b, r: (0, 0, 0)),
                  pl.BlockSpec((1, B_W), lambda b, r: (0, 0))],
        out_specs=pl.BlockSpec((rb * GRID_W, B_W), lambda b, r: (b * nb + r, 0)),
        compiler_params=_cparams(("arbitrary", "arbitrary")),
        name="natten",
    )(proj, proj, proj, bias_t, g_b)


def _na_bias_table(rpb, rows):
    kr = min(NA_ROWS_MAX, rows)
    kc = NA_COLS
    col = jnp.arange(GRID_W)
    col_start = jnp.clip(col - kc // 2, 0, GRID_W - kc)
    col_mask = (col[None, :] >= col_start[:, None]) & (col[None, :] < col_start[:, None] + kc)
    col_off = jnp.clip(col[None, :] - col[:, None], -(kc - 1), kc - 1) + (NA_COLS - 1)
    rows_sel = jnp.stack([rpb[:, NA_ROWS_MAX - 1 - cs:NA_ROWS_MAX - 1 - cs + kr, :]
                          for cs in range(kr)], axis=0)
    pick = (col_off[None, :, :] == jnp.arange(2 * kc - 1)[:, None, None]).astype(F32)
    tbl = jnp.einsum('chjr,rqk->chqjk', rows_sel, pick, precision=lax.Precision.HIGHEST)
    tbl = jnp.where(col_mask[None, None, :, None, :], tbl, NEG)
    return tbl.reshape(kr, rpb.shape[0] * GRID_W, kr * GRID_W).astype(F32)


def _outproj_body(x_ref, ma_ref, mb_ref, wo_ref, gm_ref, sf_ref, cf_ref, wr_ref, br_ref,
                  x1_ref, hs_ref, pg_ref, tc_ref, to_ref):
    tm = SORT_TM
    tiles = range(SORT_TILES_PER_STEP)
    tt = tm * SORT_TILES_PER_STEP
    n_pos = TOP_K * tm
    mix = (jnp.dot(ma_ref[...], wo_ref[:A_Q_W, :], preferred_element_type=F32)
           + jnp.dot(mb_ref[...], wo_ref[A_Q_W:, :], preferred_element_type=F32))
    x1 = x_ref[...] + gm_ref[0] * mix
    x1_ref[...] = x1
    h = _rms(x1) * (1.0 + cf_ref[0]) + sf_ref[0]
    h_hi = h.astype(BF16)
    h_lo = (h - h_hi.astype(F32)).astype(BF16)
    w_r = wr_ref[...]
    w_hi = w_r.astype(BF16)
    w_lo = (w_r - w_hi.astype(F32)).astype(BF16)
    l_hi = jnp.dot(h_hi, jnp.concatenate([w_hi, w_lo], axis=1), preferred_element_type=F32)
    logits = (l_hi[:, :LANES] + l_hi[:, LANES:]
              + jnp.dot(h_lo, w_hi, preferred_element_type=F32) + br_ref[...])
    lane = lax.broadcasted_iota(I32, (tt, LANES), 1)
    work = logits
    vals, idxs = [], []
    for _ in range(TOP_K):
        m = jnp.max(work, axis=-1, keepdims=True)
        ik = jnp.min(jnp.where(work == m, lane, LANES), axis=-1, keepdims=True)
        vals.append(m)
        idxs.append(ik)
        work = jnp.where(lane == ik, -jnp.inf, work)
    es = [jnp.exp(v - vals[0]) for v in vals]
    den = es[0] + es[1] + es[2] + es[3]
    mh = jnp.zeros((tt, LANES), F32)
    for ik in idxs:
        mh = mh + (lane == ik).astype(F32)
    mhb = mh.astype(BF16)
    tri = (lax.broadcasted_iota(I32, (tm, tm), 0) > lax.broadcasted_iota(I32, (tm, tm), 1)).astype(BF16)
    lower = (lax.broadcasted_iota(I32, (LANES, LANES), 0)
             < lax.broadcasted_iota(I32, (LANES, LANES), 1)).astype(BF16)
    mh_t = [mhb[ti * tm:(ti + 1) * tm] for ti in tiles]
    earlier = [jnp.dot(tri, mh_t[ti], preferred_element_type=F32) for ti in tiles]
    below = [jnp.dot(mh_t[ti], lower, preferred_element_type=F32) for ti in tiles]
    toff = [jnp.sum(below[ti], axis=0, keepdims=True) for ti in tiles]
    cnt = [jnp.sum(mh[ti * tm:(ti + 1) * tm], axis=0, keepdims=True) for ti in tiles]
    posf = jnp.concatenate([earlier[ti] + toff[ti] for ti in tiles], axis=0)
    pg = jnp.zeros((tt, LANES), F32)
    for k in range(TOP_K):
        pos_k = jnp.sum(jnp.where(lane == idxs[k], posf, 0.0), axis=-1, keepdims=True)
        pg = jnp.where(lane == k, pos_k, pg)
        pg = jnp.where(lane == TOP_K + k, es[k] / den, pg)
    pg_ref[...] = pg
    pgt_all = pg.T[:ROW_TILE]
    pgt = [pgt_all[:, ti * tm:(ti + 1) * tm] for ti in tiles]
    for ti in tiles:
        tc_ref[ti] = cnt[ti]
        to_ref[ti] = toff[ti]
    pos_iota = lax.broadcasted_iota(I32, (n_pos, tm), 0)
    hits = []
    for ti in tiles:
        hit = pos_iota == pgt[ti][0:1, :].astype(I32)
        for k in range(1, TOP_K):
            hit = hit | (pos_iota == pgt[ti][k:k + 1, :].astype(I32))
        hits.append(jnp.where(hit, 1.0, 0.0).astype(BF16))
    srt = [jnp.dot(hits[ti], h_hi[ti * tm:(ti + 1) * tm], preferred_element_type=F32) for ti in tiles]
    for ti in tiles:
        for c in range(ROW_TILE):
            hs_ref[pl.ds(ti * n_pos * ROW_TILE + c, n_pos, stride=ROW_TILE), :] = (
                srt[ti][:, c * LANES:(c + 1) * LANES])


def _outproj(x2, mix_a, mix_b, w_out, gate_m, shift_f, scale_f, w_r, b_r, seq):
    t, d = x2.shape
    tm = SORT_TM
    nt = t // tm
    ts = SORT_TILES_PER_STEP
    tpb = seq // (ts * tm)
    n_pos = TOP_K * tm
    mod_spec = pl.BlockSpec((1, 1, d), lambda i: (i // tpb, 0, 0))
    return pl.pallas_call(
        _outproj_body,
        out_shape=(jax.ShapeDtypeStruct((t, d), F32),
                   jax.ShapeDtypeStruct((t * TOP_K * ROW_TILE, LANES), F32),
                   jax.ShapeDtypeStruct((t, LANES), F32),
                   jax.ShapeDtypeStruct((nt, 1, LANES), F32),
                   jax.ShapeDtypeStruct((nt, 1, LANES), F32)),
        grid=(nt // ts,),
        in_specs=[pl.BlockSpec((ts * tm, d), lambda i: (i, 0)),
                  pl.BlockSpec((ts * tm, A_Q_W), lambda i: (i, 0)),
                  pl.BlockSpec((ts * tm, B_W), lambda i: (i, 0)),
                  pl.BlockSpec((A_Q_W + B_W, d), lambda i: (0, 0)),
                  mod_spec, mod_spec, mod_spec,
                  pl.BlockSpec((d, LANES), lambda i: (0, 0)),
                  pl.BlockSpec((1, LANES), lambda i: (0, 0))],
        out_specs=(pl.BlockSpec((ts * tm, d), lambda i: (i, 0)),
                   pl.BlockSpec((ts * n_pos * ROW_TILE, LANES), lambda i: (i, 0)),
                   pl.BlockSpec((ts * tm, LANES), lambda i: (i, 0)),
                   pl.BlockSpec((ts, 1, LANES), lambda i: (i, 0, 0)),
                   pl.BlockSpec((ts, 1, LANES), lambda i: (i, 0, 0))),
        compiler_params=_cparams(("arbitrary",)),
        name="outproj",
    )(x2, mix_a, mix_b, w_out, gate_m, shift_f, scale_f, w_r, b_r)


def _segment_copies(seg_ref, n_seg, first, last, src_hbm, dst_buf, sem, tbl=0):
    def body(i, carry):
        size = pl.multiple_of(seg_ref[tbl, 0, 2 * n_seg + i], ROW_TILE)

        @pl.when(size > 0)
        def _():
            src = pl.multiple_of(seg_ref[tbl, 0, i], ROW_TILE)
            dst = pl.multiple_of(seg_ref[tbl, 0, n_seg + i], ROW_TILE)
            pltpu.make_async_copy(src_hbm.at[pl.ds(src, size), :], dst_buf.at[pl.ds(dst, size), :], sem).start()
        return carry

    lax.fori_loop(first, last, body, 0)


def _experts_body(be_ref, vl_ref, seg0_ref, segn_ref, hs_hbm, wgu32_ref, bgu_ref, wd32_ref, bd_ref, o_ref,
                  xbuf, xs_ref, wgu_ref, wd_ref, gsem, *, n_blk, n_seg):
    bm = MOE_BM
    rt = ROW_TILE
    d_ff = wd_ref.shape[0]
    j = pl.program_id(0)
    slot = j % 2
    other = 1 - slot
    valid = vl_ref[j]

    def issue(seg_ref, s):
        _segment_copies(seg_ref, n_seg, seg_ref[0, 0, 3 * n_seg], seg_ref[0, 0, 3 * n_seg + 1],
                        hs_hbm, xbuf.at[s], gsem.at[s])

    @pl.when(j == 0)
    def _():
        xbuf[...] = jnp.zeros_like(xbuf)
        issue(seg0_ref, 0)

    @pl.when((j == 0) | (be_ref[j] != be_ref[jnp.maximum(j - 1, 0)]))
    def _():
        wgu_ref[...] = wgu32_ref[0].astype(BF16)
        wd_ref[...] = wd32_ref[0].astype(BF16)

    @pl.when(valid > 0)
    def _():
        rows = pl.multiple_of(valid * rt, rt)
        pltpu.make_async_copy(hs_hbm.at[pl.ds(0, rows), :], xbuf.at[slot, pl.ds(0, rows), :],
                              gsem.at[slot]).wait()

    @pl.when(j + 1 < n_blk)
    def _():
        issue(segn_ref, other)

    sub = MOE_SUB
    n_sub = (valid + sub - 1) // sub

    def zero_sub(sb, carry):
        r0 = pl.multiple_of(sb * sub * rt, sub * rt)
        o_ref[pl.ds(r0, sub * rt), :] = jnp.zeros((sub * rt, LANES), F32)
        return carry

    lax.fori_loop(n_sub, bm // sub, zero_sub, 0)

    def compute_sub(sb, carry):
        r0 = pl.multiple_of(sb * sub * rt, sub * rt)
        xs_ref[...] = jnp.concatenate(
            [xbuf[slot, pl.ds(r0 + c, sub, stride=rt), :] for c in range(rt)], axis=1).astype(BF16)
        xs = xs_ref[...]
        acts = []
        n_chunk = 4
        cw = d_ff // n_chunk
        for c in range(n_chunk):
            gate = jnp.dot(xs, wgu_ref[:, c * cw:(c + 1) * cw], preferred_element_type=F32)
            gate = gate + bgu_ref[0, :, c * cw:(c + 1) * cw]
            up = jnp.dot(xs, wgu_ref[:, d_ff + c * cw:d_ff + (c + 1) * cw], preferred_element_type=F32)
            up = up + bgu_ref[0, :, d_ff + c * cw:d_ff + (c + 1) * cw]
            gate = jnp.minimum(gate, SWIGLU_LIMIT)
            up = jnp.clip(up, -SWIGLU_LIMIT, SWIGLU_LIMIT)
            glu = gate / (1.0 + jnp.exp(-SWIGLU_ALPHA * gate))
            acts.append(((up + 1.0) * glu).astype(BF16))
        act = jnp.concatenate(acts, axis=1)
        ys = jnp.dot(act, wd_ref[...], preferred_element_type=F32) + bd_ref[0]
        for c in range(rt):
            o_ref[pl.ds(r0 + c, sub, stride=rt), :] = ys[:, c * LANES:(c + 1) * LANES]
        return carry

    lax.fori_loop(0, n_sub, compute_sub, 0)


def _experts(blk_expert, blk_valid, seg, hs, wgu, bgu, wd, bd, n_seg):
    bm = MOE_BM
    rt = ROW_TILE
    n_blk, _, seg_w = seg.shape
    d = wd.shape[2]
    d_ff = wd.shape[1]
    assert d == rt * LANES and hs.shape[1] == LANES
    smem = functools.partial(pl.BlockSpec, (1, 1, seg_w), memory_space=pltpu.SMEM)
    last = n_blk - 1
    grid_spec = pltpu.PrefetchScalarGridSpec(
        num_scalar_prefetch=2, grid=(n_blk,),
        in_specs=[smem(lambda j, be, vl: (j, 0, 0)),
                  smem(lambda j, be, vl: (jnp.minimum(j + 1, last), 0, 0)),
                  pl.BlockSpec(memory_space=pl.ANY),
                  pl.BlockSpec((1, d, 2 * d_ff), lambda j, be, vl: (be[j], 0, 0)),
                  pl.BlockSpec((1, 1, 2 * d_ff), lambda j, be, vl: (be[j], 0, 0)),
                  pl.BlockSpec((1, d_ff, d), lambda j, be, vl: (be[j], 0, 0)),
                  pl.BlockSpec((1, 1, d), lambda j, be, vl: (be[j], 0, 0))],
        out_specs=pl.BlockSpec((bm * rt, LANES), lambda j, be, vl: (j, 0)),
        scratch_shapes=[pltpu.VMEM((2, bm * rt, LANES), F32), pltpu.VMEM((MOE_SUB, d), BF16),
                        pltpu.VMEM((d, 2 * d_ff), BF16), pltpu.VMEM((d_ff, d), BF16),
                        pltpu.SemaphoreType.DMA((2,))])
    return pl.pallas_call(
        functools.partial(_experts_body, n_blk=n_blk, n_seg=n_seg),
        out_shape=jax.ShapeDtypeStruct((n_blk * bm * rt, LANES), F32),
        grid_spec=grid_spec,
        compiler_params=_cparams(("arbitrary",), VMEM_LIMIT_EXPERTS),
        name="experts",
    )(blk_expert, blk_valid, seg, seg, hs, wgu, bgu, wd, bd)


def _combine_body(seg0_ref, segn_ref, x1_ref, pg_ref, gf_ref, gfin_ref, ys_hbm, o_ref,
                  ybuf, sem, *, n_steps):
    tm = SORT_TM
    tiles = range(SORT_TILES_PER_STEP)
    n_pos = TOP_K * tm
    rt = ROW_TILE
    i = pl.program_id(0)
    slot = i % 2
    other = 1 - slot

    def issue(seg_ref, s):
        for ti in tiles:
            _segment_copies(seg_ref, N_EXPERTS, 0, N_EXPERTS, ys_hbm, ybuf.at[s, ti], sem.at[s], tbl=ti)

    @pl.when(i == 0)
    def _():
        issue(seg0_ref, 0)

    for ti in tiles:
        pltpu.make_async_copy(ys_hbm.at[pl.ds(0, n_pos * rt), :], ybuf.at[slot, ti], sem.at[slot]).wait()

    @pl.when(i + 1 < n_steps)
    def _():
        issue(segn_ref, other)

    lane_pos = lax.broadcasted_iota(I32, (tm, n_pos), 1)
    ws = []
    for ti in tiles:
        pg = pg_ref[ti * tm:(ti + 1) * tm, :]
        w = jnp.zeros((tm, n_pos), F32)
        for k in range(TOP_K):
            w = jnp.where(lane_pos == pg[:, k:k + 1].astype(I32), pg[:, TOP_K + k:TOP_K + k + 1], w)
        ws.append(w)
    w_his = [w.astype(BF16) for w in ws]
    w_los = [(ws[ti] - w_his[ti].astype(F32)).astype(BF16) for ti in tiles]
    ysbs = [jnp.concatenate([ybuf[slot, ti, pl.ds(c, n_pos, stride=rt), :] for c in range(rt)],
                            axis=1).astype(BF16) for ti in tiles]
    ys = [jnp.dot(w_his[ti], ysbs[ti], preferred_element_type=F32)
          + jnp.dot(w_los[ti], ysbs[ti], preferred_element_type=F32) for ti in tiles]
    for ti in tiles:
        x2 = x1_ref[ti * tm:(ti + 1) * tm, :] + gf_ref[0] * ys[ti]
        o_ref[ti * tm:(ti + 1) * tm, :] = _rms(x2) * gfin_ref[...]


def _combine(seg, x1, pg, gate_f, g_final, ys, seq):
    t, d = x1.shape
    tm = SORT_TM
    ts = SORT_TILES_PER_STEP
    tpb = seq // (ts * tm)
    n_steps = t // (ts * tm)
    n_pos = TOP_K * tm
    seg_w = seg.shape[2]
    smem = functools.partial(pl.BlockSpec, (ts, 1, seg_w), memory_space=pltpu.SMEM)
    return pl.pallas_call(
        functools.partial(_combine_body, n_steps=n_steps),
        out_shape=jax.ShapeDtypeStruct((t, d), F32),
        grid=(n_steps,),
        in_specs=[smem(lambda i: (i, 0, 0)),
                  smem(lambda i: (jnp.minimum(i + 1, n_steps - 1), 0, 0)),
                  pl.BlockSpec((ts * tm, d), lambda i: (i, 0)),
                  pl.BlockSpec((ts * tm, LANES), lambda i: (i, 0)),
                  pl.BlockSpec((1, 1, d), lambda i: (i // tpb, 0, 0)),
                  pl.BlockSpec((1, d), lambda i: (0, 0)),
                  pl.BlockSpec(memory_space=pl.ANY)],
        out_specs=pl.BlockSpec((ts * tm, d), lambda i: (i, 0)),
        scratch_shapes=[pltpu.VMEM((2, ts, n_pos * ROW_TILE, LANES), F32), pltpu.SemaphoreType.DMA((2,))],
        compiler_params=_cparams(("arbitrary",)),
        name="combine",
    )(seg, seg, x1, pg, gate_f, g_final, ys)


def _rope_tables(seq):
    half = ROT_DIMS // 2
    pos = jnp.arange(seq, dtype=F32)
    inv_freq = ROPE_THETA ** (-jnp.arange(0, ROT_DIMS, 2, dtype=F32) / ROT_DIMS)
    ang = pos[:, None] * inv_freq[None, :]
    cos, sin = jnp.cos(ang), jnp.sin(ang)
    ones = jnp.ones((seq, HEAD_DIM - ROT_DIMS), F32)
    zeros = jnp.zeros((seq, HEAD_DIM - ROT_DIMS), F32)
    zh = jnp.zeros((seq, half), F32)
    rep = LANES // HEAD_DIM
    cos_t = jnp.tile(jnp.concatenate([cos, cos, ones], axis=1), (1, rep))
    sa_t = jnp.tile(jnp.concatenate([-sin, zh, zeros], axis=1), (1, rep))
    sb_t = jnp.tile(jnp.concatenate([zh, sin, zeros], axis=1), (1, rep))
    return cos_t, sa_t, sb_t


def _widen_in_proj(w_in):
    d = w_in.shape[0]
    scale = HEAD_DIM ** -0.5
    o = 0
    wqa = w_in[:, o:o + A_Q_W] * scale; o += A_Q_W
    wka = w_in[:, o:o + A_KV_W]; o += A_KV_W
    wva = w_in[:, o:o + A_KV_W]; o += A_KV_W
    wqb = w_in[:, o:o + B_W] * scale; o += B_W
    wkb = w_in[:, o:o + B_W]; o += B_W
    wvb = w_in[:, o:o + B_W]

    def widen(w):
        w = w.reshape(d, A_KV_HEADS, 1, HEAD_DIM)
        return jnp.broadcast_to(w, (d, A_KV_HEADS, A_GROUP, HEAD_DIM)).reshape(d, A_Q_W)

    return jnp.concatenate([wqa, widen(wka), widen(wva), wqb, wkb, wvb], axis=1).astype(BF16)


def _round_up(n, m):
    return (n + m - 1) // m * m


def _moe_tables(tile_cnt, tile_off, n_blk):
    bm = MOE_BM
    nt = tile_cnt.shape[0]
    n_pos = TOP_K * SORT_TM
    cum = jnp.cumsum(tile_cnt, axis=0) - tile_cnt
    counts = jnp.sum(tile_cnt, axis=0)
    padded = ((counts + bm - 1) // bm) * bm
    pends = jnp.cumsum(padded)
    pstarts = pends - padded
    blk_first = jnp.arange(n_blk, dtype=I32) * bm
    blk_expert = jnp.minimum(
        jnp.sum((pends[None, :] <= blk_first[:, None]).astype(I32), axis=1), N_EXPERTS - 1)
    r0 = blk_first - pstarts[blk_expert]
    blk_valid = jnp.clip(counts[blk_expert] - r0, 0, bm)
    cum_b = cum.T[blk_expert]
    cnt_b = tile_cnt.T[blk_expert]
    off_b = tile_off.T[blk_expert]
    lo = jnp.maximum(r0[:, None], cum_b)
    hi = jnp.minimum(r0[:, None] + bm, cum_b + cnt_b)
    size = jnp.maximum(hi - lo, 0)
    tile_ids = jnp.arange(nt, dtype=I32)[None, :]
    src = tile_ids * n_pos + off_b + (lo - cum_b)
    dst = jnp.where(size > 0, lo - r0[:, None], 0)
    first = jnp.min(jnp.where(size > 0, tile_ids, nt), axis=1)
    last = jnp.max(jnp.where(size > 0, tile_ids + 1, 0), axis=1)
    first = jnp.minimum(first, last)
    seg_w = _round_up(3 * nt + 2, LANES)
    exp_seg = jnp.concatenate(
        [src * ROW_TILE, dst * ROW_TILE, size * ROW_TILE, first[:, None], last[:, None],
         jnp.zeros((n_blk, seg_w - 3 * nt - 2), I32)], axis=1).reshape(n_blk, 1, seg_w)
    comb_seg = jnp.concatenate(
        [(pstarts[None, :] + cum) * ROW_TILE, tile_off * ROW_TILE, tile_cnt * ROW_TILE,
         jnp.zeros((nt, LANES - 3 * N_EXPERTS), I32)], axis=1).reshape(nt, 1, LANES)
    return blk_expert, blk_valid, exp_seg, comb_seg


def kernel(x, c, w_ada, b_ada, w_in, sink, rpb, g_out_a, g_out_b, w_out, w_router, b_router,
           w_gate_up, b_gate_up, w_down, b_down, g_final):
    bsz, seq, d = x.shape
    t = bsz * seq
    depth = w_ada.shape[0]
    assert depth == 1, "the final norm is fused into the layer combine"
    bm = MOE_BM
    n_blk = (t * TOP_K) // bm + N_EXPERTS
    nt = t // SORT_TM

    cos_t, sa_t, sb_t = _rope_tables(seq)
    x2 = x.reshape(t, d)
    for l in range(depth):
        mod = _ada(c, w_ada[l], b_ada[l][None, :])
        shift_m, scale_m, gate_m, shift_f, scale_f, gate_f = [
            m.reshape(bsz, 1, d) for m in jnp.split(mod, 6, axis=-1)]

        proj = _inproj(x2, shift_m, scale_m, _widen_in_proj(w_in[l]), cos_t, sa_t, sb_t, seq)
        mix_a = _swa(proj, sink[l], g_out_a[l][None, :], bsz, seq)
        mix_b = _natten(proj, _na_bias_table(rpb[l], seq // GRID_W), g_out_b[l][None, :], bsz, seq)

        w_r = jnp.pad(w_router[l], ((0, 0), (0, LANES - N_EXPERTS)))
        b_r = jnp.pad(b_router[l], (0, LANES - N_EXPERTS), constant_values=NEG)[None, :]
        x1, hs, pg, tile_cnt, tile_off = _outproj(
            x2, mix_a, mix_b, w_out[l].astype(BF16), gate_m, shift_f, scale_f, w_r, b_r, seq)

        blk_expert, blk_valid, exp_seg, comb_seg = _moe_tables(
            tile_cnt[:, 0, :N_EXPERTS].astype(I32), tile_off[:, 0, :N_EXPERTS].astype(I32), n_blk)
        ys = _experts(blk_expert, blk_valid, exp_seg, hs,
                      w_gate_up[l], b_gate_up[l][:, None, :],
                      w_down[l], b_down[l][:, None, :], nt)
        x2 = _combine(comb_seg, x1, pg, gate_f, g_final[None, :], ys, seq)
    return x2.reshape(bsz, seq, d)
```

```python
import functools

import jax
import jax.numpy as jnp
from jax import lax
from jax.experimental import pallas as pl
from jax.experimental.pallas import tpu as pltpu

F32 = jnp.float32
BF16 = jnp.bfloat16
I32 = jnp.int32

HEAD_DIM = 64
A_Q_HEADS = 8
A_KV_HEADS = 2
A_GROUP = A_Q_HEADS // A_KV_HEADS
B_HEADS = 8
A_Q_W = A_Q_HEADS * HEAD_DIM
A_KV_W = A_KV_HEADS * HEAD_DIM
B_W = B_HEADS * HEAD_DIM
WINDOW = 128
ROT_DIMS = HEAD_DIM // 4
ROPE_THETA = 500000.0
GRID_W = 64
NA_ROWS_MAX = 8
NA_COLS = 16
N_EXPERTS = 32
TOP_K = 4
SWIGLU_LIMIT = 7.0
SWIGLU_ALPHA = 1.702
EPS = 1e-5
NEG = -1e30

LANES = 128
ROW_WORDS = 4
PROJ_BLK = 512
TM_PROJ = 512
SORT_TM = 256
SORT_TILES_PER_STEP = 2
MOE_BM = 1024
MOE_SUB = 512
SWA_BLOCKS_PER_STEP = 2
NA_ROWS_PER_STEP = 8
NA_ROW_UNROLL = 4
VMEM_LIMIT = 48 << 20
VMEM_LIMIT_EXPERTS = 58 << 20


def _cparams(sem, limit=VMEM_LIMIT):
    return pltpu.CompilerParams(dimension_semantics=sem, vmem_limit_bytes=limit)


def _rms(x):
    return x * lax.rsqrt(jnp.mean(x * x, axis=-1, keepdims=True) + EPS)


def _pack_bf16_pair(lo, hi):
    return pltpu.pack_elementwise([lo, hi], packed_dtype=BF16)


def _unpack_bf16_pair(words):
    return (pltpu.unpack_elementwise(words, index=0, packed_dtype=BF16, unpacked_dtype=F32),
            pltpu.unpack_elementwise(words, index=1, packed_dtype=BF16, unpacked_dtype=F32))


def _rows_to_words(x):
    return [_pack_bf16_pair(x[:, 2 * c * LANES:(2 * c + 1) * LANES], x[:, (2 * c + 1) * LANES:(2 * c + 2) * LANES])
            for c in range(ROW_WORDS)]


def _words_to_rows(words):
    parts = []
    for w in words:
        parts.extend(_unpack_bf16_pair(w))
    return jnp.concatenate(parts, axis=1).astype(BF16)


def _ada_body(c_ref, w_ref, b_ref, o_ref):
    c = c_ref[...]
    ca = c / (1.0 + jnp.exp(-c))
    o_ref[...] = jnp.dot(ca, w_ref[...], preferred_element_type=F32,
                         precision=lax.Precision.HIGHEST) + b_ref[...]


def _ada(c, w, b):
    bsz, d = c.shape
    n = w.shape[1]
    bn = n // 4
    return pl.pallas_call(
        _ada_body,
        out_shape=jax.ShapeDtypeStruct((bsz, n), F32),
        grid=(4,),
        in_specs=[pl.BlockSpec((bsz, d), lambda i: (0, 0)),
                  pl.BlockSpec((d, bn), lambda i: (0, i)),
                  pl.BlockSpec((1, bn), lambda i: (0, i))],
        out_specs=pl.BlockSpec((bsz, bn), lambda i: (0, i)),
        compiler_params=_cparams(("arbitrary",)),
        name="ada",
    )(c, w, b)


def _inproj_body(x_ref, sh_ref, sc_ref, w_ref, cos_ref, sa_ref, sb_ref, o_ref):
    h = (_rms(x_ref[...]) * (1.0 + sc_ref[0]) + sh_ref[0]).astype(BF16)
    cos = cos_ref[...]
    sa = sa_ref[...]
    sb = sb_ref[...]
    n_blk = w_ref.shape[1] // PROJ_BLK
    for cb in range(n_blk):
        p = jnp.dot(h, w_ref[:, cb * PROJ_BLK:(cb + 1) * PROJ_BLK], preferred_element_type=F32)
        if cb < 2:
            parts = []
            for j in range(PROJ_BLK // LANES):
                pj = p[:, j * LANES:(j + 1) * LANES]
                parts.append(pj * cos + pltpu.roll(pj, LANES - ROT_DIMS // 2, axis=1) * sa
                             + pltpu.roll(pj, ROT_DIMS // 2, axis=1) * sb)
            p = jnp.concatenate(parts, axis=1)
        o_ref[:, cb * PROJ_BLK:(cb + 1) * PROJ_BLK] = p.astype(BF16)


def _inproj(x2, shift, scale, w_ext, cos_t, sa_t, sb_t, seq):
    t, d = x2.shape
    n = w_ext.shape[1]
    tm = TM_PROJ
    tpb = seq // tm
    return pl.pallas_call(
        _inproj_body,
        out_shape=jax.ShapeDtypeStruct((t, n), BF16),
        grid=(t // tm,),
        in_specs=[pl.BlockSpec((tm, d), lambda i: (i, 0)),
                  pl.BlockSpec((1, 1, d), lambda i: (i // tpb, 0, 0)),
                  pl.BlockSpec((1, 1, d), lambda i: (i // tpb, 0, 0)),
                  pl.BlockSpec((d, n), lambda i: (0, 0)),
                  pl.BlockSpec((tm, LANES), lambda i: (i % tpb, 0)),
                  pl.BlockSpec((tm, LANES), lambda i: (i % tpb, 0)),
                  pl.BlockSpec((tm, LANES), lambda i: (i % tpb, 0))],
        out_specs=pl.BlockSpec((tm, n), lambda i: (i, 0)),
        compiler_params=_cparams(("arbitrary",)),
        name="inproj",
    )(x2, shift, scale, w_ext, cos_t, sa_t, sb_t)


def _swa_body(sink_ref, q_ref, k_ref, v_ref, g_ref, o_ref, *, seq):
    step = pl.program_id(1)
    blk = WINDOW
    kw_len = 3 * blk
    gw = A_GROUP * HEAD_DIM
    srows = A_GROUP * blk
    row = lax.broadcasted_iota(I32, (srows, 1), 0)
    grp = lax.broadcasted_iota(I32, (1, gw), 1) // HEAD_DIM
    blocks = range(SWA_BLOCKS_PER_STEP)
    units = [(bi, h) for bi in blocks for h in range(A_KV_HEADS)]
    kws, vws, alloweds = [], [], []
    for bi in blocks:
        n = step * SWA_BLOCKS_PER_STEP + bi
        start = pl.multiple_of(jnp.clip(n * blk - blk, 0, seq - kw_len), blk)
        kws.append(k_ref[pl.ds(start, kw_len), :])
        vws.append(v_ref[pl.ds(start, kw_len), :])
        qpos = n * blk + row % blk
        kpos = start + lax.broadcasted_iota(I32, (1, kw_len), 1)
        alloweds.append(jnp.abs(qpos - kpos) <= WINDOW)
    sks = []
    for h in range(A_KV_HEADS):
        sk = jnp.full((srows, 1), sink_ref[h * A_GROUP], F32)
        for g in range(1, A_GROUP):
            sk = jnp.where(row // blk == g, sink_ref[h * A_GROUP + g], sk)
        sks.append(sk)
    qms = []
    for bi, h in units:
        qh = q_ref[bi * blk:(bi + 1) * blk, h * gw:(h + 1) * gw]
        qms.append(jnp.concatenate(
            [jnp.where(grp == g, qh, jnp.zeros_like(qh)) for g in range(A_GROUP)], axis=0))
    ss = [lax.dot_general(qms[u], kws[bi][:, h * gw:(h + 1) * gw], (((1,), (1,)), ((), ())),
                          preferred_element_type=F32) for u, (bi, h) in enumerate(units)]
    ss = [jnp.where(alloweds[bi], ss[u], NEG) for u, (bi, h) in enumerate(units)]
    ms = [jnp.maximum(jnp.max(ss[u], axis=-1, keepdims=True), sks[h]) for u, (bi, h) in enumerate(units)]
    ps = [jnp.exp(ss[u] - ms[u]) for u in range(len(units))]
    invs = [1.0 / (jnp.sum(ps[u], axis=-1, keepdims=True) + jnp.exp(sks[h] - ms[u]))
            for u, (bi, h) in enumerate(units)]
    pns = [(ps[u] * invs[u]).astype(BF16) for u in range(len(units))]
    ress = [jnp.dot(pns[u], vws[bi][:, h * gw:(h + 1) * gw], preferred_element_type=F32)
            for u, (bi, h) in enumerate(units)]
    accs = []
    for res in ress:
        acc = res[0:blk]
        for g in range(1, A_GROUP):
            acc = jnp.where(grp == g, res[g * blk:(g + 1) * blk], acc)
        accs.append(acc)
    for bi in blocks:
        oa = jnp.concatenate(accs[bi * A_KV_HEADS:(bi + 1) * A_KV_HEADS], axis=1)
        o_ref[bi * blk:(bi + 1) * blk, :] = (_rms(oa) * g_ref[...]).astype(BF16)


def _swa(proj, sink, g_a, bsz, seq):
    t = proj.shape[0]
    blk = WINDOW * SWA_BLOCKS_PER_STEP
    nq = seq // blk
    grid_spec = pltpu.PrefetchScalarGridSpec(
        num_scalar_prefetch=1, grid=(bsz, nq),
        in_specs=[pl.BlockSpec((blk, PROJ_BLK), lambda b, n, s: (b * nq + n, 0)),
                  pl.BlockSpec((seq, PROJ_BLK), lambda b, n, s: (b, 1)),
                  pl.BlockSpec((seq, PROJ_BLK), lambda b, n, s: (b, 2)),
                  pl.BlockSpec((1, A_Q_W), lambda b, n, s: (0, 0))],
        out_specs=pl.BlockSpec((blk, A_Q_W), lambda b, n, s: (b * nq + n, 0)))
    return pl.pallas_call(
        functools.partial(_swa_body, seq=seq),
        out_shape=jax.ShapeDtypeStruct((t, A_Q_W), BF16),
        grid_spec=grid_spec,
        compiler_params=_cparams(("arbitrary", "arbitrary")),
        name="swa",
    )(sink, proj, proj, proj, g_a)


def _na_body(q_ref, k_ref, v_ref, bias_ref, g_ref, o_ref, *, rows, kr, rb):
    blk = pl.program_id(1)
    per = LANES // HEAD_DIM
    n_pair = B_W // LANES
    half = lax.broadcasted_iota(I32, (1, LANES), 1) // HEAD_DIM
    gain = g_ref[...]

    def one_row(i, carry):
        r = blk * rb + i
        rs = jnp.clip(r - kr // 2, 0, rows - kr)
        start = pl.multiple_of(rs * GRID_W, GRID_W)
        kw = k_ref[pl.ds(start, kr * GRID_W), :]
        vw = v_ref[pl.ds(start, kr * GRID_W), :]
        q = q_ref[pl.ds(pl.multiple_of(i * GRID_W, GRID_W), GRID_W), :]
        s_parts = []
        for pr in range(n_pair):
            qp = q[:, pr * LANES:(pr + 1) * LANES]
            qm = jnp.concatenate([jnp.where(half == j, qp, jnp.zeros_like(qp)) for j in range(per)], axis=0)
            s_parts.append(lax.dot_general(qm, kw[:, pr * LANES:(pr + 1) * LANES],
                                           (((1,), (1,)), ((), ())), preferred_element_type=F32))
        s = jnp.concatenate(s_parts, axis=0) + bias_ref[r - rs]
        m = jnp.max(s, axis=-1, keepdims=True)
        p = jnp.exp(s - m)
        pn = (p * (1.0 / jnp.sum(p, axis=-1, keepdims=True))).astype(BF16)
        o_parts = []
        for pr in range(n_pair):
            res = jnp.dot(pn[pr * per * GRID_W:(pr + 1) * per * GRID_W], vw[:, pr * LANES:(pr + 1) * LANES],
                          preferred_element_type=F32)
            op = res[0:GRID_W]
            for j in range(1, per):
                op = jnp.where(half == j, res[j * GRID_W:(j + 1) * GRID_W], op)
            o_parts.append(op)
        ob = jnp.concatenate(o_parts, axis=1)
        o_ref[pl.ds(pl.multiple_of(i * GRID_W, GRID_W), GRID_W), :] = (_rms(ob) * gain).astype(BF16)
        return carry

    def row_group(ig, carry):
        for u in range(NA_ROW_UNROLL):
            one_row(NA_ROW_UNROLL * ig + u, carry)
        return carry

    lax.fori_loop(0, rb // NA_ROW_UNROLL, row_group, 0)


def _natten(proj, bias_t, g_b, bsz, seq):
    t = proj.shape[0]
    rows = seq // GRID_W
    kr = min(NA_ROWS_MAX, rows)
    rb = NA_ROWS_PER_STEP
    nb = rows // rb
    return pl.pallas_call(
        functools.partial(_na_body, rows=rows, kr=kr, rb=rb),
        out_shape=jax.ShapeDtypeStruct((t, B_W), BF16),
        grid=(bsz, nb),
        in_specs=[pl.BlockSpec((rb * GRID_W, PROJ_BLK), lambda b, r: (b * nb + r, 3)),
                  pl.BlockSpec((seq, PROJ_BLK), lambda b, r: (b, 4)),
                  pl.BlockSpec((seq, PROJ_BLK), lambda b, r: (b, 5)),
                  pl.BlockSpec((kr, B_HEADS * GRID_W, kr * GRID_W), lambda b, r: (0, 0, 0)),
                  pl.BlockSpec((1, B_W), lambda b, r: (0, 0))],
        out_specs=pl.BlockSpec((rb * GRID_W, B_W), lambda b, r: (b * nb + r, 0)),
        compiler_params=_cparams(("arbitrary", "arbitrary")),
        name="natten",
    )(proj, proj, proj, bias_t, g_b)


def _na_bias_table(rpb, rows):
    kr = min(NA_ROWS_MAX, rows)
    kc = NA_COLS
    col = jnp.arange(GRID_W)
    col_start = jnp.clip(col - kc // 2, 0, GRID_W - kc)
    col_mask = (col[None, :] >= col_start[:, None]) & (col[None, :] < col_start[:, None] + kc)
    col_off = jnp.clip(col[None, :] - col[:, None], -(kc - 1), kc - 1) + (NA_COLS - 1)
    rows_sel = jnp.stack([rpb[:, NA_ROWS_MAX - 1 - cs:NA_ROWS_MAX - 1 - cs + kr, :]
                          for cs in range(kr)], axis=0)
    pick = (col_off[None, :, :] == jnp.arange(2 * kc - 1)[:, None, None]).astype(F32)
    tbl = jnp.einsum('chjr,rqk->chqjk', rows_sel, pick, precision=lax.Precision.HIGHEST)
    tbl = jnp.where(col_mask[None, None, :, None, :], tbl, NEG)
    return tbl.reshape(kr, rpb.shape[0] * GRID_W, kr * GRID_W).astype(F32)


def _outproj_body(x_ref, ma_ref, mb_ref, wo_ref, gm_ref, sf_ref, cf_ref, wr_ref, br_ref,
                  x1_ref, hs_ref, pg_ref, tc_ref, to_ref):
    tm = SORT_TM
    tiles = range(SORT_TILES_PER_STEP)
    tt = tm * SORT_TILES_PER_STEP
    n_pos = TOP_K * tm
    mix = (jnp.dot(ma_ref[...], wo_ref[:A_Q_W, :], preferred_element_type=F32)
           + jnp.dot(mb_ref[...], wo_ref[A_Q_W:, :], preferred_element_type=F32))
    x1 = x_ref[...] + gm_ref[0] * mix
    x1_ref[...] = x1
    h = _rms(x1) * (1.0 + cf_ref[0]) + sf_ref[0]
    h_hi = h.astype(BF16)
    h_lo = (h - h_hi.astype(F32)).astype(BF16)
    w_r = wr_ref[...]
    w_hi = w_r.astype(BF16)
    w_lo = (w_r - w_hi.astype(F32)).astype(BF16)
    l_hi = jnp.dot(h_hi, jnp.concatenate([w_hi, w_lo], axis=1), preferred_element_type=F32)
    logits = (l_hi[:, :LANES] + l_hi[:, LANES:]
              + jnp.dot(h_lo, w_hi, preferred_element_type=F32) + br_ref[...])
    lane = lax.broadcasted_iota(I32, (tt, LANES), 1)
    work = logits
    vals, idxs = [], []
    for _ in range(TOP_K):
        m = jnp.max(work, axis=-1, keepdims=True)
        ik = jnp.min(jnp.where(work == m, lane, LANES), axis=-1, keepdims=True)
        vals.append(m)
        idxs.append(ik)
        work = jnp.where(lane == ik, -jnp.inf, work)
    es = [jnp.exp(v - vals[0]) for v in vals]
    den = es[0] + es[1] + es[2] + es[3]
    mh = jnp.zeros((tt, LANES), F32)
    for ik in idxs:
        mh = mh + (lane == ik).astype(F32)
    mhb = mh.astype(BF16)
    tri = (lax.broadcasted_iota(I32, (tm, tm), 0) > lax.broadcasted_iota(I32, (tm, tm), 1)).astype(BF16)
    lower = (lax.broadcasted_iota(I32, (LANES, LANES), 0)
             < lax.broadcasted_iota(I32, (LANES, LANES), 1)).astype(BF16)
    mh_t = [mhb[ti * tm:(ti + 1) * tm] for ti in tiles]
    earlier = [jnp.dot(tri, mh_t[ti], preferred_element_type=F32) for ti in tiles]
    below = [jnp.dot(mh_t[ti], lower, preferred_element_type=F32) for ti in tiles]
    toff = [jnp.sum(below[ti], axis=0, keepdims=True) for ti in tiles]
    cnt = [jnp.sum(mh[ti * tm:(ti + 1) * tm], axis=0, keepdims=True) for ti in tiles]
    posf = jnp.concatenate([earlier[ti] + toff[ti] for ti in tiles], axis=0)
    pg = jnp.zeros((tt, LANES), F32)
    for k in range(TOP_K):
        pos_k = jnp.sum(jnp.where(lane == idxs[k], posf, 0.0), axis=-1, keepdims=True)
        pg = jnp.where(lane == k, pos_k, pg)
        pg = jnp.where(lane == TOP_K + k, es[k] / den, pg)
    pg_ref[...] = pg
    pgt_all = pg.T[:ROW_WORDS]
    pgt = [pgt_all[:, ti * tm:(ti + 1) * tm] for ti in tiles]
    for ti in tiles:
        tc_ref[ti] = cnt[ti]
        to_ref[ti] = toff[ti]
    pos_iota = lax.broadcasted_iota(I32, (n_pos, tm), 0)
    hits = []
    for ti in tiles:
        hit = pos_iota == pgt[ti][0:1, :].astype(I32)
        for k in range(1, TOP_K):
            hit = hit | (pos_iota == pgt[ti][k:k + 1, :].astype(I32))
        hits.append(jnp.where(hit, 1.0, 0.0).astype(BF16))
    srt = [jnp.dot(hits[ti], h_hi[ti * tm:(ti + 1) * tm], preferred_element_type=F32) for ti in tiles]
    for ti in tiles:
        for c, words in enumerate(_rows_to_words(srt[ti])):
            hs_ref[pl.ds(ti * n_pos * ROW_WORDS + c, n_pos, stride=ROW_WORDS), :] = words


def _outproj(x2, mix_a, mix_b, w_out, gate_m, shift_f, scale_f, w_r, b_r, seq):
    t, d = x2.shape
    tm = SORT_TM
    nt = t // tm
    ts = SORT_TILES_PER_STEP
    tpb = seq // (ts * tm)
    n_pos = TOP_K * tm
    mod_spec = pl.BlockSpec((1, 1, d), lambda i: (i // tpb, 0, 0))
    return pl.pallas_call(
        _outproj_body,
        out_shape=(jax.ShapeDtypeStruct((t, d), F32),
                   jax.ShapeDtypeStruct((t * TOP_K * ROW_WORDS, LANES), I32),
                   jax.ShapeDtypeStruct((t, LANES), F32),
                   jax.ShapeDtypeStruct((nt, 1, LANES), F32),
                   jax.ShapeDtypeStruct((nt, 1, LANES), F32)),
        grid=(nt // ts,),
        in_specs=[pl.BlockSpec((ts * tm, d), lambda i: (i, 0)),
                  pl.BlockSpec((ts * tm, A_Q_W), lambda i: (i, 0)),
                  pl.BlockSpec((ts * tm, B_W), lambda i: (i, 0)),
                  pl.BlockSpec((A_Q_W + B_W, d), lambda i: (0, 0)),
                  mod_spec, mod_spec, mod_spec,
                  pl.BlockSpec((d, LANES), lambda i: (0, 0)),
                  pl.BlockSpec((1, LANES), lambda i: (0, 0))],
        out_specs=(pl.BlockSpec((ts * tm, d), lambda i: (i, 0)),
                   pl.BlockSpec((ts * n_pos * ROW_WORDS, LANES), lambda i: (i, 0)),
                   pl.BlockSpec((ts * tm, LANES), lambda i: (i, 0)),
                   pl.BlockSpec((ts, 1, LANES), lambda i: (i, 0, 0)),
                   pl.BlockSpec((ts, 1, LANES), lambda i: (i, 0, 0))),
        compiler_params=_cparams(("arbitrary",)),
        name="outproj",
    )(x2, mix_a, mix_b, w_out, gate_m, shift_f, scale_f, w_r, b_r)


def _segment_copies(seg_ref, n_seg, first, last, src_hbm, dst_buf, sem, tbl=0):
    def body(i, carry):
        size = pl.multiple_of(seg_ref[tbl, 0, 2 * n_seg + i], ROW_WORDS)

        @pl.when(size > 0)
        def _():
            src = pl.multiple_of(seg_ref[tbl, 0, i], ROW_WORDS)
            dst = pl.multiple_of(seg_ref[tbl, 0, n_seg + i], ROW_WORDS)
            pltpu.make_async_copy(src_hbm.at[pl.ds(src, size), :], dst_buf.at[pl.ds(dst, size), :], sem).start()
        return carry

    lax.fori_loop(first, last, body, 0)


def _experts_body(be_ref, vl_ref, seg0_ref, segn_ref, hs_hbm, wgu32_ref, bgu_ref, wd32_ref, bd_ref, o_ref,
                  xbuf, xs_ref, wgu_ref, wd_ref, gsem, *, n_blk, n_seg):
    bm = MOE_BM
    rt = ROW_WORDS
    d_ff = wd_ref.shape[0]
    j = pl.program_id(0)
    slot = j % 2
    other = 1 - slot
    valid = vl_ref[j]

    def issue(seg_ref, s):
        _segment_copies(seg_ref, n_seg, seg_ref[0, 0, 3 * n_seg], seg_ref[0, 0, 3 * n_seg + 1],
                        hs_hbm, xbuf.at[s], gsem.at[s])

    @pl.when(j == 0)
    def _():
        xbuf[...] = jnp.zeros_like(xbuf)
        issue(seg0_ref, 0)

    @pl.when((j == 0) | (be_ref[j] != be_ref[jnp.maximum(j - 1, 0)]))
    def _():
        for c in range(d_ff // LANES):
            wgu_ref[:, 2 * c * LANES:(2 * c + 1) * LANES] = (
                wgu32_ref[0, :, c * LANES:(c + 1) * LANES].astype(BF16))
            wgu_ref[:, (2 * c + 1) * LANES:(2 * c + 2) * LANES] = (
                wgu32_ref[0, :, d_ff + c * LANES:d_ff + (c + 1) * LANES].astype(BF16))
        wd_ref[...] = wd32_ref[0].astype(BF16)

    @pl.when(valid > 0)
    def _():
        rows = pl.multiple_of(valid * rt, rt)
        pltpu.make_async_copy(hs_hbm.at[pl.ds(0, rows), :], xbuf.at[slot, pl.ds(0, rows), :],
                              gsem.at[slot]).wait()

    @pl.when(j + 1 < n_blk)
    def _():
        issue(segn_ref, other)

    sub = MOE_SUB
    n_sub = (valid + sub - 1) // sub

    def zero_sub(sb, carry):
        r0 = pl.multiple_of(sb * sub * rt, sub * rt)
        o_ref[pl.ds(r0, sub * rt), :] = jnp.zeros((sub * rt, LANES), I32)
        return carry

    lax.fori_loop(n_sub, bm // sub, zero_sub, 0)

    def compute_sub(sb, carry):
        r0 = pl.multiple_of(sb * sub * rt, sub * rt)
        xs_ref[...] = _words_to_rows([xbuf[slot, pl.ds(r0 + c, sub, stride=rt), :] for c in range(rt)])
        xs = xs_ref[...]
        acts = []
        for c in range(d_ff // LANES):
            gu = jnp.dot(xs, wgu_ref[:, 2 * c * LANES:(2 * c + 2) * LANES], preferred_element_type=F32)
            gate = gu[:, :LANES] + bgu_ref[0, :, c * LANES:(c + 1) * LANES]
            up = gu[:, LANES:] + bgu_ref[0, :, d_ff + c * LANES:d_ff + (c + 1) * LANES]
            gate = jnp.minimum(gate, SWIGLU_LIMIT)
            up = jnp.clip(up, -SWIGLU_LIMIT, SWIGLU_LIMIT)
            glu = gate / (1.0 + jnp.exp(-SWIGLU_ALPHA * gate))
            acts.append(((up + 1.0) * glu).astype(BF16))
        act = jnp.concatenate(acts, axis=1)
        ys = jnp.dot(act, wd_ref[...], preferred_element_type=F32) + bd_ref[0]
        for c, words in enumerate(_rows_to_words(ys)):
            o_ref[pl.ds(r0 + c, sub, stride=rt), :] = words
        return carry

    lax.fori_loop(0, n_sub, compute_sub, 0)


def _experts(blk_expert, blk_valid, seg, hs, wgu, bgu, wd, bd, n_seg):
    bm = MOE_BM
    rt = ROW_WORDS
    n_blk, _, seg_w = seg.shape
    d = wd.shape[2]
    d_ff = wd.shape[1]
    assert d == 2 * rt * LANES and hs.shape[1] == LANES
    smem = functools.partial(pl.BlockSpec, (1, 1, seg_w), memory_space=pltpu.SMEM)
    last = n_blk - 1
    grid_spec = pltpu.PrefetchScalarGridSpec(
        num_scalar_prefetch=2, grid=(n_blk,),
        in_specs=[smem(lambda j, be, vl: (j, 0, 0)),
                  smem(lambda j, be, vl: (jnp.minimum(j + 1, last), 0, 0)),
                  pl.BlockSpec(memory_space=pl.ANY),
                  pl.BlockSpec((1, d, 2 * d_ff), lambda j, be, vl: (be[j], 0, 0)),
                  pl.BlockSpec((1, 1, 2 * d_ff), lambda j, be, vl: (be[j], 0, 0)),
                  pl.BlockSpec((1, d_ff, d), lambda j, be, vl: (be[j], 0, 0)),
                  pl.BlockSpec((1, 1, d), lambda j, be, vl: (be[j], 0, 0))],
        out_specs=pl.BlockSpec((bm * rt, LANES), lambda j, be, vl: (j, 0)),
        scratch_shapes=[pltpu.VMEM((2, bm * rt, LANES), I32), pltpu.VMEM((MOE_SUB, d), BF16),
                        pltpu.VMEM((d, 2 * d_ff), BF16), pltpu.VMEM((d_ff, d), BF16),
                        pltpu.SemaphoreType.DMA((2,))])
    return pl.pallas_call(
        functools.partial(_experts_body, n_blk=n_blk, n_seg=n_seg),
        out_shape=jax.ShapeDtypeStruct((n_blk * bm * rt, LANES), I32),
        grid_spec=grid_spec,
        compiler_params=_cparams(("arbitrary",), VMEM_LIMIT_EXPERTS),
        name="experts",
    )(blk_expert, blk_valid, seg, seg, hs, wgu, bgu, wd, bd)


def _combine_body(seg0_ref, segn_ref, x1_ref, pg_ref, gf_ref, gfin_ref, ys_hbm, o_ref,
                  ybuf, sem, *, n_steps):
    tm = SORT_TM
    tiles = range(SORT_TILES_PER_STEP)
    n_pos = TOP_K * tm
    rt = ROW_WORDS
    i = pl.program_id(0)
    slot = i % 2
    other = 1 - slot

    def issue(seg_ref, s):
        for ti in tiles:
            _segment_copies(seg_ref, N_EXPERTS, 0, N_EXPERTS, ys_hbm, ybuf.at[s, ti], sem.at[s], tbl=ti)

    @pl.when(i == 0)
    def _():
        issue(seg0_ref, 0)

    for ti in tiles:
        pltpu.make_async_copy(ys_hbm.at[pl.ds(0, n_pos * rt), :], ybuf.at[slot, ti], sem.at[slot]).wait()

    @pl.when(i + 1 < n_steps)
    def _():
        issue(segn_ref, other)

    lane_pos = lax.broadcasted_iota(I32, (tm, n_pos), 1)
    ws = []
    for ti in tiles:
        pg = pg_ref[ti * tm:(ti + 1) * tm, :]
        w = jnp.zeros((tm, n_pos), F32)
        for k in range(TOP_K):
            w = jnp.where(lane_pos == pg[:, k:k + 1].astype(I32), pg[:, TOP_K + k:TOP_K + k + 1], w)
        ws.append(w)
    w_his = [w.astype(BF16) for w in ws]
    w_los = [(ws[ti] - w_his[ti].astype(F32)).astype(BF16) for ti in tiles]
    ysbs = [_words_to_rows([ybuf[slot, ti, pl.ds(c, n_pos, stride=rt), :] for c in range(rt)])
            for ti in tiles]
    ys = [jnp.dot(w_his[ti], ysbs[ti], preferred_element_type=F32)
          + jnp.dot(w_los[ti], ysbs[ti], preferred_element_type=F32) for ti in tiles]
    for ti in tiles:
        x2 = x1_ref[ti * tm:(ti + 1) * tm, :] + gf_ref[0] * ys[ti]
        o_ref[ti * tm:(ti + 1) * tm, :] = _rms(x2) * gfin_ref[...]


def _combine(seg, x1, pg, gate_f, g_final, ys, seq):
    t, d = x1.shape
    tm = SORT_TM
    ts = SORT_TILES_PER_STEP
    tpb = seq // (ts * tm)
    n_steps = t // (ts * tm)
    n_pos = TOP_K * tm
    seg_w = seg.shape[2]
    smem = functools.partial(pl.BlockSpec, (ts, 1, seg_w), memory_space=pltpu.SMEM)
    return pl.pallas_call(
        functools.partial(_combine_body, n_steps=n_steps),
        out_shape=jax.ShapeDtypeStruct((t, d), F32),
        grid=(n_steps,),
        in_specs=[smem(lambda i: (i, 0, 0)),
                  smem(lambda i: (jnp.minimum(i + 1, n_steps - 1), 0, 0)),
                  pl.BlockSpec((ts * tm, d), lambda i: (i, 0)),
                  pl.BlockSpec((ts * tm, LANES), lambda i: (i, 0)),
                  pl.BlockSpec((1, 1, d), lambda i: (i // tpb, 0, 0)),
                  pl.BlockSpec((1, d), lambda i: (0, 0)),
                  pl.BlockSpec(memory_space=pl.ANY)],
        out_specs=pl.BlockSpec((ts * tm, d), lambda i: (i, 0)),
        scratch_shapes=[pltpu.VMEM((2, ts, n_pos * ROW_WORDS, LANES), I32), pltpu.SemaphoreType.DMA((2,))],
        compiler_params=_cparams(("arbitrary",)),
        name="combine",
    )(seg, seg, x1, pg, gate_f, g_final, ys)


def _rope_tables(seq):
    half = ROT_DIMS // 2
    pos = jnp.arange(seq, dtype=F32)
    inv_freq = ROPE_THETA ** (-jnp.arange(0, ROT_DIMS, 2, dtype=F32) / ROT_DIMS)
    ang = pos[:, None] * inv_freq[None, :]
    cos, sin = jnp.cos(ang), jnp.sin(ang)
    ones = jnp.ones((seq, HEAD_DIM - ROT_DIMS), F32)
    zeros = jnp.zeros((seq, HEAD_DIM - ROT_DIMS), F32)
    zh = jnp.zeros((seq, half), F32)
    rep = LANES // HEAD_DIM
    cos_t = jnp.tile(jnp.concatenate([cos, cos, ones], axis=1), (1, rep))
    sa_t = jnp.tile(jnp.concatenate([-sin, zh, zeros], axis=1), (1, rep))
    sb_t = jnp.tile(jnp.concatenate([zh, sin, zeros], axis=1), (1, rep))
    return cos_t, sa_t, sb_t


def _widen_in_proj(w_in):
    d = w_in.shape[0]
    scale = HEAD_DIM ** -0.5
    o = 0
    wqa = w_in[:, o:o + A_Q_W] * scale; o += A_Q_W
    wka = w_in[:, o:o + A_KV_W]; o += A_KV_W
    wva = w_in[:, o:o + A_KV_W]; o += A_KV_W
    wqb = w_in[:, o:o + B_W] * scale; o += B_W
    wkb = w_in[:, o:o + B_W]; o += B_W
    wvb = w_in[:, o:o + B_W]

    def widen(w):
        w = w.reshape(d, A_KV_HEADS, 1, HEAD_DIM)
        return jnp.broadcast_to(w, (d, A_KV_HEADS, A_GROUP, HEAD_DIM)).reshape(d, A_Q_W)

    return jnp.concatenate([wqa, widen(wka), widen(wva), wqb, wkb, wvb], axis=1).astype(BF16)


def _round_up(n, m):
    return (n + m - 1) // m * m


def _moe_tables(tile_cnt, tile_off, n_blk):
    bm = MOE_BM
    nt = tile_cnt.shape[0]
    n_pos = TOP_K * SORT_TM
    cum = jnp.cumsum(tile_cnt, axis=0) - tile_cnt
    counts = jnp.sum(tile_cnt, axis=0)
    padded = ((counts + bm - 1) // bm) * bm
    pends = jnp.cumsum(padded)
    pstarts = pends - padded
    blk_first = jnp.arange(n_blk, dtype=I32) * bm
    blk_expert = jnp.minimum(
        jnp.sum((pends[None, :] <= blk_first[:, None]).astype(I32), axis=1), N_EXPERTS - 1)
    r0 = blk_first - pstarts[blk_expert]
    blk_valid = jnp.clip(counts[blk_expert] - r0, 0, bm)
    cum_b = cum.T[blk_expert]
    cnt_b = tile_cnt.T[blk_expert]
    off_b = tile_off.T[blk_expert]
    lo = jnp.maximum(r0[:, None], cum_b)
    hi = jnp.minimum(r0[:, None] + bm, cum_b + cnt_b)
    size = jnp.maximum(hi - lo, 0)
    tile_ids = jnp.arange(nt, dtype=I32)[None, :]
    src = tile_ids * n_pos + off_b + (lo - cum_b)
    dst = jnp.where(size > 0, lo - r0[:, None], 0)
    first = jnp.min(jnp.where(size > 0, tile_ids, nt), axis=1)
    last = jnp.max(jnp.where(size > 0, tile_ids + 1, 0), axis=1)
    first = jnp.minimum(first, last)
    seg_w = _round_up(3 * nt + 2, LANES)
    exp_seg = jnp.concatenate(
        [src * ROW_WORDS, dst * ROW_WORDS, size * ROW_WORDS, first[:, None], last[:, None],
         jnp.zeros((n_blk, seg_w - 3 * nt - 2), I32)], axis=1).reshape(n_blk, 1, seg_w)
    comb_seg = jnp.concatenate(
        [(pstarts[None, :] + cum) * ROW_WORDS, tile_off * ROW_WORDS, tile_cnt * ROW_WORDS,
         jnp.zeros((nt, LANES - 3 * N_EXPERTS), I32)], axis=1).reshape(nt, 1, LANES)
    return blk_expert, blk_valid, exp_seg, comb_seg


def kernel(x, c, w_ada, b_ada, w_in, sink, rpb, g_out_a, g_out_b, w_out, w_router, b_router,
           w_gate_up, b_gate_up, w_down, b_down, g_final):
    bsz, seq, d = x.shape
    t = bsz * seq
    depth = w_ada.shape[0]
    assert depth == 1, "the final norm is fused into the layer combine"
    bm = MOE_BM
    n_blk = (t * TOP_K) // bm + N_EXPERTS
    nt = t // SORT_TM

    cos_t, sa_t, sb_t = _rope_tables(seq)
    x2 = x.reshape(t, d)
    for l in range(depth):
        mod = _ada(c, w_ada[l], b_ada[l][None, :])
        shift_m, scale_m, gate_m, shift_f, scale_f, gate_f = [
            m.reshape(bsz, 1, d) for m in jnp.split(mod, 6, axis=-1)]

        proj = _inproj(x2, shift_m, scale_m, _widen_in_proj(w_in[l]), cos_t, sa_t, sb_t, seq)
        mix_a = _swa(proj, sink[l], g_out_a[l][None, :], bsz, seq)
        mix_b = _natten(proj, _na_bias_table(rpb[l], seq // GRID_W), g_out_b[l][None, :], bsz, seq)

        w_r = jnp.pad(w_router[l], ((0, 0), (0, LANES - N_EXPERTS)))
        b_r = jnp.pad(b_router[l], (0, LANES - N_EXPERTS), constant_values=NEG)[None, :]
        x1, hs, pg, tile_cnt, tile_off = _outproj(
            x2, mix_a, mix_b, w_out[l].astype(BF16), gate_m, shift_f, scale_f, w_r, b_r, seq)

        blk_expert, blk_valid, exp_seg, comb_seg = _moe_tables(
            tile_cnt[:, 0, :N_EXPERTS].astype(I32), tile_off[:, 0, :N_EXPERTS].astype(I32), n_blk)
        ys = _experts(blk_expert, blk_valid, exp_seg, hs,
                      w_gate_up[l], b_gate_up[l][:, None, :],
                      w_down[l], b_down[l][:, None, :], nt)
        x2 = _combine(comb_seg, x1, pg, gate_f, g_final[None, :], ys, seq)
    return x2.reshape(bsz, seq, d)
```

```python
import functools

import jax
import jax.numpy as jnp
from jax import lax
from jax.experimental import pallas as pl
from jax.experimental.pallas import tpu as pltpu

F32 = jnp.float32
BF16 = jnp.bfloat16
I32 = jnp.int32

HEAD_DIM = 64
A_Q_HEADS = 8
A_KV_HEADS = 2
A_GROUP = A_Q_HEADS // A_KV_HEADS
B_HEADS = 8
A_Q_W = A_Q_HEADS * HEAD_DIM
A_KV_W = A_KV_HEADS * HEAD_DIM
B_W = B_HEADS * HEAD_DIM
WINDOW = 128
ROT_DIMS = HEAD_DIM // 4
ROPE_THETA = 500000.0
GRID_W = 64
NA_ROWS_MAX = 8
NA_COLS = 16
N_EXPERTS = 32
TOP_K = 4
SWIGLU_LIMIT = 7.0
SWIGLU_ALPHA = 1.702
EPS = 1e-5
NEG = -1e30
LOG2E = 1.4426950408889634

LANES = 128
ROW_WORDS = 4
PROJ_BLK = 512
TM_PROJ = 512
SORT_TM = 256
SORT_TILES_PER_STEP = 2
MOE_BM = 1024
MOE_SUB = 512
SWA_BLOCKS_PER_STEP = 2
SWA_STAGED_BLOCKS = 1
NA_ROWS_PER_STEP = 8
NA_ROW_UNROLL = 4
VMEM_LIMIT = 48 << 20
VMEM_LIMIT_EXPERTS = 58 << 20


def _cparams(sem, limit=VMEM_LIMIT):
    return pltpu.CompilerParams(dimension_semantics=sem, vmem_limit_bytes=limit)


def _rms(x):
    return x * lax.rsqrt(jnp.mean(x * x, axis=-1, keepdims=True) + EPS)


def _pack_bf16_pair(lo, hi):
    return pltpu.pack_elementwise([lo, hi], packed_dtype=BF16)


def _unpack_bf16_pair(words):
    return (pltpu.unpack_elementwise(words, index=0, packed_dtype=BF16, unpacked_dtype=F32),
            pltpu.unpack_elementwise(words, index=1, packed_dtype=BF16, unpacked_dtype=F32))


def _rows_to_words(x):
    return [_pack_bf16_pair(x[:, 2 * c * LANES:(2 * c + 1) * LANES], x[:, (2 * c + 1) * LANES:(2 * c + 2) * LANES])
            for c in range(ROW_WORDS)]


def _words_to_rows(words):
    parts = []
    for w in words:
        parts.extend(_unpack_bf16_pair(w))
    return jnp.concatenate(parts, axis=1).astype(BF16)


def _ada_body(c_ref, w_ref, b_ref, o_ref):
    c = c_ref[...]
    ca = c / (1.0 + jnp.exp(-c))
    o_ref[...] = jnp.dot(ca, w_ref[...], preferred_element_type=F32,
                         precision=lax.Precision.HIGHEST) + b_ref[...]


def _ada(c, w, b):
    bsz, d = c.shape
    n = w.shape[1]
    bn = n // 4
    return pl.pallas_call(
        _ada_body,
        out_shape=jax.ShapeDtypeStruct((bsz, n), F32),
        grid=(4,),
        in_specs=[pl.BlockSpec((bsz, d), lambda i: (0, 0)),
                  pl.BlockSpec((d, bn), lambda i: (0, i)),
                  pl.BlockSpec((1, bn), lambda i: (0, i))],
        out_specs=pl.BlockSpec((bsz, bn), lambda i: (0, i)),
        compiler_params=_cparams(("arbitrary",)),
        name="ada",
    )(c, w, b)


def _inproj_body(x_ref, sh_ref, sc_ref, w_ref, cos_ref, sa_ref, sb_ref, o_ref):
    h = (_rms(x_ref[...]) * (1.0 + sc_ref[0]) + sh_ref[0]).astype(BF16)
    cos = cos_ref[...]
    sa = sa_ref[...]
    sb = sb_ref[...]
    n_blk = w_ref.shape[1] // PROJ_BLK
    for cb in range(n_blk):
        p = jnp.dot(h, w_ref[:, cb * PROJ_BLK:(cb + 1) * PROJ_BLK], preferred_element_type=F32)
        if cb < 2:
            parts = []
            for j in range(PROJ_BLK // LANES):
                pj = p[:, j * LANES:(j + 1) * LANES]
                parts.append(pj * cos + pltpu.roll(pj, LANES - ROT_DIMS // 2, axis=1) * sa
                             + pltpu.roll(pj, ROT_DIMS // 2, axis=1) * sb)
            p = jnp.concatenate(parts, axis=1)
        o_ref[:, cb * PROJ_BLK:(cb + 1) * PROJ_BLK] = p.astype(BF16)


def _inproj(x2, shift, scale, w_ext, cos_t, sa_t, sb_t, seq):
    t, d = x2.shape
    n = w_ext.shape[1]
    tm = TM_PROJ
    tpb = seq // tm
    return pl.pallas_call(
        _inproj_body,
        out_shape=jax.ShapeDtypeStruct((t, n), BF16),
        grid=(t // tm,),
        in_specs=[pl.BlockSpec((tm, d), lambda i: (i, 0)),
                  pl.BlockSpec((1, 1, d), lambda i: (i // tpb, 0, 0)),
                  pl.BlockSpec((1, 1, d), lambda i: (i // tpb, 0, 0)),
                  pl.BlockSpec((d, n), lambda i: (0, 0)),
                  pl.BlockSpec((tm, LANES), lambda i: (i % tpb, 0)),
                  pl.BlockSpec((tm, LANES), lambda i: (i % tpb, 0)),
                  pl.BlockSpec((tm, LANES), lambda i: (i % tpb, 0))],
        out_specs=pl.BlockSpec((tm, n), lambda i: (i, 0)),
        compiler_params=_cparams(("arbitrary",)),
        name="inproj",
    )(x2, shift, scale, w_ext, cos_t, sa_t, sb_t)


def _swa_body(sink_ref, q_ref, k_ref, v_ref, g_ref, o_ref, *, seq):
    step = pl.program_id(1)
    blk = WINDOW
    kw_len = 3 * blk
    gw = A_GROUP * HEAD_DIM
    srows = A_GROUP * blk
    row = lax.broadcasted_iota(I32, (srows, 1), 0)
    grp = lax.broadcasted_iota(I32, (1, gw), 1) // HEAD_DIM
    blocks = range(SWA_BLOCKS_PER_STEP)
    kws, vws, alloweds = [], [], []
    for bi in blocks:
        n = step * SWA_BLOCKS_PER_STEP + bi
        start = pl.multiple_of(jnp.clip(n * blk - blk, 0, seq - kw_len), blk)
        kws.append(k_ref[pl.ds(start, kw_len), :])
        vws.append(v_ref[pl.ds(start, kw_len), :])
        qpos = n * blk + row % blk
        kpos = start + lax.broadcasted_iota(I32, (1, kw_len), 1)
        alloweds.append(jnp.abs(qpos - kpos) <= WINDOW)
    sks = []
    for h in range(A_KV_HEADS):
        sk = jnp.full((srows, 1), sink_ref[h * A_GROUP], F32)
        for g in range(1, A_GROUP):
            sk = jnp.where(row // blk == g, sink_ref[h * A_GROUP + g], sk)
        sks.append(sk * LOG2E)
    for b0 in range(0, SWA_BLOCKS_PER_STEP, SWA_STAGED_BLOCKS):
        group = range(b0, b0 + SWA_STAGED_BLOCKS)
        units = [(bi, h) for bi in group for h in range(A_KV_HEADS)]
        qms = []
        for bi, h in units:
            qh = q_ref[bi * blk:(bi + 1) * blk, h * gw:(h + 1) * gw]
            qms.append(jnp.concatenate(
                [jnp.where(grp == g, qh, jnp.zeros_like(qh)) for g in range(A_GROUP)], axis=0))
        ss = [lax.dot_general(qms[u], kws[bi][:, h * gw:(h + 1) * gw], (((1,), (1,)), ((), ())),
                              preferred_element_type=F32) for u, (bi, h) in enumerate(units)]
        ss = [jnp.where(alloweds[bi], ss[u], NEG) for u, (bi, h) in enumerate(units)]
        ms = [jnp.maximum(jnp.max(ss[u], axis=-1, keepdims=True), sks[h]) for u, (bi, h) in enumerate(units)]
        ps = [jnp.exp2(ss[u] - ms[u]) for u in range(len(units))]
        invs = [1.0 / (jnp.sum(ps[u], axis=-1, keepdims=True) + jnp.exp2(sks[h] - ms[u]))
                for u, (bi, h) in enumerate(units)]
        ress = [jnp.dot(ps[u].astype(BF16), vws[bi][:, h * gw:(h + 1) * gw], preferred_element_type=F32) * invs[u]
                for u, (bi, h) in enumerate(units)]
        accs = []
        for res in ress:
            acc = res[0:blk]
            for g in range(1, A_GROUP):
                acc = jnp.where(grp == g, res[g * blk:(g + 1) * blk], acc)
            accs.append(acc)
        for gi, bi in enumerate(group):
            oa = jnp.concatenate(accs[gi * A_KV_HEADS:(gi + 1) * A_KV_HEADS], axis=1)
            o_ref[bi * blk:(bi + 1) * blk, :] = (_rms(oa) * g_ref[...]).astype(BF16)


def _swa(proj, sink, g_a, bsz, seq):
    t = proj.shape[0]
    blk = WINDOW * SWA_BLOCKS_PER_STEP
    nq = seq // blk
    grid_spec = pltpu.PrefetchScalarGridSpec(
        num_scalar_prefetch=1, grid=(bsz, nq),
        in_specs=[pl.BlockSpec((blk, PROJ_BLK), lambda b, n, s: (b * nq + n, 0)),
                  pl.BlockSpec((seq, PROJ_BLK), lambda b, n, s: (b, 1)),
                  pl.BlockSpec((seq, PROJ_BLK), lambda b, n, s: (b, 2)),
                  pl.BlockSpec((1, A_Q_W), lambda b, n, s: (0, 0))],
        out_specs=pl.BlockSpec((blk, A_Q_W), lambda b, n, s: (b * nq + n, 0)))
    return pl.pallas_call(
        functools.partial(_swa_body, seq=seq),
        out_shape=jax.ShapeDtypeStruct((t, A_Q_W), BF16),
        grid_spec=grid_spec,
        compiler_params=_cparams(("arbitrary", "arbitrary")),
        name="swa",
    )(sink, proj, proj, proj, g_a)


def _na_body(q_ref, k_ref, v_ref, bias_ref, g_ref, o_ref, *, rows, kr, rb):
    blk = pl.program_id(1)
    per = LANES // HEAD_DIM
    n_pair = B_W // LANES
    half = lax.broadcasted_iota(I32, (1, LANES), 1) // HEAD_DIM
    gain = g_ref[...]

    def one_row(i, carry):
        r = blk * rb + i
        rs = jnp.clip(r - kr // 2, 0, rows - kr)
        start = pl.multiple_of(rs * GRID_W, GRID_W)
        kw = k_ref[pl.ds(start, kr * GRID_W), :]
        vw = v_ref[pl.ds(start, kr * GRID_W), :]
        q = q_ref[pl.ds(pl.multiple_of(i * GRID_W, GRID_W), GRID_W), :]
        s_parts = []
        for pr in range(n_pair):
            qp = q[:, pr * LANES:(pr + 1) * LANES]
            qm = jnp.concatenate([jnp.where(half == j, qp, jnp.zeros_like(qp)) for j in range(per)], axis=0)
            s_parts.append(lax.dot_general(qm, kw[:, pr * LANES:(pr + 1) * LANES],
                                           (((1,), (1,)), ((), ())), preferred_element_type=F32))
        s = jnp.concatenate(s_parts, axis=0) + bias_ref[r - rs]
        m = jnp.max(s, axis=-1, keepdims=True)
        p = jnp.exp2(s - m)
        inv = 1.0 / jnp.sum(p, axis=-1, keepdims=True)
        pn = p.astype(BF16)
        o_parts = []
        for pr in range(n_pair):
            srows = slice(pr * per * GRID_W, (pr + 1) * per * GRID_W)
            res = jnp.dot(pn[srows], vw[:, pr * LANES:(pr + 1) * LANES], preferred_element_type=F32) * inv[srows]
            op = res[0:GRID_W]
            for j in range(1, per):
                op = jnp.where(half == j, res[j * GRID_W:(j + 1) * GRID_W], op)
            o_parts.append(op)
        ob = jnp.concatenate(o_parts, axis=1)
        o_ref[pl.ds(pl.multiple_of(i * GRID_W, GRID_W), GRID_W), :] = (_rms(ob) * gain).astype(BF16)
        return carry

    def row_group(ig, carry):
        for u in range(NA_ROW_UNROLL):
            one_row(NA_ROW_UNROLL * ig + u, carry)
        return carry

    lax.fori_loop(0, rb // NA_ROW_UNROLL, row_group, 0)


def _natten(proj, bias_t, g_b, bsz, seq):
    t = proj.shape[0]
    rows = seq // GRID_W
    kr = min(NA_ROWS_MAX, rows)
    rb = NA_ROWS_PER_STEP
    nb = rows // rb
    return pl.pallas_call(
        functools.partial(_na_body, rows=rows, kr=kr, rb=rb),
        out_shape=jax.ShapeDtypeStruct((t, B_W), BF16),
        grid=(bsz, nb),
        in_specs=[pl.BlockSpec((rb * GRID_W, PROJ_BLK), lambda b, r: (b * nb + r, 3)),
                  pl.BlockSpec((seq, PROJ_BLK), lambda b, r: (b, 4)),
                  pl.BlockSpec((seq, PROJ_BLK), lambda b, r: (b, 5)),
                  pl.BlockSpec((kr, B_HEADS * GRID_W, kr * GRID_W), lambda b, r: (0, 0, 0)),
                  pl.BlockSpec((1, B_W), lambda b, r: (0, 0))],
        out_specs=pl.BlockSpec((rb * GRID_W, B_W), lambda b, r: (b * nb + r, 0)),
        compiler_params=_cparams(("arbitrary", "arbitrary")),
        name="natten",
    )(proj, proj, proj, bias_t, g_b)


def _na_bias_table(rpb, rows):
    kr = min(NA_ROWS_MAX, rows)
    kc = NA_COLS
    col = jnp.arange(GRID_W)
    col_start = jnp.clip(col - kc // 2, 0, GRID_W - kc)
    col_mask = (col[None, :] >= col_start[:, None]) & (col[None, :] < col_start[:, None] + kc)
    col_off = jnp.clip(col[None, :] - col[:, None], -(kc - 1), kc - 1) + (NA_COLS - 1)
    rows_sel = jnp.stack([rpb[:, NA_ROWS_MAX - 1 - cs:NA_ROWS_MAX - 1 - cs + kr, :]
                          for cs in range(kr)], axis=0)
    pick = (col_off[None, :, :] == jnp.arange(2 * kc - 1)[:, None, None]).astype(F32)
    tbl = jnp.einsum('chjr,rqk->chqjk', rows_sel, pick, precision=lax.Precision.HIGHEST)
    tbl = jnp.where(col_mask[None, None, :, None, :], tbl * LOG2E, NEG)
    return tbl.reshape(kr, rpb.shape[0] * GRID_W, kr * GRID_W).astype(F32)


def _outproj_body(x_ref, ma_ref, mb_ref, wo_ref, gm_ref, sf_ref, cf_ref, wr_ref, br_ref,
                  x1_ref, hs_ref, pg_ref, tc_ref, to_ref):
    tm = SORT_TM
    tiles = range(SORT_TILES_PER_STEP)
    tt = tm * SORT_TILES_PER_STEP
    n_pos = TOP_K * tm
    mix = (jnp.dot(ma_ref[...], wo_ref[:A_Q_W, :], preferred_element_type=F32)
           + jnp.dot(mb_ref[...], wo_ref[A_Q_W:, :], preferred_element_type=F32))
    x1 = x_ref[...] + gm_ref[0] * mix
    x1_ref[...] = x1
    h = _rms(x1) * (1.0 + cf_ref[0]) + sf_ref[0]
    h_hi = h.astype(BF16)
    h_lo = (h - h_hi.astype(F32)).astype(BF16)
    w_r = wr_ref[...]
    w_hi = w_r.astype(BF16)
    w_lo = (w_r - w_hi.astype(F32)).astype(BF16)
    l_hi = jnp.dot(h_hi, jnp.concatenate([w_hi, w_lo], axis=1), preferred_element_type=F32)
    logits = (l_hi[:, :LANES] + l_hi[:, LANES:]
              + jnp.dot(h_lo, w_hi, preferred_element_type=F32) + br_ref[...])
    lane = lax.broadcasted_iota(I32, (tt, LANES), 1)
    work = logits
    vals, idxs = [], []
    for _ in range(TOP_K):
        m = jnp.max(work, axis=-1, keepdims=True)
        ik = jnp.min(jnp.where(work == m, lane, LANES), axis=-1, keepdims=True)
        vals.append(m)
        idxs.append(ik)
        work = jnp.where(lane == ik, -jnp.inf, work)
    es = [jnp.exp(v - vals[0]) for v in vals]
    den = es[0] + es[1] + es[2] + es[3]
    mh = jnp.zeros((tt, LANES), F32)
    for ik in idxs:
        mh = mh + (lane == ik).astype(F32)
    mhb = mh.astype(BF16)
    tri = (lax.broadcasted_iota(I32, (tm, tm), 0) > lax.broadcasted_iota(I32, (tm, tm), 1)).astype(BF16)
    lower = (lax.broadcasted_iota(I32, (LANES, LANES), 0)
             < lax.broadcasted_iota(I32, (LANES, LANES), 1)).astype(BF16)
    mh_t = [mhb[ti * tm:(ti + 1) * tm] for ti in tiles]
    earlier = [jnp.dot(tri, mh_t[ti], preferred_element_type=F32) for ti in tiles]
    below = [jnp.dot(mh_t[ti], lower, preferred_element_type=F32) for ti in tiles]
    toff = [jnp.sum(below[ti], axis=0, keepdims=True) for ti in tiles]
    cnt = [jnp.sum(mh[ti * tm:(ti + 1) * tm], axis=0, keepdims=True) for ti in tiles]
    posf = jnp.concatenate([earlier[ti] + toff[ti] for ti in tiles], axis=0)
    pg = jnp.zeros((tt, LANES), F32)
    for k in range(TOP_K):
        pos_k = jnp.sum(jnp.where(lane == idxs[k], posf, 0.0), axis=-1, keepdims=True)
        pg = jnp.where(lane == k, pos_k, pg)
        pg = jnp.where(lane == TOP_K + k, es[k] / den, pg)
    pg_ref[...] = pg
    pgt_all = pg.T[:ROW_WORDS]
    pgt = [pgt_all[:, ti * tm:(ti + 1) * tm] for ti in tiles]
    for ti in tiles:
        tc_ref[ti] = cnt[ti]
        to_ref[ti] = toff[ti]
    pos_iota = lax.broadcasted_iota(I32, (n_pos, tm), 0)
    hits = []
    for ti in tiles:
        hit = pos_iota == pgt[ti][0:1, :].astype(I32)
        for k in range(1, TOP_K):
            hit = hit | (pos_iota == pgt[ti][k:k + 1, :].astype(I32))
        hits.append(jnp.where(hit, 1.0, 0.0).astype(BF16))
    srt = [jnp.dot(hits[ti], h_hi[ti * tm:(ti + 1) * tm], preferred_element_type=F32) for ti in tiles]
    for ti in tiles:
        for c, words in enumerate(_rows_to_words(srt[ti])):
            hs_ref[pl.ds(ti * n_pos * ROW_WORDS + c, n_pos, stride=ROW_WORDS), :] = words


def _outproj(x2, mix_a, mix_b, w_out, gate_m, shift_f, scale_f, w_r, b_r, seq):
    t, d = x2.shape
    tm = SORT_TM
    nt = t // tm
    ts = SORT_TILES_PER_STEP
    tpb = seq // (ts * tm)
    n_pos = TOP_K * tm
    mod_spec = pl.BlockSpec((1, 1, d), lambda i: (i // tpb, 0, 0))
    return pl.pallas_call(
        _outproj_body,
        out_shape=(jax.ShapeDtypeStruct((t, d), F32),
                   jax.ShapeDtypeStruct((t * TOP_K * ROW_WORDS, LANES), I32),
                   jax.ShapeDtypeStruct((t, LANES), F32),
                   jax.ShapeDtypeStruct((nt, 1, LANES), F32),
                   jax.ShapeDtypeStruct((nt, 1, LANES), F32)),
        grid=(nt // ts,),
        in_specs=[pl.BlockSpec((ts * tm, d), lambda i: (i, 0)),
                  pl.BlockSpec((ts * tm, A_Q_W), lambda i: (i, 0)),
                  pl.BlockSpec((ts * tm, B_W), lambda i: (i, 0)),
                  pl.BlockSpec((A_Q_W + B_W, d), lambda i: (0, 0)),
                  mod_spec, mod_spec, mod_spec,
                  pl.BlockSpec((d, LANES), lambda i: (0, 0)),
                  pl.BlockSpec((1, LANES), lambda i: (0, 0))],
        out_specs=(pl.BlockSpec((ts * tm, d), lambda i: (i, 0)),
                   pl.BlockSpec((ts * n_pos * ROW_WORDS, LANES), lambda i: (i, 0)),
                   pl.BlockSpec((ts * tm, LANES), lambda i: (i, 0)),
                   pl.BlockSpec((ts, 1, LANES), lambda i: (i, 0, 0)),
                   pl.BlockSpec((ts, 1, LANES), lambda i: (i, 0, 0))),
        compiler_params=_cparams(("arbitrary",)),
        name="outproj",
    )(x2, mix_a, mix_b, w_out, gate_m, shift_f, scale_f, w_r, b_r)


def _segment_copies(seg_ref, n_seg, first, last, src_hbm, dst_buf, sem, tbl=0):
    def body(i, carry):
        size = pl.multiple_of(seg_ref[tbl, 0, 2 * n_seg + i], ROW_WORDS)

        @pl.when(size > 0)
        def _():
            src = pl.multiple_of(seg_ref[tbl, 0, i], ROW_WORDS)
            dst = pl.multiple_of(seg_ref[tbl, 0, n_seg + i], ROW_WORDS)
            pltpu.make_async_copy(src_hbm.at[pl.ds(src, size), :], dst_buf.at[pl.ds(dst, size), :], sem).start()
        return carry

    lax.fori_loop(first, last, body, 0)


def _experts_body(be_ref, vl_ref, seg0_ref, segn_ref, hs_hbm, wgu32_ref, bgu_ref, wd32_ref, bd_ref, o_ref,
                  xbuf, xs_ref, wgu_ref, wd_ref, gsem, *, n_blk, n_seg):
    bm = MOE_BM
    rt = ROW_WORDS
    d_ff = wd_ref.shape[0]
    j = pl.program_id(0)
    slot = j % 2
    other = 1 - slot
    valid = vl_ref[j]

    def issue(seg_ref, s):
        _segment_copies(seg_ref, n_seg, seg_ref[0, 0, 3 * n_seg], seg_ref[0, 0, 3 * n_seg + 1],
                        hs_hbm, xbuf.at[s], gsem.at[s])

    @pl.when(j == 0)
    def _():
        xbuf[...] = jnp.zeros_like(xbuf)
        issue(seg0_ref, 0)

    @pl.when((j == 0) | (be_ref[j] != be_ref[jnp.maximum(j - 1, 0)]))
    def _():
        for c in range(d_ff // LANES):
            wgu_ref[:, 2 * c * LANES:(2 * c + 1) * LANES] = (
                wgu32_ref[0, :, c * LANES:(c + 1) * LANES].astype(BF16))
            wgu_ref[:, (2 * c + 1) * LANES:(2 * c + 2) * LANES] = (
                wgu32_ref[0, :, d_ff + c * LANES:d_ff + (c + 1) * LANES].astype(BF16))
        wd_ref[...] = wd32_ref[0].astype(BF16)

    @pl.when(valid > 0)
    def _():
        rows = pl.multiple_of(valid * rt, rt)
        pltpu.make_async_copy(hs_hbm.at[pl.ds(0, rows), :], xbuf.at[slot, pl.ds(0, rows), :],
                              gsem.at[slot]).wait()

    @pl.when(j + 1 < n_blk)
    def _():
        issue(segn_ref, other)

    sub = MOE_SUB
    n_sub = (valid + sub - 1) // sub

    def zero_sub(sb, carry):
        r0 = pl.multiple_of(sb * sub * rt, sub * rt)
        o_ref[pl.ds(r0, sub * rt), :] = jnp.zeros((sub * rt, LANES), I32)
        return carry

    lax.fori_loop(n_sub, bm // sub, zero_sub, 0)

    def compute_sub(sb, carry):
        r0 = pl.multiple_of(sb * sub * rt, sub * rt)
        xs_ref[...] = _words_to_rows([xbuf[slot, pl.ds(r0 + c, sub, stride=rt), :] for c in range(rt)])
        xs = xs_ref[...]
        acts = []
        for c in range(d_ff // LANES):
            gu = jnp.dot(xs, wgu_ref[:, 2 * c * LANES:(2 * c + 2) * LANES], preferred_element_type=F32)
            gate = gu[:, :LANES] + bgu_ref[0, :, c * LANES:(c + 1) * LANES]
            up = gu[:, LANES:] + bgu_ref[0, :, d_ff + c * LANES:d_ff + (c + 1) * LANES]
            gate = jnp.minimum(gate, SWIGLU_LIMIT)
            up = jnp.clip(up, -SWIGLU_LIMIT, SWIGLU_LIMIT)
            glu = gate / (1.0 + jnp.exp(-SWIGLU_ALPHA * gate))
            acts.append(((up + 1.0) * glu).astype(BF16))
        act = jnp.concatenate(acts, axis=1)
        ys = jnp.dot(act, wd_ref[...], preferred_element_type=F32) + bd_ref[0]
        for c, words in enumerate(_rows_to_words(ys)):
            o_ref[pl.ds(r0 + c, sub, stride=rt), :] = words
        return carry

    lax.fori_loop(0, n_sub, compute_sub, 0)


def _experts(blk_expert, blk_valid, seg, hs, wgu, bgu, wd, bd, n_seg):
    bm = MOE_BM
    rt = ROW_WORDS
    n_blk, _, seg_w = seg.shape
    d = wd.shape[2]
    d_ff = wd.shape[1]
    assert d == 2 * rt * LANES and hs.shape[1] == LANES
    smem = functools.partial(pl.BlockSpec, (1, 1, seg_w), memory_space=pltpu.SMEM)
    last = n_blk - 1
    grid_spec = pltpu.PrefetchScalarGridSpec(
        num_scalar_prefetch=2, grid=(n_blk,),
        in_specs=[smem(lambda j, be, vl: (j, 0, 0)),
                  smem(lambda j, be, vl: (jnp.minimum(j + 1, last), 0, 0)),
                  pl.BlockSpec(memory_space=pl.ANY),
                  pl.BlockSpec((1, d, 2 * d_ff), lambda j, be, vl: (be[j], 0, 0)),
                  pl.BlockSpec((1, 1, 2 * d_ff), lambda j, be, vl: (be[j], 0, 0)),
                  pl.BlockSpec((1, d_ff, d), lambda j, be, vl: (be[j], 0, 0)),
                  pl.BlockSpec((1, 1, d), lambda j, be, vl: (be[j], 0, 0))],
        out_specs=pl.BlockSpec((bm * rt, LANES), lambda j, be, vl: (j, 0)),
        scratch_shapes=[pltpu.VMEM((2, bm * rt, LANES), I32), pltpu.VMEM((MOE_SUB, d), BF16),
                        pltpu.VMEM((d, 2 * d_ff), BF16), pltpu.VMEM((d_ff, d), BF16),
                        pltpu.SemaphoreType.DMA((2,))])
    return pl.pallas_call(
        functools.partial(_experts_body, n_blk=n_blk, n_seg=n_seg),
        out_shape=jax.ShapeDtypeStruct((n_blk * bm * rt, LANES), I32),
        grid_spec=grid_spec,
        compiler_params=_cparams(("arbitrary",), VMEM_LIMIT_EXPERTS),
        name="experts",
    )(blk_expert, blk_valid, seg, seg, hs, wgu, bgu, wd, bd)


def _combine_body(seg0_ref, segn_ref, x1_ref, pg_ref, gf_ref, gfin_ref, ys_hbm, o_ref,
                  ybuf, sem, *, n_steps):
    tm = SORT_TM
    tiles = range(SORT_TILES_PER_STEP)
    n_pos = TOP_K * tm
    rt = ROW_WORDS
    i = pl.program_id(0)
    slot = i % 2
    other = 1 - slot

    def issue(seg_ref, s):
        for ti in tiles:
            _segment_copies(seg_ref, N_EXPERTS, 0, N_EXPERTS, ys_hbm, ybuf.at[s, ti], sem.at[s], tbl=ti)

    @pl.when(i == 0)
    def _():
        issue(seg0_ref, 0)

    for ti in tiles:
        pltpu.make_async_copy(ys_hbm.at[pl.ds(0, n_pos * rt), :], ybuf.at[slot, ti], sem.at[slot]).wait()

    @pl.when(i + 1 < n_steps)
    def _():
        issue(segn_ref, other)

    lane_pos = lax.broadcasted_iota(I32, (tm, n_pos), 1)
    ws = []
    for ti in tiles:
        pg = pg_ref[ti * tm:(ti + 1) * tm, :]
        w = jnp.zeros((tm, n_pos), F32)
        for k in range(TOP_K):
            w = jnp.where(lane_pos == pg[:, k:k + 1].astype(I32), pg[:, TOP_K + k:TOP_K + k + 1], w)
        ws.append(w)
    w_his = [w.astype(BF16) for w in ws]
    w_los = [(ws[ti] - w_his[ti].astype(F32)).astype(BF16) for ti in tiles]
    ysbs = [_words_to_rows([ybuf[slot, ti, pl.ds(c, n_pos, stride=rt), :] for c in range(rt)])
            for ti in tiles]
    ys = [jnp.dot(w_his[ti], ysbs[ti], preferred_element_type=F32)
          + jnp.dot(w_los[ti], ysbs[ti], preferred_element_type=F32) for ti in tiles]
    for ti in tiles:
        x2 = x1_ref[ti * tm:(ti + 1) * tm, :] + gf_ref[0] * ys[ti]
        o_ref[ti * tm:(ti + 1) * tm, :] = _rms(x2) * gfin_ref[...]


def _combine(seg, x1, pg, gate_f, g_final, ys, seq):
    t, d = x1.shape
    tm = SORT_TM
    ts = SORT_TILES_PER_STEP
    tpb = seq // (ts * tm)
    n_steps = t // (ts * tm)
    n_pos = TOP_K * tm
    seg_w = seg.shape[2]
    smem = functools.partial(pl.BlockSpec, (ts, 1, seg_w), memory_space=pltpu.SMEM)
    return pl.pallas_call(
        functools.partial(_combine_body, n_steps=n_steps),
        out_shape=jax.ShapeDtypeStruct((t, d), F32),
        grid=(n_steps,),
        in_specs=[smem(lambda i: (i, 0, 0)),
                  smem(lambda i: (jnp.minimum(i + 1, n_steps - 1), 0, 0)),
                  pl.BlockSpec((ts * tm, d), lambda i: (i, 0)),
                  pl.BlockSpec((ts * tm, LANES), lambda i: (i, 0)),
                  pl.BlockSpec((1, 1, d), lambda i: (i // tpb, 0, 0)),
                  pl.BlockSpec((1, d), lambda i: (0, 0)),
                  pl.BlockSpec(memory_space=pl.ANY)],
        out_specs=pl.BlockSpec((ts * tm, d), lambda i: (i, 0)),
        scratch_shapes=[pltpu.VMEM((2, ts, n_pos * ROW_WORDS, LANES), I32), pltpu.SemaphoreType.DMA((2,))],
        compiler_params=_cparams(("arbitrary",)),
        name="combine",
    )(seg, seg, x1, pg, gate_f, g_final, ys)


def _rope_tables(seq):
    half = ROT_DIMS // 2
    pos = jnp.arange(seq, dtype=F32)
    inv_freq = ROPE_THETA ** (-jnp.arange(0, ROT_DIMS, 2, dtype=F32) / ROT_DIMS)
    ang = pos[:, None] * inv_freq[None, :]
    cos, sin = jnp.cos(ang), jnp.sin(ang)
    ones = jnp.ones((seq, HEAD_DIM - ROT_DIMS), F32)
    zeros = jnp.zeros((seq, HEAD_DIM - ROT_DIMS), F32)
    zh = jnp.zeros((seq, half), F32)
    rep = LANES // HEAD_DIM
    cos_t = jnp.tile(jnp.concatenate([cos, cos, ones], axis=1), (1, rep))
    sa_t = jnp.tile(jnp.concatenate([-sin, zh, zeros], axis=1), (1, rep))
    sb_t = jnp.tile(jnp.concatenate([zh, sin, zeros], axis=1), (1, rep))
    return cos_t, sa_t, sb_t


def _widen_in_proj(w_in):
    d = w_in.shape[0]
    scale = HEAD_DIM ** -0.5 * LOG2E
    o = 0
    wqa = w_in[:, o:o + A_Q_W] * scale; o += A_Q_W
    wka = w_in[:, o:o + A_KV_W]; o += A_KV_W
    wva = w_in[:, o:o + A_KV_W]; o += A_KV_W
    wqb = w_in[:, o:o + B_W] * scale; o += B_W
    wkb = w_in[:, o:o + B_W]; o += B_W
    wvb = w_in[:, o:o + B_W]

    def widen(w):
        w = w.reshape(d, A_KV_HEADS, 1, HEAD_DIM)
        return jnp.broadcast_to(w, (d, A_KV_HEADS, A_GROUP, HEAD_DIM)).reshape(d, A_Q_W)

    return jnp.concatenate([wqa, widen(wka), widen(wva), wqb, wkb, wvb], axis=1).astype(BF16)


def _round_up(n, m):
    return (n + m - 1) // m * m


def _moe_tables(tile_cnt, tile_off, n_blk):
    bm = MOE_BM
    nt = tile_cnt.shape[0]
    n_pos = TOP_K * SORT_TM
    cum = jnp.cumsum(tile_cnt, axis=0) - tile_cnt
    counts = jnp.sum(tile_cnt, axis=0)
    padded = ((counts + bm - 1) // bm) * bm
    pends = jnp.cumsum(padded)
    pstarts = pends - padded
    blk_first = jnp.arange(n_blk, dtype=I32) * bm
    blk_expert = jnp.minimum(
        jnp.sum((pends[None, :] <= blk_first[:, None]).astype(I32), axis=1), N_EXPERTS - 1)
    r0 = blk_first - pstarts[blk_expert]
    blk_valid = jnp.clip(counts[blk_expert] - r0, 0, bm)
    cum_b = cum.T[blk_expert]
    cnt_b = tile_cnt.T[blk_expert]
    off_b = tile_off.T[blk_expert]
    lo = jnp.maximum(r0[:, None], cum_b)
    hi = jnp.minimum(r0[:, None] + bm, cum_b + cnt_b)
    size = jnp.maximum(hi - lo, 0)
    tile_ids = jnp.arange(nt, dtype=I32)[None, :]
    src = tile_ids * n_pos + off_b + (lo - cum_b)
    dst = jnp.where(size > 0, lo - r0[:, None], 0)
    first = jnp.min(jnp.where(size > 0, tile_ids, nt), axis=1)
    last = jnp.max(jnp.where(size > 0, tile_ids + 1, 0), axis=1)
    first = jnp.minimum(first, last)
    seg_w = _round_up(3 * nt + 2, LANES)
    exp_seg = jnp.concatenate(
        [src * ROW_WORDS, dst * ROW_WORDS, size * ROW_WORDS, first[:, None], last[:, None],
         jnp.zeros((n_blk, seg_w - 3 * nt - 2), I32)], axis=1).reshape(n_blk, 1, seg_w)
    comb_seg = jnp.concatenate(
        [(pstarts[None, :] + cum) * ROW_WORDS, tile_off * ROW_WORDS, tile_cnt * ROW_WORDS,
         jnp.zeros((nt, LANES - 3 * N_EXPERTS), I32)], axis=1).reshape(nt, 1, LANES)
    return blk_expert, blk_valid, exp_seg, comb_seg


def kernel(x, c, w_ada, b_ada, w_in, sink, rpb, g_out_a, g_out_b, w_out, w_router, b_router,
           w_gate_up, b_gate_up, w_down, b_down, g_final):
    bsz, seq, d = x.shape
    t = bsz * seq
    depth = w_ada.shape[0]
    assert depth == 1, "the final norm is fused into the layer combine"
    bm = MOE_BM
    n_blk = (t * TOP_K) // bm + N_EXPERTS
    nt = t // SORT_TM

    cos_t, sa_t, sb_t = _rope_tables(seq)
    x2 = x.reshape(t, d)
    for l in range(depth):
        mod = _ada(c, w_ada[l], b_ada[l][None, :])
        shift_m, scale_m, gate_m, shift_f, scale_f, gate_f = [
            m.reshape(bsz, 1, d) for m in jnp.split(mod, 6, axis=-1)]

        proj = _inproj(x2, shift_m, scale_m, _widen_in_proj(w_in[l]), cos_t, sa_t, sb_t, seq)
        mix_a = _swa(proj, sink[l], g_out_a[l][None, :], bsz, seq)
        mix_b = _natten(proj, _na_bias_table(rpb[l], seq // GRID_W), g_out_b[l][None, :], bsz, seq)

        w_r = jnp.pad(w_router[l], ((0, 0), (0, LANES - N_EXPERTS)))
        b_r = jnp.pad(b_router[l], (0, LANES - N_EXPERTS), constant_values=NEG)[None, :]
        x1, hs, pg, tile_cnt, tile_off = _outproj(
            x2, mix_a, mix_b, w_out[l].astype(BF16), gate_m, shift_f, scale_f, w_r, b_r, seq)

        blk_expert, blk_valid, exp_seg, comb_seg = _moe_tables(
            tile_cnt[:, 0, :N_EXPERTS].astype(I32), tile_off[:, 0, :N_EXPERTS].astype(I32), n_blk)
        ys = _experts(blk_expert, blk_valid, exp_seg, hs,
                      w_gate_up[l], b_gate_up[l][:, None, :],
                      w_down[l], b_down[l][:, None, :], nt)
        x2 = _combine(comb_seg, x1, pg, gate_f, g_final[None, :], ys, seq)
    return x2.reshape(bsz, seq, d)
```

```python
import functools

import jax
import jax.numpy as jnp
from jax import lax
from jax.experimental import pallas as pl
from jax.experimental.pallas import tpu as pltpu

F32 = jnp.float32
BF16 = jnp.bfloat16
I32 = jnp.int32

HEAD_DIM = 64
A_Q_HEADS = 8
A_KV_HEADS = 2
A_GROUP = A_Q_HEADS // A_KV_HEADS
B_HEADS = 8
A_Q_W = A_Q_HEADS * HEAD_DIM
A_KV_W = A_KV_HEADS * HEAD_DIM
B_W = B_HEADS * HEAD_DIM
WINDOW = 128
ROT_DIMS = HEAD_DIM // 4
ROPE_THETA = 500000.0
GRID_W = 64
NA_ROWS_MAX = 8
NA_COLS = 16
N_EXPERTS = 32
TOP_K = 4
SWIGLU_LIMIT = 7.0
SWIGLU_ALPHA = 1.702
EPS = 1e-5
NEG = -1e30
LOG2E = 1.4426950408889634

LANES = 128
ROW_WORDS = 4
PROJ_BLK = 512
TM_PROJ = 512
SORT_TM = 256
SORT_TILES_PER_STEP = 2
MOE_BM = 1024
MOE_SUB = 512
SWA_BLOCKS_PER_STEP = 2
SWA_STAGED_BLOCKS = 1
NA_ROWS_PER_STEP = 8
NA_ROW_UNROLL = 4
VMEM_LIMIT = 48 << 20
VMEM_LIMIT_EXPERTS = 58 << 20


def _cparams(sem, limit=VMEM_LIMIT):
    return pltpu.CompilerParams(dimension_semantics=sem, vmem_limit_bytes=limit)


def _rms(x):
    return x * lax.rsqrt(jnp.mean(x * x, axis=-1, keepdims=True) + EPS)


def _pack_bf16_pair(lo, hi):
    return pltpu.pack_elementwise([lo, hi], packed_dtype=BF16)


def _unpack_bf16_pair(words):
    return (pltpu.unpack_elementwise(words, index=0, packed_dtype=BF16, unpacked_dtype=F32),
            pltpu.unpack_elementwise(words, index=1, packed_dtype=BF16, unpacked_dtype=F32))


def _rows_to_words(x):
    return [_pack_bf16_pair(x[:, 2 * c * LANES:(2 * c + 1) * LANES], x[:, (2 * c + 1) * LANES:(2 * c + 2) * LANES])
            for c in range(ROW_WORDS)]


def _words_to_rows(words):
    parts = []
    for w in words:
        parts.extend(_unpack_bf16_pair(w))
    return jnp.concatenate(parts, axis=1).astype(BF16)


def _ada_body(c_ref, w_ref, b_ref, o_ref):
    c = c_ref[...]
    ca = c / (1.0 + jnp.exp(-c))
    o_ref[...] = jnp.dot(ca, w_ref[...], preferred_element_type=F32,
                         precision=lax.Precision.HIGHEST) + b_ref[...]


def _ada(c, w, b):
    bsz, d = c.shape
    n = w.shape[1]
    bn = n // 4
    return pl.pallas_call(
        _ada_body,
        out_shape=jax.ShapeDtypeStruct((bsz, n), F32),
        grid=(4,),
        in_specs=[pl.BlockSpec((bsz, d), lambda i: (0, 0)),
                  pl.BlockSpec((d, bn), lambda i: (0, i)),
                  pl.BlockSpec((1, bn), lambda i: (0, i))],
        out_specs=pl.BlockSpec((bsz, bn), lambda i: (0, i)),
        compiler_params=_cparams(("arbitrary",)),
        name="ada",
    )(c, w, b)


def _inproj_body(x_ref, sh_ref, sc_ref, w_ref, cos_ref, sa_ref, sb_ref, o_ref):
    h = (_rms(x_ref[...]) * (1.0 + sc_ref[0]) + sh_ref[0]).astype(BF16)
    cos = cos_ref[...]
    sa = sa_ref[...]
    sb = sb_ref[...]
    n_blk = w_ref.shape[1] // PROJ_BLK
    for cb in range(n_blk):
        p = jnp.dot(h, w_ref[:, cb * PROJ_BLK:(cb + 1) * PROJ_BLK], preferred_element_type=F32)
        if cb < 2:
            parts = []
            for j in range(PROJ_BLK // LANES):
                pj = p[:, j * LANES:(j + 1) * LANES]
                parts.append(pj * cos + pltpu.roll(pj, LANES - ROT_DIMS // 2, axis=1) * sa
                             + pltpu.roll(pj, ROT_DIMS // 2, axis=1) * sb)
            p = jnp.concatenate(parts, axis=1)
        o_ref[:, cb * PROJ_BLK:(cb + 1) * PROJ_BLK] = p.astype(BF16)


def _inproj(x2, shift, scale, w_ext, cos_t, sa_t, sb_t, seq):
    t, d = x2.shape
    n = w_ext.shape[1]
    tm = TM_PROJ
    tpb = seq // tm
    return pl.pallas_call(
        _inproj_body,
        out_shape=jax.ShapeDtypeStruct((t, n), BF16),
        grid=(t // tm,),
        in_specs=[pl.BlockSpec((tm, d), lambda i: (i, 0)),
                  pl.BlockSpec((1, 1, d), lambda i: (i // tpb, 0, 0)),
                  pl.BlockSpec((1, 1, d), lambda i: (i // tpb, 0, 0)),
                  pl.BlockSpec((d, n), lambda i: (0, 0)),
                  pl.BlockSpec((tm, LANES), lambda i: (i % tpb, 0)),
                  pl.BlockSpec((tm, LANES), lambda i: (i % tpb, 0)),
                  pl.BlockSpec((tm, LANES), lambda i: (i % tpb, 0))],
        out_specs=pl.BlockSpec((tm, n), lambda i: (i, 0)),
        compiler_params=_cparams(("arbitrary",)),
        name="inproj",
    )(x2, shift, scale, w_ext, cos_t, sa_t, sb_t)


def _swa_body(sink_ref, q_ref, k_ref, v_ref, g_ref, o_ref, *, seq):
    step = pl.program_id(1)
    blk = WINDOW
    kw_len = 3 * blk
    gw = A_GROUP * HEAD_DIM
    srows = A_GROUP * blk
    row = lax.broadcasted_iota(I32, (srows, 1), 0)
    grp = lax.broadcasted_iota(I32, (1, gw), 1) // HEAD_DIM
    blocks = range(SWA_BLOCKS_PER_STEP)
    kws, vws, alloweds = [], [], []
    for bi in blocks:
        n = step * SWA_BLOCKS_PER_STEP + bi
        start = pl.multiple_of(jnp.clip(n * blk - blk, 0, seq - kw_len), blk)
        kws.append(k_ref[pl.ds(start, kw_len), :])
        vws.append(v_ref[pl.ds(start, kw_len), :])
        qpos = n * blk + row % blk
        kpos = start + lax.broadcasted_iota(I32, (1, kw_len), 1)
        alloweds.append(jnp.abs(qpos - kpos) <= WINDOW)
    sks = []
    for h in range(A_KV_HEADS):
        sk = jnp.full((srows, 1), sink_ref[h * A_GROUP], F32)
        for g in range(1, A_GROUP):
            sk = jnp.where(row // blk == g, sink_ref[h * A_GROUP + g], sk)
        sks.append(sk * LOG2E)
    for b0 in range(0, SWA_BLOCKS_PER_STEP, SWA_STAGED_BLOCKS):
        group = range(b0, b0 + SWA_STAGED_BLOCKS)
        units = [(bi, h) for bi in group for h in range(A_KV_HEADS)]
        qms = []
        for bi, h in units:
            qh = q_ref[bi * blk:(bi + 1) * blk, h * gw:(h + 1) * gw]
            qms.append(jnp.concatenate(
                [jnp.where(grp == g, qh, jnp.zeros_like(qh)) for g in range(A_GROUP)], axis=0))
        ss = [lax.dot_general(qms[u], kws[bi][:, h * gw:(h + 1) * gw], (((1,), (1,)), ((), ())),
                              preferred_element_type=F32) for u, (bi, h) in enumerate(units)]
        ss = [jnp.where(alloweds[bi], ss[u], NEG) for u, (bi, h) in enumerate(units)]
        ms = [jnp.maximum(jnp.max(ss[u], axis=-1, keepdims=True), sks[h]) for u, (bi, h) in enumerate(units)]
        ps = [jnp.exp2(ss[u] - ms[u]) for u in range(len(units))]
        invs = [1.0 / (jnp.sum(ps[u], axis=-1, keepdims=True) + jnp.exp2(sks[h] - ms[u]))
                for u, (bi, h) in enumerate(units)]
        ress = [jnp.dot(ps[u].astype(BF16), vws[bi][:, h * gw:(h + 1) * gw], preferred_element_type=F32) * invs[u]
                for u, (bi, h) in enumerate(units)]
        accs = []
        for res in ress:
            acc = res[0:blk]
            for g in range(1, A_GROUP):
                acc = jnp.where(grp == g, res[g * blk:(g + 1) * blk], acc)
            accs.append(acc)
        for gi, bi in enumerate(group):
            oa = jnp.concatenate(accs[gi * A_KV_HEADS:(gi + 1) * A_KV_HEADS], axis=1)
            o_ref[bi * blk:(bi + 1) * blk, :] = (_rms(oa) * g_ref[...]).astype(BF16)


def _swa(proj, sink, g_a, bsz, seq):
    t = proj.shape[0]
    blk = WINDOW * SWA_BLOCKS_PER_STEP
    nq = seq // blk
    grid_spec = pltpu.PrefetchScalarGridSpec(
        num_scalar_prefetch=1, grid=(bsz, nq),
        in_specs=[pl.BlockSpec((blk, PROJ_BLK), lambda b, n, s: (b * nq + n, 0)),
                  pl.BlockSpec((seq, PROJ_BLK), lambda b, n, s: (b, 1)),
                  pl.BlockSpec((seq, PROJ_BLK), lambda b, n, s: (b, 2)),
                  pl.BlockSpec((1, A_Q_W), lambda b, n, s: (0, 0))],
        out_specs=pl.BlockSpec((blk, A_Q_W), lambda b, n, s: (b * nq + n, 0)))
    return pl.pallas_call(
        functools.partial(_swa_body, seq=seq),
        out_shape=jax.ShapeDtypeStruct((t, A_Q_W), BF16),
        grid_spec=grid_spec,
        compiler_params=_cparams(("arbitrary", "arbitrary")),
        name="swa",
    )(sink, proj, proj, proj, g_a)


def _na_body(q_ref, k_ref, v_ref, bias_ref, g_ref, o_ref, *, rows, kr, rb):
    blk = pl.program_id(1)
    per = LANES // HEAD_DIM
    n_pair = B_W // LANES
    half = lax.broadcasted_iota(I32, (1, LANES), 1) // HEAD_DIM
    gain = g_ref[...]

    def one_row(i, carry):
        r = blk * rb + i
        rs = jnp.clip(r - kr // 2, 0, rows - kr)
        start = pl.multiple_of(rs * GRID_W, GRID_W)
        kw = k_ref[pl.ds(start, kr * GRID_W), :]
        vw = v_ref[pl.ds(start, kr * GRID_W), :]
        q = q_ref[pl.ds(pl.multiple_of(i * GRID_W, GRID_W), GRID_W), :]
        s_parts = []
        for pr in range(n_pair):
            qp = q[:, pr * LANES:(pr + 1) * LANES]
            qm = jnp.concatenate([jnp.where(half == j, qp, jnp.zeros_like(qp)) for j in range(per)], axis=0)
            s_parts.append(lax.dot_general(qm, kw[:, pr * LANES:(pr + 1) * LANES],
                                           (((1,), (1,)), ((), ())), preferred_element_type=F32))
        s = jnp.concatenate(s_parts, axis=0) + bias_ref[r - rs]
        m = jnp.max(s, axis=-1, keepdims=True)
        p = jnp.exp2(s - m)
        inv = 1.0 / jnp.sum(p, axis=-1, keepdims=True)
        pn = p.astype(BF16)
        o_parts = []
        for pr in range(n_pair):
            srows = slice(pr * per * GRID_W, (pr + 1) * per * GRID_W)
            res = jnp.dot(pn[srows], vw[:, pr * LANES:(pr + 1) * LANES], preferred_element_type=F32) * inv[srows]
            op = res[0:GRID_W]
            for j in range(1, per):
                op = jnp.where(half == j, res[j * GRID_W:(j + 1) * GRID_W], op)
            o_parts.append(op)
        ob = jnp.concatenate(o_parts, axis=1)
        o_ref[pl.ds(pl.multiple_of(i * GRID_W, GRID_W), GRID_W), :] = (_rms(ob) * gain).astype(BF16)
        return carry

    def row_group(ig, carry):
        for u in range(NA_ROW_UNROLL):
            one_row(NA_ROW_UNROLL * ig + u, carry)
        return carry

    lax.fori_loop(0, rb // NA_ROW_UNROLL, row_group, 0)


def _natten(proj, bias_t, g_b, bsz, seq):
    t = proj.shape[0]
    rows = seq // GRID_W
    kr = min(NA_ROWS_MAX, rows)
    rb = NA_ROWS_PER_STEP
    nb = rows // rb
    return pl.pallas_call(
        functools.partial(_na_body, rows=rows, kr=kr, rb=rb),
        out_shape=jax.ShapeDtypeStruct((t, B_W), BF16),
        grid=(bsz, nb),
        in_specs=[pl.BlockSpec((rb * GRID_W, PROJ_BLK), lambda b, r: (b * nb + r, 3)),
                  pl.BlockSpec((seq, PROJ_BLK), lambda b, r: (b, 4)),
                  pl.BlockSpec((seq, PROJ_BLK), lambda b, r: (b, 5)),
                  pl.BlockSpec((kr, B_HEADS * GRID_W, kr * GRID_W), lambda b, r: (0, 0, 0)),
                  pl.BlockSpec((1, B_W), lambda b, r: (0, 0))],
        out_specs=pl.BlockSpec((rb * GRID_W, B_W), lambda b, r: (b * nb + r, 0)),
        compiler_params=_cparams(("arbitrary", "arbitrary")),
        name="natten",
    )(proj, proj, proj, bias_t, g_b)


def _na_bias_table(rpb, rows):
    kr = min(NA_ROWS_MAX, rows)
    kc = NA_COLS
    col = jnp.arange(GRID_W)
    col_start = jnp.clip(col - kc // 2, 0, GRID_W - kc)
    col_mask = (col[None, :] >= col_start[:, None]) & (col[None, :] < col_start[:, None] + kc)
    col_off = jnp.clip(col[None, :] - col[:, None], -(kc - 1), kc - 1) + (NA_COLS - 1)
    rows_sel = jnp.stack([rpb[:, NA_ROWS_MAX - 1 - cs:NA_ROWS_MAX - 1 - cs + kr, :]
                          for cs in range(kr)], axis=0)
    pick = (col_off[None, :, :] == jnp.arange(2 * kc - 1)[:, None, None]).astype(F32)
    tbl = jnp.einsum('chjr,rqk->chqjk', rows_sel, pick, precision=lax.Precision.HIGHEST)
    tbl = jnp.where(col_mask[None, None, :, None, :], tbl * LOG2E, NEG)
    return tbl.reshape(kr, rpb.shape[0] * GRID_W, kr * GRID_W).astype(F32)


def _outproj_body(x_ref, ma_ref, mb_ref, wo_ref, gm_ref, sf_ref, cf_ref, wr_ref, br_ref,
                  x1_ref, hs_ref, pg_ref, tc_ref, to_ref):
    tm = SORT_TM
    tiles = range(SORT_TILES_PER_STEP)
    tt = tm * SORT_TILES_PER_STEP
    n_pos = TOP_K * tm
    mix = (jnp.dot(ma_ref[...], wo_ref[:A_Q_W, :], preferred_element_type=F32)
           + jnp.dot(mb_ref[...], wo_ref[A_Q_W:, :], preferred_element_type=F32))
    x1 = x_ref[...] + gm_ref[0] * mix
    x1_ref[...] = x1
    h = _rms(x1) * (1.0 + cf_ref[0]) + sf_ref[0]
    h_hi = h.astype(BF16)
    h_lo = (h - h_hi.astype(F32)).astype(BF16)
    w_r = wr_ref[...]
    w_hi = w_r.astype(BF16)
    w_lo = (w_r - w_hi.astype(F32)).astype(BF16)
    l_hi = jnp.dot(h_hi, jnp.concatenate([w_hi, w_lo], axis=1), preferred_element_type=F32)
    logits = (l_hi[:, :LANES] + l_hi[:, LANES:]
              + jnp.dot(h_lo, w_hi, preferred_element_type=F32) + br_ref[...])
    lane = lax.broadcasted_iota(I32, (tt, LANES), 1)
    work = logits
    vals, idxs = [], []
    for _ in range(TOP_K):
        m = jnp.max(work, axis=-1, keepdims=True)
        ik = jnp.min(jnp.where(work == m, lane, LANES), axis=-1, keepdims=True)
        vals.append(m)
        idxs.append(ik)
        work = jnp.where(lane == ik, -jnp.inf, work)
    es = [jnp.exp(v - vals[0]) for v in vals]
    den = es[0] + es[1] + es[2] + es[3]
    mh = jnp.zeros((tt, LANES), F32)
    for ik in idxs:
        mh = mh + (lane == ik).astype(F32)
    mhb = mh.astype(BF16)
    tri = (lax.broadcasted_iota(I32, (tm, tm), 0) > lax.broadcasted_iota(I32, (tm, tm), 1)).astype(BF16)
    lower = (lax.broadcasted_iota(I32, (LANES, LANES), 0)
             < lax.broadcasted_iota(I32, (LANES, LANES), 1)).astype(BF16)
    mh_t = [mhb[ti * tm:(ti + 1) * tm] for ti in tiles]
    earlier = [jnp.dot(tri, mh_t[ti], preferred_element_type=F32) for ti in tiles]
    below = [jnp.dot(mh_t[ti], lower, preferred_element_type=F32) for ti in tiles]
    toff = [jnp.sum(below[ti], axis=0, keepdims=True) for ti in tiles]
    cnt = [jnp.sum(mh[ti * tm:(ti + 1) * tm], axis=0, keepdims=True) for ti in tiles]
    posf = jnp.concatenate([earlier[ti] + toff[ti] for ti in tiles], axis=0)
    pg = jnp.zeros((tt, LANES), F32)
    for k in range(TOP_K):
        pos_k = jnp.sum(jnp.where(lane == idxs[k], posf, 0.0), axis=-1, keepdims=True)
        pg = jnp.where(lane == k, pos_k, pg)
        pg = jnp.where(lane == TOP_K + k, es[k] / den, pg)
    pg_ref[...] = pg
    pgt_all = pg.T[:ROW_WORDS]
    pgt = [pgt_all[:, ti * tm:(ti + 1) * tm] for ti in tiles]
    for ti in tiles:
        tc_ref[ti] = cnt[ti]
        to_ref[ti] = toff[ti]
    pos_iota = lax.broadcasted_iota(I32, (n_pos, tm), 0)
    hits = []
    for ti in tiles:
        hit = pos_iota == pgt[ti][0:1, :].astype(I32)
        for k in range(1, TOP_K):
            hit = hit | (pos_iota == pgt[ti][k:k + 1, :].astype(I32))
        hits.append(jnp.where(hit, 1.0, 0.0).astype(BF16))
    srt = [jnp.dot(hits[ti], h_hi[ti * tm:(ti + 1) * tm], preferred_element_type=F32) for ti in tiles]
    for ti in tiles:
        for c, words in enumerate(_rows_to_words(srt[ti])):
            hs_ref[pl.ds(ti * n_pos * ROW_WORDS + c, n_pos, stride=ROW_WORDS), :] = words


def _outproj(x2, mix_a, mix_b, w_out, gate_m, shift_f, scale_f, w_r, b_r, seq):
    t, d = x2.shape
    tm = SORT_TM
    nt = t // tm
    ts = SORT_TILES_PER_STEP
    tpb = seq // (ts * tm)
    n_pos = TOP_K * tm
    mod_spec = pl.BlockSpec((1, 1, d), lambda i: (i // tpb, 0, 0))
    return pl.pallas_call(
        _outproj_body,
        out_shape=(jax.ShapeDtypeStruct((t, d), F32),
                   jax.ShapeDtypeStruct((t * TOP_K * ROW_WORDS, LANES), I32),
                   jax.ShapeDtypeStruct((t, LANES), F32),
                   jax.ShapeDtypeStruct((nt, 1, LANES), F32),
                   jax.ShapeDtypeStruct((nt, 1, LANES), F32)),
        grid=(nt // ts,),
        in_specs=[pl.BlockSpec((ts * tm, d), lambda i: (i, 0)),
                  pl.BlockSpec((ts * tm, A_Q_W), lambda i: (i, 0)),
                  pl.BlockSpec((ts * tm, B_W), lambda i: (i, 0)),
                  pl.BlockSpec((A_Q_W + B_W, d), lambda i: (0, 0)),
                  mod_spec, mod_spec, mod_spec,
                  pl.BlockSpec((d, LANES), lambda i: (0, 0)),
                  pl.BlockSpec((1, LANES), lambda i: (0, 0))],
        out_specs=(pl.BlockSpec((ts * tm, d), lambda i: (i, 0)),
                   pl.BlockSpec((ts * n_pos * ROW_WORDS, LANES), lambda i: (i, 0)),
                   pl.BlockSpec((ts * tm, LANES), lambda i: (i, 0)),
                   pl.BlockSpec((ts, 1, LANES), lambda i: (i, 0, 0)),
                   pl.BlockSpec((ts, 1, LANES), lambda i: (i, 0, 0))),
        compiler_params=_cparams(("arbitrary",)),
        name="outproj",
    )(x2, mix_a, mix_b, w_out, gate_m, shift_f, scale_f, w_r, b_r)


def _segment_copies(seg_ref, n_seg, first, last, src_hbm, dst_buf, sem, tbl=0):
    def body(i, carry):
        size = pl.multiple_of(seg_ref[tbl, 0, 2 * n_seg + i], ROW_WORDS)

        @pl.when(size > 0)
        def _():
            src = pl.multiple_of(seg_ref[tbl, 0, i], ROW_WORDS)
            dst = pl.multiple_of(seg_ref[tbl, 0, n_seg + i], ROW_WORDS)
            pltpu.make_async_copy(src_hbm.at[pl.ds(src, size), :], dst_buf.at[pl.ds(dst, size), :], sem).start()
        return carry

    lax.fori_loop(first, last, body, 0)


def _experts_body(be_ref, vl_ref, seg0_ref, segn_ref, hs_hbm, wgu32_ref, bgu_ref, wd32_ref, bd_ref, o_ref,
                  xbuf, xs_ref, wgu_ref, wd_ref, gsem, *, n_blk, n_seg):
    bm = MOE_BM
    rt = ROW_WORDS
    d_ff = wd_ref.shape[0]
    j = pl.program_id(0)
    slot = j % 2
    other = 1 - slot
    valid = vl_ref[j]

    def issue(seg_ref, s):
        _segment_copies(seg_ref, n_seg, seg_ref[0, 0, 3 * n_seg], seg_ref[0, 0, 3 * n_seg + 1],
                        hs_hbm, xbuf.at[s], gsem.at[s])

    @pl.when(j == 0)
    def _():
        xbuf[...] = jnp.zeros_like(xbuf)
        issue(seg0_ref, 0)

    @pl.when((j == 0) | (be_ref[j] != be_ref[jnp.maximum(j - 1, 0)]))
    def _():
        for c in range(d_ff // LANES):
            wgu_ref[:, 2 * c * LANES:(2 * c + 1) * LANES] = (
                wgu32_ref[0, :, c * LANES:(c + 1) * LANES].astype(BF16))
            wgu_ref[:, (2 * c + 1) * LANES:(2 * c + 2) * LANES] = (
                wgu32_ref[0, :, d_ff + c * LANES:d_ff + (c + 1) * LANES].astype(BF16))
        wd_ref[...] = wd32_ref[0].astype(BF16)

    @pl.when(valid > 0)
    def _():
        rows = pl.multiple_of(valid * rt, rt)
        pltpu.make_async_copy(hs_hbm.at[pl.ds(0, rows), :], xbuf.at[slot, pl.ds(0, rows), :],
                              gsem.at[slot]).wait()

    @pl.when(j + 1 < n_blk)
    def _():
        issue(segn_ref, other)

    sub = MOE_SUB
    n_chunk = d_ff // LANES
    second = sub * rt
    n_sub = (valid + sub - 1) // sub

    def assemble(r0):
        return _words_to_rows([xbuf[slot, pl.ds(r0 + c, sub, stride=rt), :] for c in range(rt)])

    def act_chunk(xs, c):
        gu = jnp.dot(xs, wgu_ref[:, 2 * c * LANES:(2 * c + 2) * LANES], preferred_element_type=F32)
        gate = gu[:, :LANES] + bgu_ref[0, :, c * LANES:(c + 1) * LANES]
        up = gu[:, LANES:] + bgu_ref[0, :, d_ff + c * LANES:d_ff + (c + 1) * LANES]
        gate = jnp.minimum(gate, SWIGLU_LIMIT)
        up = jnp.clip(up, -SWIGLU_LIMIT, SWIGLU_LIMIT)
        glu = gate / (1.0 + jnp.exp(-SWIGLU_ALPHA * gate))
        return ((up + 1.0) * glu).astype(BF16)

    def down(acts):
        return jnp.dot(jnp.concatenate(acts, axis=1), wd_ref[...], preferred_element_type=F32) + bd_ref[0]

    def store_rows(r0, ys):
        for c, words in enumerate(_rows_to_words(ys)):
            o_ref[pl.ds(r0 + c, sub, stride=rt), :] = words

    def zero_rows(r0):
        o_ref[pl.ds(r0, sub * rt), :] = jnp.zeros((sub * rt, LANES), I32)

    @pl.when(n_sub == 0)
    def _():
        zero_rows(0)
        zero_rows(second)

    @pl.when(n_sub == 1)
    def _():
        xs_ref[0] = assemble(0)
        xs = xs_ref[0]
        store_rows(0, down([act_chunk(xs, c) for c in range(n_chunk)]))
        zero_rows(second)

    @pl.when(n_sub == 2)
    def _():
        xs_ref[0] = assemble(0)
        xa = xs_ref[0]
        acts = []
        for c in range(n_chunk):
            acts.append(act_chunk(xa, c))
            if c == n_chunk // 2:
                xs_ref[1] = assemble(second)
        ys_a = down(acts)
        xb = xs_ref[1]
        acts = []
        for c in range(n_chunk):
            acts.append(act_chunk(xb, c))
            if c == 1:
                store_rows(0, ys_a)
        store_rows(second, down(acts))


def _experts(blk_expert, blk_valid, seg, hs, wgu, bgu, wd, bd, n_seg):
    bm = MOE_BM
    rt = ROW_WORDS
    n_blk, _, seg_w = seg.shape
    d = wd.shape[2]
    d_ff = wd.shape[1]
    assert d == 2 * rt * LANES and hs.shape[1] == LANES and bm == 2 * MOE_SUB
    smem = functools.partial(pl.BlockSpec, (1, 1, seg_w), memory_space=pltpu.SMEM)
    last = n_blk - 1
    grid_spec = pltpu.PrefetchScalarGridSpec(
        num_scalar_prefetch=2, grid=(n_blk,),
        in_specs=[smem(lambda j, be, vl: (j, 0, 0)),
                  smem(lambda j, be, vl: (jnp.minimum(j + 1, last), 0, 0)),
                  pl.BlockSpec(memory_space=pl.ANY),
                  pl.BlockSpec((1, d, 2 * d_ff), lambda j, be, vl: (be[j], 0, 0)),
                  pl.BlockSpec((1, 1, 2 * d_ff), lambda j, be, vl: (be[j], 0, 0)),
                  pl.BlockSpec((1, d_ff, d), lambda j, be, vl: (be[j], 0, 0)),
                  pl.BlockSpec((1, 1, d), lambda j, be, vl: (be[j], 0, 0))],
        out_specs=pl.BlockSpec((bm * rt, LANES), lambda j, be, vl: (j, 0)),
        scratch_shapes=[pltpu.VMEM((2, bm * rt, LANES), I32), pltpu.VMEM((2, MOE_SUB, d), BF16),
                        pltpu.VMEM((d, 2 * d_ff), BF16), pltpu.VMEM((d_ff, d), BF16),
                        pltpu.SemaphoreType.DMA((2,))])
    return pl.pallas_call(
        functools.partial(_experts_body, n_blk=n_blk, n_seg=n_seg),
        out_shape=jax.ShapeDtypeStruct((n_blk * bm * rt, LANES), I32),
        grid_spec=grid_spec,
        compiler_params=_cparams(("arbitrary",), VMEM_LIMIT_EXPERTS),
        name="experts",
    )(blk_expert, blk_valid, seg, seg, hs, wgu, bgu, wd, bd)


def _combine_body(seg0_ref, segn_ref, x1_ref, pg_ref, gf_ref, gfin_ref, ys_hbm, o_ref,
                  ybuf, sem, *, n_steps):
    tm = SORT_TM
    tiles = range(SORT_TILES_PER_STEP)
    n_pos = TOP_K * tm
    rt = ROW_WORDS
    i = pl.program_id(0)
    slot = i % 2
    other = 1 - slot

    def issue(seg_ref, s):
        for ti in tiles:
            _segment_copies(seg_ref, N_EXPERTS, 0, N_EXPERTS, ys_hbm, ybuf.at[s, ti], sem.at[s], tbl=ti)

    @pl.when(i == 0)
    def _():
        issue(seg0_ref, 0)

    for ti in tiles:
        pltpu.make_async_copy(ys_hbm.at[pl.ds(0, n_pos * rt), :], ybuf.at[slot, ti], sem.at[slot]).wait()

    @pl.when(i + 1 < n_steps)
    def _():
        issue(segn_ref, other)

    lane_pos = lax.broadcasted_iota(I32, (tm, n_pos), 1)
    ws = []
    for ti in tiles:
        pg = pg_ref[ti * tm:(ti + 1) * tm, :]
        w = jnp.zeros((tm, n_pos), F32)
        for k in range(TOP_K):
            w = jnp.where(lane_pos == pg[:, k:k + 1].astype(I32), pg[:, TOP_K + k:TOP_K + k + 1], w)
        ws.append(w)
    w_his = [w.astype(BF16) for w in ws]
    w_los = [(ws[ti] - w_his[ti].astype(F32)).astype(BF16) for ti in tiles]
    ysbs = [_words_to_rows([ybuf[slot, ti, pl.ds(c, n_pos, stride=rt), :] for c in range(rt)])
            for ti in tiles]
    ys = [jnp.dot(w_his[ti], ysbs[ti], preferred_element_type=F32)
          + jnp.dot(w_los[ti], ysbs[ti], preferred_element_type=F32) for ti in tiles]
    for ti in tiles:
        x2 = x1_ref[ti * tm:(ti + 1) * tm, :] + gf_ref[0] * ys[ti]
        o_ref[ti * tm:(ti + 1) * tm, :] = _rms(x2) * gfin_ref[...]


def _combine(seg, x1, pg, gate_f, g_final, ys, seq):
    t, d = x1.shape
    tm = SORT_TM
    ts = SORT_TILES_PER_STEP
    tpb = seq // (ts * tm)
    n_steps = t // (ts * tm)
    n_pos = TOP_K * tm
    seg_w = seg.shape[2]
    smem = functools.partial(pl.BlockSpec, (ts, 1, seg_w), memory_space=pltpu.SMEM)
    return pl.pallas_call(
        functools.partial(_combine_body, n_steps=n_steps),
        out_shape=jax.ShapeDtypeStruct((t, d), F32),
        grid=(n_steps,),
        in_specs=[smem(lambda i: (i, 0, 0)),
                  smem(lambda i: (jnp.minimum(i + 1, n_steps - 1), 0, 0)),
                  pl.BlockSpec((ts * tm, d), lambda i: (i, 0)),
                  pl.BlockSpec((ts * tm, LANES), lambda i: (i, 0)),
                  pl.BlockSpec((1, 1, d), lambda i: (i // tpb, 0, 0)),
                  pl.BlockSpec((1, d), lambda i: (0, 0)),
                  pl.BlockSpec(memory_space=pl.ANY)],
        out_specs=pl.BlockSpec((ts * tm, d), lambda i: (i, 0)),
        scratch_shapes=[pltpu.VMEM((2, ts, n_pos * ROW_WORDS, LANES), I32), pltpu.SemaphoreType.DMA((2,))],
        compiler_params=_cparams(("arbitrary",)),
        name="combine",
    )(seg, seg, x1, pg, gate_f, g_final, ys)


def _rope_tables(seq):
    half = ROT_DIMS // 2
    pos = jnp.arange(seq, dtype=F32)
    inv_freq = ROPE_THETA ** (-jnp.arange(0, ROT_DIMS, 2, dtype=F32) / ROT_DIMS)
    ang = pos[:, None] * inv_freq[None, :]
    cos, sin = jnp.cos(ang), jnp.sin(ang)
    ones = jnp.ones((seq, HEAD_DIM - ROT_DIMS), F32)
    zeros = jnp.zeros((seq, HEAD_DIM - ROT_DIMS), F32)
    zh = jnp.zeros((seq, half), F32)
    rep = LANES // HEAD_DIM
    cos_t = jnp.tile(jnp.concatenate([cos, cos, ones], axis=1), (1, rep))
    sa_t = jnp.tile(jnp.concatenate([-sin, zh, zeros], axis=1), (1, rep))
    sb_t = jnp.tile(jnp.concatenate([zh, sin, zeros], axis=1), (1, rep))
    return cos_t, sa_t, sb_t


def _widen_in_proj(w_in):
    d = w_in.shape[0]
    scale = HEAD_DIM ** -0.5 * LOG2E
    o = 0
    wqa = w_in[:, o:o + A_Q_W] * scale; o += A_Q_W
    wka = w_in[:, o:o + A_KV_W]; o += A_KV_W
    wva = w_in[:, o:o + A_KV_W]; o += A_KV_W
    wqb = w_in[:, o:o + B_W] * scale; o += B_W
    wkb = w_in[:, o:o + B_W]; o += B_W
    wvb = w_in[:, o:o + B_W]

    def widen(w):
        w = w.reshape(d, A_KV_HEADS, 1, HEAD_DIM)
        return jnp.broadcast_to(w, (d, A_KV_HEADS, A_GROUP, HEAD_DIM)).reshape(d, A_Q_W)

    return jnp.concatenate([wqa, widen(wka), widen(wva), wqb, wkb, wvb], axis=1).astype(BF16)


def _round_up(n, m):
    return (n + m - 1) // m * m


def _moe_tables(tile_cnt, tile_off, n_blk):
    bm = MOE_BM
    nt = tile_cnt.shape[0]
    n_pos = TOP_K * SORT_TM
    cum = jnp.cumsum(tile_cnt, axis=0) - tile_cnt
    counts = jnp.sum(tile_cnt, axis=0)
    padded = ((counts + bm - 1) // bm) * bm
    pends = jnp.cumsum(padded)
    pstarts = pends - padded
    blk_first = jnp.arange(n_blk, dtype=I32) * bm
    blk_expert = jnp.minimum(
        jnp.sum((pends[None, :] <= blk_first[:, None]).astype(I32), axis=1), N_EXPERTS - 1)
    own = blk_expert[:, None] == jnp.arange(N_EXPERTS, dtype=I32)[None, :]

    def of_block_expert(per_expert):
        return jnp.sum(jnp.where(own[:, None, :], per_expert[None], 0), axis=-1)

    r0 = blk_first - of_block_expert(pstarts[None, :])[:, 0]
    blk_valid = jnp.clip(of_block_expert(counts[None, :])[:, 0] - r0, 0, bm)
    cum_b = of_block_expert(cum)
    cnt_b = of_block_expert(tile_cnt)
    off_b = of_block_expert(tile_off)
    lo = jnp.maximum(r0[:, None], cum_b)
    hi = jnp.minimum(r0[:, None] + bm, cum_b + cnt_b)
    size = jnp.maximum(hi - lo, 0)
    tile_ids = jnp.arange(nt, dtype=I32)[None, :]
    src = tile_ids * n_pos + off_b + (lo - cum_b)
    dst = jnp.where(size > 0, lo - r0[:, None], 0)
    first = jnp.min(jnp.where(size > 0, tile_ids, nt), axis=1)
    last = jnp.max(jnp.where(size > 0, tile_ids + 1, 0), axis=1)
    first = jnp.minimum(first, last)
    seg_w = _round_up(3 * nt + 2, LANES)
    exp_seg = jnp.concatenate(
        [src * ROW_WORDS, dst * ROW_WORDS, size * ROW_WORDS, first[:, None], last[:, None],
         jnp.zeros((n_blk, seg_w - 3 * nt - 2), I32)], axis=1).reshape(n_blk, 1, seg_w)
    comb_seg = jnp.concatenate(
        [(pstarts[None, :] + cum) * ROW_WORDS, tile_off * ROW_WORDS, tile_cnt * ROW_WORDS,
         jnp.zeros((nt, LANES - 3 * N_EXPERTS), I32)], axis=1).reshape(nt, 1, LANES)
    return blk_expert, blk_valid, exp_seg, comb_seg


def kernel(x, c, w_ada, b_ada, w_in, sink, rpb, g_out_a, g_out_b, w_out, w_router, b_router,
           w_gate_up, b_gate_up, w_down, b_down, g_final):
    bsz, seq, d = x.shape
    t = bsz * seq
    depth = w_ada.shape[0]
    assert depth == 1, "the final norm is fused into the layer combine"
    bm = MOE_BM
    n_blk = (t * TOP_K) // bm + N_EXPERTS
    nt = t // SORT_TM

    cos_t, sa_t, sb_t = _rope_tables(seq)
    x2 = x.reshape(t, d)
    for l in range(depth):
        mod = _ada(c, w_ada[l], b_ada[l][None, :])
        shift_m, scale_m, gate_m, shift_f, scale_f, gate_f = [
            m.reshape(bsz, 1, d) for m in jnp.split(mod, 6, axis=-1)]

        proj = _inproj(x2, shift_m, scale_m, _widen_in_proj(w_in[l]), cos_t, sa_t, sb_t, seq)
        mix_a = _swa(proj, sink[l], g_out_a[l][None, :], bsz, seq)
        mix_b = _natten(proj, _na_bias_table(rpb[l], seq // GRID_W), g_out_b[l][None, :], bsz, seq)

        w_r = jnp.pad(w_router[l], ((0, 0), (0, LANES - N_EXPERTS)))
        b_r = jnp.pad(b_router[l], (0, LANES - N_EXPERTS), constant_values=NEG)[None, :]
        x1, hs, pg, tile_cnt, tile_off = _outproj(
            x2, mix_a, mix_b, w_out[l].astype(BF16), gate_m, shift_f, scale_f, w_r, b_r, seq)

        blk_expert, blk_valid, exp_seg, comb_seg = _moe_tables(
            tile_cnt[:, 0, :N_EXPERTS].astype(I32), tile_off[:, 0, :N_EXPERTS].astype(I32), n_blk)
        ys = _experts(blk_expert, blk_valid, exp_seg, hs,
                      w_gate_up[l], b_gate_up[l][:, None, :],
                      w_down[l], b_down[l][:, None, :], nt)
        x2 = _combine(comb_seg, x1, pg, gate_f, g_final[None, :], ys, seq)
    return x2.reshape(bsz, seq, d)
```

```python
import functools

import jax
import jax.numpy as jnp
from jax import lax
from jax.experimental import pallas as pl
from jax.experimental.pallas import tpu as pltpu

F32 = jnp.float32
BF16 = jnp.bfloat16
I32 = jnp.int32

HEAD_DIM = 64
A_Q_HEADS = 8
A_KV_HEADS = 2
A_GROUP = A_Q_HEADS // A_KV_HEADS
B_HEADS = 8
A_Q_W = A_Q_HEADS * HEAD_DIM
A_KV_W = A_KV_HEADS * HEAD_DIM
B_W = B_HEADS * HEAD_DIM
WINDOW = 128
ROT_DIMS = HEAD_DIM // 4
ROPE_THETA = 500000.0
GRID_W = 64
NA_ROWS_MAX = 8
NA_COLS = 16
N_EXPERTS = 32
TOP_K = 4
SWIGLU_LIMIT = 7.0
SWIGLU_ALPHA = 1.702
EPS = 1e-5
NEG = -1e30
LOG2E = 1.4426950408889634

LANES = 128
ROW_WORDS = 4
PROJ_BLK = 512
KV_A_COL_BLOCK = (A_Q_W + 3 * B_W) // (2 * A_KV_W)
TM_PROJ = 512
SORT_TM = 256
SORT_TILES_PER_STEP = 2
MOE_BM = 1024
MOE_SUB = 512
SWA_BLOCKS_PER_STEP = 2
SWA_STAGED_BLOCKS = 1
NA_ROWS_PER_STEP = 8
NA_ROW_UNROLL = 4
VMEM_LIMIT = 48 << 20
VMEM_LIMIT_EXPERTS = 58 << 20


def _cparams(sem, limit=VMEM_LIMIT):
    return pltpu.CompilerParams(dimension_semantics=sem, vmem_limit_bytes=limit)


def _rms(x):
    return x * lax.rsqrt(jnp.mean(x * x, axis=-1, keepdims=True) + EPS)


def _pack_bf16_pair(lo, hi):
    return pltpu.pack_elementwise([lo, hi], packed_dtype=BF16)


def _unpack_bf16_pair(words):
    return (pltpu.unpack_elementwise(words, index=0, packed_dtype=BF16, unpacked_dtype=F32),
            pltpu.unpack_elementwise(words, index=1, packed_dtype=BF16, unpacked_dtype=F32))


def _rows_to_words(x):
    return [_pack_bf16_pair(x[:, 2 * c * LANES:(2 * c + 1) * LANES], x[:, (2 * c + 1) * LANES:(2 * c + 2) * LANES])
            for c in range(ROW_WORDS)]


def _words_to_rows(words):
    parts = []
    for w in words:
        parts.extend(_unpack_bf16_pair(w))
    return jnp.concatenate(parts, axis=1).astype(BF16)


def _ada_body(c_ref, w_ref, b_ref, o_ref):
    c = c_ref[...]
    ca = c / (1.0 + jnp.exp(-c))
    o_ref[...] = jnp.dot(ca, w_ref[...], preferred_element_type=F32,
                         precision=lax.Precision.HIGHEST) + b_ref[...]


def _ada(c, w, b):
    bsz, d = c.shape
    n = w.shape[1]
    bn = n // 4
    return pl.pallas_call(
        _ada_body,
        out_shape=jax.ShapeDtypeStruct((bsz, n), F32),
        grid=(4,),
        in_specs=[pl.BlockSpec((bsz, d), lambda i: (0, 0)),
                  pl.BlockSpec((d, bn), lambda i: (0, i)),
                  pl.BlockSpec((1, bn), lambda i: (0, i))],
        out_specs=pl.BlockSpec((bsz, bn), lambda i: (0, i)),
        compiler_params=_cparams(("arbitrary",)),
        name="ada",
    )(c, w, b)


def _inproj_body(x_ref, sh_ref, sc_ref, w_ref, cos_ref, sa_ref, sb_ref, o_ref):
    h = (_rms(x_ref[...]) * (1.0 + sc_ref[0]) + sh_ref[0]).astype(BF16)
    cos = cos_ref[...]
    sa = sa_ref[...]
    sb = sb_ref[...]
    blocks = [(0, A_Q_W, A_Q_W // LANES)]
    blocks += [(A_Q_W + i * B_W, B_W, 0) for i in range(3)]
    blocks += [(A_Q_W + 3 * B_W, 2 * A_KV_W, A_KV_W // LANES)]
    for c0, width, n_rot in blocks:
        p = jnp.dot(h, w_ref[:, c0:c0 + width], preferred_element_type=F32)
        if n_rot:
            parts = []
            for j in range(width // LANES):
                pj = p[:, j * LANES:(j + 1) * LANES]
                if j < n_rot:
                    pj = (pj * cos + pltpu.roll(pj, LANES - ROT_DIMS // 2, axis=1) * sa
                          + pltpu.roll(pj, ROT_DIMS // 2, axis=1) * sb)
                parts.append(pj)
            p = jnp.concatenate(parts, axis=1)
        o_ref[:, c0:c0 + width] = p.astype(BF16)


def _inproj(x2, shift, scale, w_ext, cos_t, sa_t, sb_t, seq):
    t, d = x2.shape
    n = w_ext.shape[1]
    tm = TM_PROJ
    tpb = seq // tm
    return pl.pallas_call(
        _inproj_body,
        out_shape=jax.ShapeDtypeStruct((t, n), BF16),
        grid=(t // tm,),
        in_specs=[pl.BlockSpec((tm, d), lambda i: (i, 0)),
                  pl.BlockSpec((1, 1, d), lambda i: (i // tpb, 0, 0)),
                  pl.BlockSpec((1, 1, d), lambda i: (i // tpb, 0, 0)),
                  pl.BlockSpec((d, n), lambda i: (0, 0)),
                  pl.BlockSpec((tm, LANES), lambda i: (i % tpb, 0)),
                  pl.BlockSpec((tm, LANES), lambda i: (i % tpb, 0)),
                  pl.BlockSpec((tm, LANES), lambda i: (i % tpb, 0))],
        out_specs=pl.BlockSpec((tm, n), lambda i: (i, 0)),
        compiler_params=_cparams(("arbitrary",)),
        name="inproj",
    )(x2, shift, scale, w_ext, cos_t, sa_t, sb_t)


def _swa_body(sink_ref, q_ref, kv_ref, g_ref, o_ref, *, seq):
    step = pl.program_id(1)
    blk = WINDOW
    kw_len = 3 * blk
    gw = A_GROUP * HEAD_DIM
    srows = A_GROUP * blk
    row = lax.broadcasted_iota(I32, (srows, 1), 0)
    grp = lax.broadcasted_iota(I32, (1, gw), 1) // HEAD_DIM
    blocks = range(SWA_BLOCKS_PER_STEP)
    def spread(tile):
        words = pltpu.bitcast(tile, jnp.uint32)
        swapped = pltpu.roll(words, HEAD_DIM, axis=1)
        low = lax.broadcasted_iota(I32, words.shape, 1) < HEAD_DIM
        out = []
        for pair in (jnp.where(low, words, swapped), jnp.where(low, swapped, words)):
            pair = pltpu.bitcast(pair, BF16)
            out.append(jnp.concatenate([pair] * (gw // LANES), axis=1))
        return out

    kws, vws, alloweds = [], [], []
    for bi in blocks:
        n = step * SWA_BLOCKS_PER_STEP + bi
        start = pl.multiple_of(jnp.clip(n * blk - blk, 0, seq - kw_len), blk)
        kv = kv_ref[pl.ds(start, kw_len), :]
        kws.append(jnp.concatenate(spread(kv[:, :A_KV_W]), axis=1))
        vws.append(jnp.concatenate(spread(kv[:, A_KV_W:]), axis=1))
        qpos = n * blk + row % blk
        kpos = start + lax.broadcasted_iota(I32, (1, kw_len), 1)
        alloweds.append(jnp.abs(qpos - kpos) <= WINDOW)
    sks = []
    for h in range(A_KV_HEADS):
        sk = jnp.full((srows, 1), sink_ref[h * A_GROUP], F32)
        for g in range(1, A_GROUP):
            sk = jnp.where(row // blk == g, sink_ref[h * A_GROUP + g], sk)
        sks.append(sk * LOG2E)
    for b0 in range(0, SWA_BLOCKS_PER_STEP, SWA_STAGED_BLOCKS):
        group = range(b0, b0 + SWA_STAGED_BLOCKS)
        units = [(bi, h) for bi in group for h in range(A_KV_HEADS)]
        qms = []
        for bi, h in units:
            qh = q_ref[bi * blk:(bi + 1) * blk, h * gw:(h + 1) * gw]
            qms.append(jnp.concatenate(
                [jnp.where(grp == g, qh, jnp.zeros_like(qh)) for g in range(A_GROUP)], axis=0))
        ss = [lax.dot_general(qms[u], kws[bi][:, h * gw:(h + 1) * gw], (((1,), (1,)), ((), ())),
                              preferred_element_type=F32) for u, (bi, h) in enumerate(units)]
        ss = [jnp.where(alloweds[bi], ss[u], NEG) for u, (bi, h) in enumerate(units)]
        ms = [jnp.maximum(jnp.max(ss[u], axis=-1, keepdims=True), sks[h]) for u, (bi, h) in enumerate(units)]
        ps = [jnp.exp2(ss[u] - ms[u]) for u in range(len(units))]
        invs = [1.0 / (jnp.sum(ps[u], axis=-1, keepdims=True) + jnp.exp2(sks[h] - ms[u]))
                for u, (bi, h) in enumerate(units)]
        ress = [jnp.dot(ps[u].astype(BF16), vws[bi][:, h * gw:(h + 1) * gw], preferred_element_type=F32) * invs[u]
                for u, (bi, h) in enumerate(units)]
        accs = []
        for res in ress:
            acc = res[0:blk]
            for g in range(1, A_GROUP):
                acc = jnp.where(grp == g, res[g * blk:(g + 1) * blk], acc)
            accs.append(acc)
        for gi, bi in enumerate(group):
            oa = jnp.concatenate(accs[gi * A_KV_HEADS:(gi + 1) * A_KV_HEADS], axis=1)
            o_ref[bi * blk:(bi + 1) * blk, :] = (_rms(oa) * g_ref[...]).astype(BF16)


def _swa(proj, sink, g_a, bsz, seq):
    t = proj.shape[0]
    blk = WINDOW * SWA_BLOCKS_PER_STEP
    nq = seq // blk
    grid_spec = pltpu.PrefetchScalarGridSpec(
        num_scalar_prefetch=1, grid=(bsz, nq),
        in_specs=[pl.BlockSpec((blk, PROJ_BLK), lambda b, n, s: (b * nq + n, 0)),
                  pl.BlockSpec((seq, 2 * A_KV_W), lambda b, n, s: (b, KV_A_COL_BLOCK)),
                  pl.BlockSpec((1, A_Q_W), lambda b, n, s: (0, 0))],
        out_specs=pl.BlockSpec((blk, A_Q_W), lambda b, n, s: (b * nq + n, 0)))
    return pl.pallas_call(
        functools.partial(_swa_body, seq=seq),
        out_shape=jax.ShapeDtypeStruct((t, A_Q_W), BF16),
        grid_spec=grid_spec,
        compiler_params=_cparams(("arbitrary", "arbitrary")),
        name="swa",
    )(sink, proj, proj, g_a)


def _na_body(q_ref, k_ref, v_ref, bias_ref, g_ref, o_ref, *, rows, kr, rb):
    blk = pl.program_id(1)
    per = LANES // HEAD_DIM
    n_pair = B_W // LANES
    half = lax.broadcasted_iota(I32, (1, LANES), 1) // HEAD_DIM
    gain = g_ref[...]

    def one_row(i, carry):
        r = blk * rb + i
        rs = jnp.clip(r - kr // 2, 0, rows - kr)
        start = pl.multiple_of(rs * GRID_W, GRID_W)
        kw = k_ref[pl.ds(start, kr * GRID_W), :]
        vw = v_ref[pl.ds(start, kr * GRID_W), :]
        q = q_ref[pl.ds(pl.multiple_of(i * GRID_W, GRID_W), GRID_W), :]
        s_parts = []
        for pr in range(n_pair):
            qp = q[:, pr * LANES:(pr + 1) * LANES]
            qm = jnp.concatenate([jnp.where(half == j, qp, jnp.zeros_like(qp)) for j in range(per)], axis=0)
            s_parts.append(lax.dot_general(qm, kw[:, pr * LANES:(pr + 1) * LANES],
                                           (((1,), (1,)), ((), ())), preferred_element_type=F32))
        s = jnp.concatenate(s_parts, axis=0) + bias_ref[r - rs]
        m = jnp.max(s, axis=-1, keepdims=True)
        p = jnp.exp2(s - m)
        inv = 1.0 / jnp.sum(p, axis=-1, keepdims=True)
        pn = p.astype(BF16)
        o_parts = []
        for pr in range(n_pair):
            srows = slice(pr * per * GRID_W, (pr + 1) * per * GRID_W)
            res = jnp.dot(pn[srows], vw[:, pr * LANES:(pr + 1) * LANES], preferred_element_type=F32) * inv[srows]
            op = res[0:GRID_W]
            for j in range(1, per):
                op = jnp.where(half == j, res[j * GRID_W:(j + 1) * GRID_W], op)
            o_parts.append(op)
        ob = jnp.concatenate(o_parts, axis=1)
        o_ref[pl.ds(pl.multiple_of(i * GRID_W, GRID_W), GRID_W), :] = (_rms(ob) * gain).astype(BF16)
        return carry

    def row_group(ig, carry):
        for u in range(NA_ROW_UNROLL):
            one_row(NA_ROW_UNROLL * ig + u, carry)
        return carry

    lax.fori_loop(0, rb // NA_ROW_UNROLL, row_group, 0)


def _natten(proj, bias_t, g_b, bsz, seq):
    t = proj.shape[0]
    rows = seq // GRID_W
    kr = min(NA_ROWS_MAX, rows)
    rb = NA_ROWS_PER_STEP
    nb = rows // rb
    return pl.pallas_call(
        functools.partial(_na_body, rows=rows, kr=kr, rb=rb),
        out_shape=jax.ShapeDtypeStruct((t, B_W), BF16),
        grid=(bsz, nb),
        in_specs=[pl.BlockSpec((rb * GRID_W, PROJ_BLK), lambda b, r: (b * nb + r, 1)),
                  pl.BlockSpec((seq, PROJ_BLK), lambda b, r: (b, 2)),
                  pl.BlockSpec((seq, PROJ_BLK), lambda b, r: (b, 3)),
                  pl.BlockSpec((kr, B_HEADS * GRID_W, kr * GRID_W), lambda b, r: (0, 0, 0)),
                  pl.BlockSpec((1, B_W), lambda b, r: (0, 0))],
        out_specs=pl.BlockSpec((rb * GRID_W, B_W), lambda b, r: (b * nb + r, 0)),
        compiler_params=_cparams(("arbitrary", "arbitrary")),
        name="natten",
    )(proj, proj, proj, bias_t, g_b)


def _na_bias_table(rpb, rows):
    kr = min(NA_ROWS_MAX, rows)
    kc = NA_COLS
    col = jnp.arange(GRID_W)
    col_start = jnp.clip(col - kc // 2, 0, GRID_W - kc)
    col_mask = (col[None, :] >= col_start[:, None]) & (col[None, :] < col_start[:, None] + kc)
    col_off = jnp.clip(col[None, :] - col[:, None], -(kc - 1), kc - 1) + (NA_COLS - 1)
    rows_sel = jnp.stack([rpb[:, NA_ROWS_MAX - 1 - cs:NA_ROWS_MAX - 1 - cs + kr, :]
                          for cs in range(kr)], axis=0)
    pick = (col_off[None, :, :] == jnp.arange(2 * kc - 1)[:, None, None]).astype(F32)
    tbl = jnp.einsum('chjr,rqk->chqjk', rows_sel, pick, precision=lax.Precision.HIGHEST)
    tbl = jnp.where(col_mask[None, None, :, None, :], tbl * LOG2E, NEG)
    return tbl.reshape(kr, rpb.shape[0] * GRID_W, kr * GRID_W).astype(F32)


def _outproj_body(x_ref, ma_ref, mb_ref, wo_ref, gm_ref, sf_ref, cf_ref, wr_ref, br_ref,
                  x1_ref, hs_ref, pg_ref, tc_ref, to_ref):
    tm = SORT_TM
    tiles = range(SORT_TILES_PER_STEP)
    tt = tm * SORT_TILES_PER_STEP
    n_pos = TOP_K * tm
    mix = (jnp.dot(ma_ref[...], wo_ref[:A_Q_W, :], preferred_element_type=F32)
           + jnp.dot(mb_ref[...], wo_ref[A_Q_W:, :], preferred_element_type=F32))
    x1 = x_ref[...] + gm_ref[0] * mix
    x1_ref[...] = x1
    h = _rms(x1) * (1.0 + cf_ref[0]) + sf_ref[0]
    h_hi = h.astype(BF16)
    h_lo = (h - h_hi.astype(F32)).astype(BF16)
    w_r = wr_ref[...]
    w_hi = w_r.astype(BF16)
    w_lo = (w_r - w_hi.astype(F32)).astype(BF16)
    l_hi = jnp.dot(h_hi, jnp.concatenate([w_hi, w_lo], axis=1), preferred_element_type=F32)
    logits = (l_hi[:, :LANES] + l_hi[:, LANES:]
              + jnp.dot(h_lo, w_hi, preferred_element_type=F32) + br_ref[...])
    lane = lax.broadcasted_iota(I32, (tt, LANES), 1)
    work = logits
    vals, idxs = [], []
    for _ in range(TOP_K):
        m = jnp.max(work, axis=-1, keepdims=True)
        ik = jnp.min(jnp.where(work == m, lane, LANES), axis=-1, keepdims=True)
        vals.append(m)
        idxs.append(ik)
        work = jnp.where(lane == ik, -jnp.inf, work)
    es = [jnp.exp(v - vals[0]) for v in vals]
    den = es[0] + es[1] + es[2] + es[3]
    mh = jnp.zeros((tt, LANES), F32)
    for ik in idxs:
        mh = mh + (lane == ik).astype(F32)
    mhb = mh.astype(BF16)
    tri = (lax.broadcasted_iota(I32, (tm, tm), 0) > lax.broadcasted_iota(I32, (tm, tm), 1)).astype(BF16)
    lower = (lax.broadcasted_iota(I32, (LANES, LANES), 0)
             < lax.broadcasted_iota(I32, (LANES, LANES), 1)).astype(BF16)
    mh_t = [mhb[ti * tm:(ti + 1) * tm] for ti in tiles]
    earlier = [jnp.dot(tri, mh_t[ti], preferred_element_type=F32) for ti in tiles]
    below = [jnp.dot(mh_t[ti], lower, preferred_element_type=F32) for ti in tiles]
    toff = [jnp.sum(below[ti], axis=0, keepdims=True) for ti in tiles]
    cnt = [jnp.sum(mh[ti * tm:(ti + 1) * tm], axis=0, keepdims=True) for ti in tiles]
    posf = jnp.concatenate([earlier[ti] + toff[ti] for ti in tiles], axis=0)
    pg = jnp.zeros((tt, LANES), F32)
    for k in range(TOP_K):
        pos_k = jnp.sum(jnp.where(lane == idxs[k], posf, 0.0), axis=-1, keepdims=True)
        pg = jnp.where(lane == k, pos_k, pg)
        pg = jnp.where(lane == TOP_K + k, es[k] / den, pg)
    pg_ref[...] = pg
    pgt_all = pg.T[:ROW_WORDS]
    pgt = [pgt_all[:, ti * tm:(ti + 1) * tm] for ti in tiles]
    for ti in tiles:
        tc_ref[ti] = cnt[ti]
        to_ref[ti] = toff[ti]
    pos_iota = lax.broadcasted_iota(I32, (n_pos, tm), 0)
    hits = []
    for ti in tiles:
        hit = pos_iota == pgt[ti][0:1, :].astype(I32)
        for k in range(1, TOP_K):
            hit = hit | (pos_iota == pgt[ti][k:k + 1, :].astype(I32))
        hits.append(jnp.where(hit, 1.0, 0.0).astype(BF16))
    srt = [jnp.dot(hits[ti], h_hi[ti * tm:(ti + 1) * tm], preferred_element_type=F32) for ti in tiles]
    for ti in tiles:
        for c, words in enumerate(_rows_to_words(srt[ti])):
            hs_ref[pl.ds(ti * n_pos * ROW_WORDS + c, n_pos, stride=ROW_WORDS), :] = words


def _outproj(x2, mix_a, mix_b, w_out, gate_m, shift_f, scale_f, w_r, b_r, seq):
    t, d = x2.shape
    tm = SORT_TM
    nt = t // tm
    ts = SORT_TILES_PER_STEP
    tpb = seq // (ts * tm)
    n_pos = TOP_K * tm
    mod_spec = pl.BlockSpec((1, 1, d), lambda i: (i // tpb, 0, 0))
    return pl.pallas_call(
        _outproj_body,
        out_shape=(jax.ShapeDtypeStruct((t, d), F32),
                   jax.ShapeDtypeStruct((t * TOP_K * ROW_WORDS, LANES), I32),
                   jax.ShapeDtypeStruct((t, LANES), F32),
                   jax.ShapeDtypeStruct((nt, 1, LANES), F32),
                   jax.ShapeDtypeStruct((nt, 1, LANES), F32)),
        grid=(nt // ts,),
        in_specs=[pl.BlockSpec((ts * tm, d), lambda i: (i, 0)),
                  pl.BlockSpec((ts * tm, A_Q_W), lambda i: (i, 0)),
                  pl.BlockSpec((ts * tm, B_W), lambda i: (i, 0)),
                  pl.BlockSpec((A_Q_W + B_W, d), lambda i: (0, 0)),
                  mod_spec, mod_spec, mod_spec,
                  pl.BlockSpec((d, LANES), lambda i: (0, 0)),
                  pl.BlockSpec((1, LANES), lambda i: (0, 0))],
        out_specs=(pl.BlockSpec((ts * tm, d), lambda i: (i, 0)),
                   pl.BlockSpec((ts * n_pos * ROW_WORDS, LANES), lambda i: (i, 0)),
                   pl.BlockSpec((ts * tm, LANES), lambda i: (i, 0)),
                   pl.BlockSpec((ts, 1, LANES), lambda i: (i, 0, 0)),
                   pl.BlockSpec((ts, 1, LANES), lambda i: (i, 0, 0))),
        compiler_params=_cparams(("arbitrary",)),
        name="outproj",
    )(x2, mix_a, mix_b, w_out, gate_m, shift_f, scale_f, w_r, b_r)


def _segment_copies(seg_ref, n_seg, first, last, src_hbm, dst_buf, sem, tbl=0):
    def body(i, carry):
        size = pl.multiple_of(seg_ref[tbl, 0, 2 * n_seg + i], ROW_WORDS)

        @pl.when(size > 0)
        def _():
            src = pl.multiple_of(seg_ref[tbl, 0, i], ROW_WORDS)
            dst = pl.multiple_of(seg_ref[tbl, 0, n_seg + i], ROW_WORDS)
            pltpu.make_async_copy(src_hbm.at[pl.ds(src, size), :], dst_buf.at[pl.ds(dst, size), :], sem).start()
        return carry

    lax.fori_loop(first, last, body, 0)


def _experts_body(be_ref, vl_ref, seg0_ref, segn_ref, hs_hbm, wgu32_ref, bgu_ref, wd32_ref, bd_ref, o_ref,
                  xbuf, xs_ref, wgu_ref, wd_ref, gsem, *, n_blk, n_seg):
    bm = MOE_BM
    rt = ROW_WORDS
    d_ff = wd_ref.shape[0]
    j = pl.program_id(0)
    slot = j % 2
    other = 1 - slot
    valid = vl_ref[j]

    def issue(seg_ref, s):
        _segment_copies(seg_ref, n_seg, seg_ref[0, 0, 3 * n_seg], seg_ref[0, 0, 3 * n_seg + 1],
                        hs_hbm, xbuf.at[s], gsem.at[s])

    @pl.when(j == 0)
    def _():
        xbuf[...] = jnp.zeros_like(xbuf)
        issue(seg0_ref, 0)

    @pl.when((j == 0) | (be_ref[j] != be_ref[jnp.maximum(j - 1, 0)]))
    def _():
        for c in range(d_ff // LANES):
            wgu_ref[:, 2 * c * LANES:(2 * c + 1) * LANES] = (
                wgu32_ref[0, :, c * LANES:(c + 1) * LANES].astype(BF16))
            wgu_ref[:, (2 * c + 1) * LANES:(2 * c + 2) * LANES] = (
                wgu32_ref[0, :, d_ff + c * LANES:d_ff + (c + 1) * LANES].astype(BF16))
        wd_ref[...] = wd32_ref[0].astype(BF16)

    @pl.when(valid > 0)
    def _():
        rows = pl.multiple_of(valid * rt, rt)
        pltpu.make_async_copy(hs_hbm.at[pl.ds(0, rows), :], xbuf.at[slot, pl.ds(0, rows), :],
                              gsem.at[slot]).wait()

    @pl.when(j + 1 < n_blk)
    def _():
        issue(segn_ref, other)

    sub = MOE_SUB
    n_chunk = d_ff // (2 * LANES)
    second = sub * rt
    n_sub = (valid + sub - 1) // sub

    def assemble(r0):
        return _words_to_rows([xbuf[slot, pl.ds(r0 + c, sub, stride=rt), :] for c in range(rt)])

    def act_chunk(xs, c):
        gu = jnp.dot(xs, wgu_ref[:, 4 * c * LANES:(4 * c + 4) * LANES], preferred_element_type=F32)
        outs = []
        for t in range(2):
            lt = 2 * c + t
            gate = gu[:, 2 * t * LANES:(2 * t + 1) * LANES] + bgu_ref[0, :, lt * LANES:(lt + 1) * LANES]
            up = (gu[:, (2 * t + 1) * LANES:(2 * t + 2) * LANES]
                  + bgu_ref[0, :, d_ff + lt * LANES:d_ff + (lt + 1) * LANES])
            gate = jnp.minimum(gate, SWIGLU_LIMIT)
            up = jnp.clip(up, -SWIGLU_LIMIT, SWIGLU_LIMIT)
            glu = gate / (1.0 + jnp.exp(-SWIGLU_ALPHA * gate))
            outs.append(((up + 1.0) * glu).astype(BF16))
        return jnp.concatenate(outs, axis=1)

    def down(acts):
        return jnp.dot(jnp.concatenate(acts, axis=1), wd_ref[...], preferred_element_type=F32) + bd_ref[0]

    def store_rows(r0, ys):
        for c, words in enumerate(_rows_to_words(ys)):
            o_ref[pl.ds(r0 + c, sub, stride=rt), :] = words

    def zero_rows(r0):
        o_ref[pl.ds(r0, sub * rt), :] = jnp.zeros((sub * rt, LANES), I32)

    @pl.when(n_sub == 0)
    def _():
        zero_rows(0)
        zero_rows(second)

    @pl.when(n_sub == 1)
    def _():
        xs_ref[0] = assemble(0)
        xs = xs_ref[0]
        store_rows(0, down([act_chunk(xs, c) for c in range(n_chunk)]))
        zero_rows(second)

    @pl.when(n_sub == 2)
    def _():
        xs_ref[0] = assemble(0)
        xa = xs_ref[0]
        acts = []
        for c in range(n_chunk):
            acts.append(act_chunk(xa, c))
            if c == n_chunk // 2:
                xs_ref[1] = assemble(second)
        ys_a = down(acts)
        xb = xs_ref[1]
        acts = []
        for c in range(n_chunk):
            acts.append(act_chunk(xb, c))
            if c == 1:
                store_rows(0, ys_a)
        store_rows(second, down(acts))


def _experts(blk_expert, blk_valid, seg, hs, wgu, bgu, wd, bd, n_seg):
    bm = MOE_BM
    rt = ROW_WORDS
    n_blk, _, seg_w = seg.shape
    d = wd.shape[2]
    d_ff = wd.shape[1]
    assert d == 2 * rt * LANES and hs.shape[1] == LANES and bm == 2 * MOE_SUB
    smem = functools.partial(pl.BlockSpec, (1, 1, seg_w), memory_space=pltpu.SMEM)
    last = n_blk - 1
    grid_spec = pltpu.PrefetchScalarGridSpec(
        num_scalar_prefetch=2, grid=(n_blk,),
        in_specs=[smem(lambda j, be, vl: (j, 0, 0)),
                  smem(lambda j, be, vl: (jnp.minimum(j + 1, last), 0, 0)),
                  pl.BlockSpec(memory_space=pl.ANY),
                  pl.BlockSpec((1, d, 2 * d_ff), lambda j, be, vl: (be[j], 0, 0)),
                  pl.BlockSpec((1, 1, 2 * d_ff), lambda j, be, vl: (be[j], 0, 0)),
                  pl.BlockSpec((1, d_ff, d), lambda j, be, vl: (be[j], 0, 0)),
                  pl.BlockSpec((1, 1, d), lambda j, be, vl: (be[j], 0, 0))],
        out_specs=pl.BlockSpec((bm * rt, LANES), lambda j, be, vl: (j, 0)),
        scratch_shapes=[pltpu.VMEM((2, bm * rt, LANES), I32), pltpu.VMEM((2, MOE_SUB, d), BF16),
                        pltpu.VMEM((d, 2 * d_ff), BF16), pltpu.VMEM((d_ff, d), BF16),
                        pltpu.SemaphoreType.DMA((2,))])
    return pl.pallas_call(
        functools.partial(_experts_body, n_blk=n_blk, n_seg=n_seg),
        out_shape=jax.ShapeDtypeStruct((n_blk * bm * rt, LANES), I32),
        grid_spec=grid_spec,
        compiler_params=_cparams(("arbitrary",), VMEM_LIMIT_EXPERTS),
        name="experts",
    )(blk_expert, blk_valid, seg, seg, hs, wgu, bgu, wd, bd)


def _combine_body(seg0_ref, segn_ref, x1_ref, pg_ref, gf_ref, gfin_ref, ys_hbm, o_ref,
                  ybuf, sem, *, n_steps):
    tm = SORT_TM
    tiles = range(SORT_TILES_PER_STEP)
    n_pos = TOP_K * tm
    rt = ROW_WORDS
    i = pl.program_id(0)
    slot = i % 2
    other = 1 - slot

    def issue(seg_ref, s):
        for ti in tiles:
            _segment_copies(seg_ref, N_EXPERTS, 0, N_EXPERTS, ys_hbm, ybuf.at[s, ti], sem.at[s], tbl=ti)

    @pl.when(i == 0)
    def _():
        issue(seg0_ref, 0)

    for ti in tiles:
        pltpu.make_async_copy(ys_hbm.at[pl.ds(0, n_pos * rt), :], ybuf.at[slot, ti], sem.at[slot]).wait()

    @pl.when(i + 1 < n_steps)
    def _():
        issue(segn_ref, other)

    lane_pos = lax.broadcasted_iota(I32, (tm, n_pos), 1)
    ws = []
    for ti in tiles:
        pg = pg_ref[ti * tm:(ti + 1) * tm, :]
        w = jnp.zeros((tm, n_pos), F32)
        for k in range(TOP_K):
            w = jnp.where(lane_pos == pg[:, k:k + 1].astype(I32), pg[:, TOP_K + k:TOP_K + k + 1], w)
        ws.append(w)
    w_his = [w.astype(BF16) for w in ws]
    w_los = [(ws[ti] - w_his[ti].astype(F32)).astype(BF16) for ti in tiles]
    ysbs = [_words_to_rows([ybuf[slot, ti, pl.ds(c, n_pos, stride=rt), :] for c in range(rt)])
            for ti in tiles]
    ys = [jnp.dot(w_his[ti], ysbs[ti], preferred_element_type=F32)
          + jnp.dot(w_los[ti], ysbs[ti], preferred_element_type=F32) for ti in tiles]
    for ti in tiles:
        x2 = x1_ref[ti * tm:(ti + 1) * tm, :] + gf_ref[0] * ys[ti]
        o_ref[ti * tm:(ti + 1) * tm, :] = _rms(x2) * gfin_ref[...]


def _combine(seg, x1, pg, gate_f, g_final, ys, seq):
    t, d = x1.shape
    tm = SORT_TM
    ts = SORT_TILES_PER_STEP
    tpb = seq // (ts * tm)
    n_steps = t // (ts * tm)
    n_pos = TOP_K * tm
    seg_w = seg.shape[2]
    smem = functools.partial(pl.BlockSpec, (ts, 1, seg_w), memory_space=pltpu.SMEM)
    return pl.pallas_call(
        functools.partial(_combine_body, n_steps=n_steps),
        out_shape=jax.ShapeDtypeStruct((t, d), F32),
        grid=(n_steps,),
        in_specs=[smem(lambda i: (i, 0, 0)),
                  smem(lambda i: (jnp.minimum(i + 1, n_steps - 1), 0, 0)),
                  pl.BlockSpec((ts * tm, d), lambda i: (i, 0)),
                  pl.BlockSpec((ts * tm, LANES), lambda i: (i, 0)),
                  pl.BlockSpec((1, 1, d), lambda i: (i // tpb, 0, 0)),
                  pl.BlockSpec((1, d), lambda i: (0, 0)),
                  pl.BlockSpec(memory_space=pl.ANY)],
        out_specs=pl.BlockSpec((ts * tm, d), lambda i: (i, 0)),
        scratch_shapes=[pltpu.VMEM((2, ts, n_pos * ROW_WORDS, LANES), I32), pltpu.SemaphoreType.DMA((2,))],
        compiler_params=_cparams(("arbitrary",)),
        name="combine",
    )(seg, seg, x1, pg, gate_f, g_final, ys)


def _rope_tables(seq):
    half = ROT_DIMS // 2
    pos = jnp.arange(seq, dtype=F32)
    inv_freq = ROPE_THETA ** (-jnp.arange(0, ROT_DIMS, 2, dtype=F32) / ROT_DIMS)
    ang = pos[:, None] * inv_freq[None, :]
    cos, sin = jnp.cos(ang), jnp.sin(ang)
    ones = jnp.ones((seq, HEAD_DIM - ROT_DIMS), F32)
    zeros = jnp.zeros((seq, HEAD_DIM - ROT_DIMS), F32)
    zh = jnp.zeros((seq, half), F32)
    rep = LANES // HEAD_DIM
    cos_t = jnp.tile(jnp.concatenate([cos, cos, ones], axis=1), (1, rep))
    sa_t = jnp.tile(jnp.concatenate([-sin, zh, zeros], axis=1), (1, rep))
    sb_t = jnp.tile(jnp.concatenate([zh, sin, zeros], axis=1), (1, rep))
    return cos_t, sa_t, sb_t


def _arrange_in_proj(w_in):
    scale = HEAD_DIM ** -0.5 * LOG2E
    o = 0
    wqa = w_in[:, o:o + A_Q_W] * scale; o += A_Q_W
    wka = w_in[:, o:o + A_KV_W]; o += A_KV_W
    wva = w_in[:, o:o + A_KV_W]; o += A_KV_W
    wqb = w_in[:, o:o + B_W] * scale; o += B_W
    wkb = w_in[:, o:o + B_W]; o += B_W
    wvb = w_in[:, o:o + B_W]
    return jnp.concatenate([wqa, wqb, wkb, wvb, wka, wva], axis=1).astype(BF16)


def _round_up(n, m):
    return (n + m - 1) // m * m


def _moe_tables(tile_cnt, tile_off, n_blk):
    bm = MOE_BM
    nt = tile_cnt.shape[0]
    n_pos = TOP_K * SORT_TM
    cum = jnp.cumsum(tile_cnt, axis=0) - tile_cnt
    counts = jnp.sum(tile_cnt, axis=0)
    padded = ((counts + bm - 1) // bm) * bm
    pends = jnp.cumsum(padded)
    pstarts = pends - padded
    blk_first = jnp.arange(n_blk, dtype=I32) * bm
    blk_expert = jnp.minimum(
        jnp.sum((pends[None, :] <= blk_first[:, None]).astype(I32), axis=1), N_EXPERTS - 1)
    own = blk_expert[:, None] == jnp.arange(N_EXPERTS, dtype=I32)[None, :]

    def of_block_expert(per_expert):
        return jnp.sum(jnp.where(own[:, None, :], per_expert[None], 0), axis=-1)

    r0 = blk_first - of_block_expert(pstarts[None, :])[:, 0]
    blk_valid = jnp.clip(of_block_expert(counts[None, :])[:, 0] - r0, 0, bm)
    cum_b = of_block_expert(cum)
    cnt_b = of_block_expert(tile_cnt)
    off_b = of_block_expert(tile_off)
    lo = jnp.maximum(r0[:, None], cum_b)
    hi = jnp.minimum(r0[:, None] + bm, cum_b + cnt_b)
    size = jnp.maximum(hi - lo, 0)
    tile_ids = jnp.arange(nt, dtype=I32)[None, :]
    src = tile_ids * n_pos + off_b + (lo - cum_b)
    dst = jnp.where(size > 0, lo - r0[:, None], 0)
    first = jnp.min(jnp.where(size > 0, tile_ids, nt), axis=1)
    last = jnp.max(jnp.where(size > 0, tile_ids + 1, 0), axis=1)
    first = jnp.minimum(first, last)
    seg_w = _round_up(3 * nt + 2, LANES)
    exp_seg = jnp.concatenate(
        [src * ROW_WORDS, dst * ROW_WORDS, size * ROW_WORDS, first[:, None], last[:, None],
         jnp.zeros((n_blk, seg_w - 3 * nt - 2), I32)], axis=1).reshape(n_blk, 1, seg_w)
    comb_seg = jnp.concatenate(
        [(pstarts[None, :] + cum) * ROW_WORDS, tile_off * ROW_WORDS, tile_cnt * ROW_WORDS,
         jnp.zeros((nt, LANES - 3 * N_EXPERTS), I32)], axis=1).reshape(nt, 1, LANES)
    return blk_expert, blk_valid, exp_seg, comb_seg


def kernel(x, c, w_ada, b_ada, w_in, sink, rpb, g_out_a, g_out_b, w_out, w_router, b_router,
           w_gate_up, b_gate_up, w_down, b_down, g_final):
    bsz, seq, d = x.shape
    t = bsz * seq
    depth = w_ada.shape[0]
    assert depth == 1, "the final norm is fused into the layer combine"
    bm = MOE_BM
    n_blk = (t * TOP_K) // bm + N_EXPERTS
    nt = t // SORT_TM

    cos_t, sa_t, sb_t = _rope_tables(seq)
    x2 = x.reshape(t, d)
    for l in range(depth):
        mod = _ada(c, w_ada[l], b_ada[l][None, :])
        shift_m, scale_m, gate_m, shift_f, scale_f, gate_f = [
            m.reshape(bsz, 1, d) for m in jnp.split(mod, 6, axis=-1)]

        proj = _inproj(x2, shift_m, scale_m, _arrange_in_proj(w_in[l]), cos_t, sa_t, sb_t, seq)
        mix_a = _swa(proj, sink[l], g_out_a[l][None, :], bsz, seq)
        mix_b = _natten(proj, _na_bias_table(rpb[l], seq // GRID_W), g_out_b[l][None, :], bsz, seq)

        w_r = jnp.pad(w_router[l], ((0, 0), (0, LANES - N_EXPERTS)))
        b_r = jnp.pad(b_router[l], (0, LANES - N_EXPERTS), constant_values=NEG)[None, :]
        x1, hs, pg, tile_cnt, tile_off = _outproj(
            x2, mix_a, mix_b, w_out[l].astype(BF16), gate_m, shift_f, scale_f, w_r, b_r, seq)

        blk_expert, blk_valid, exp_seg, comb_seg = _moe_tables(
            tile_cnt[:, 0, :N_EXPERTS].astype(I32), tile_off[:, 0, :N_EXPERTS].astype(I32), n_blk)
        ys = _experts(blk_expert, blk_valid, exp_seg, hs,
                      w_gate_up[l], b_gate_up[l][:, None, :],
                      w_down[l], b_down[l][:, None, :], nt)
        x2 = _combine(comb_seg, x1, pg, gate_f, g_final[None, :], ys, seq)
    return x2.reshape(bsz, seq, d)
```

```python
import functools

import jax
import jax.numpy as jnp
from jax import lax
from jax.experimental import pallas as pl
from jax.experimental.pallas import tpu as pltpu

F32 = jnp.float32
BF16 = jnp.bfloat16
I32 = jnp.int32

HEAD_DIM = 64
A_Q_HEADS = 8
A_KV_HEADS = 2
A_GROUP = A_Q_HEADS // A_KV_HEADS
B_HEADS = 8
A_Q_W = A_Q_HEADS * HEAD_DIM
A_KV_W = A_KV_HEADS * HEAD_DIM
B_W = B_HEADS * HEAD_DIM
WINDOW = 128
ROT_DIMS = HEAD_DIM // 4
ROPE_THETA = 500000.0
GRID_W = 64
NA_ROWS_MAX = 8
NA_COLS = 16
N_EXPERTS = 32
TOP_K = 4
SWIGLU_LIMIT = 7.0
SWIGLU_ALPHA = 1.702
EPS = 1e-5
NEG = -1e30
LOG2E = 1.4426950408889634

LANES = 128
ROW_WORDS = 4
PROJ_BLK = 512
KV_A_COL_BLOCK = (A_Q_W + 3 * B_W) // (2 * A_KV_W)
TM_PROJ = 512
SORT_TM = 256
SORT_TILES_PER_STEP = 2
MOE_BM = 1024
MOE_SUB = 512
SWA_BLOCKS_PER_STEP = 2
SWA_STAGED_BLOCKS = 1
NA_ROWS_PER_STEP = 8
NA_ROW_UNROLL = 4
VMEM_LIMIT = 48 << 20
VMEM_LIMIT_EXPERTS = 58 << 20


def _cparams(sem, limit=VMEM_LIMIT):
    return pltpu.CompilerParams(dimension_semantics=sem, vmem_limit_bytes=limit)


def _rms(x):
    return x * lax.rsqrt(jnp.mean(x * x, axis=-1, keepdims=True) + EPS)


def _pack_bf16_pair(lo, hi):
    return pltpu.pack_elementwise([lo, hi], packed_dtype=BF16)


def _unpack_bf16_pair(words):
    return (pltpu.unpack_elementwise(words, index=0, packed_dtype=BF16, unpacked_dtype=F32),
            pltpu.unpack_elementwise(words, index=1, packed_dtype=BF16, unpacked_dtype=F32))


def _rows_to_words(x):
    return [_pack_bf16_pair(x[:, 2 * c * LANES:(2 * c + 1) * LANES], x[:, (2 * c + 1) * LANES:(2 * c + 2) * LANES])
            for c in range(ROW_WORDS)]


def _words_to_rows(words):
    parts = []
    for w in words:
        parts.extend(_unpack_bf16_pair(w))
    return jnp.concatenate(parts, axis=1).astype(BF16)


def _ada_body(c_ref, w_ref, b_ref, o_ref):
    c = c_ref[...]
    ca = c / (1.0 + jnp.exp(-c))
    o_ref[...] = jnp.dot(ca, w_ref[...], preferred_element_type=F32,
                         precision=lax.Precision.HIGHEST) + b_ref[...]


def _ada(c, w, b):
    bsz, d = c.shape
    n = w.shape[1]
    bn = n // 4
    return pl.pallas_call(
        _ada_body,
        out_shape=jax.ShapeDtypeStruct((bsz, n), F32),
        grid=(4,),
        in_specs=[pl.BlockSpec((bsz, d), lambda i: (0, 0)),
                  pl.BlockSpec((d, bn), lambda i: (0, i)),
                  pl.BlockSpec((1, bn), lambda i: (0, i))],
        out_specs=pl.BlockSpec((bsz, bn), lambda i: (0, i)),
        compiler_params=_cparams(("arbitrary",)),
        name="ada",
    )(c, w, b)


def _inproj_body(x_ref, sh_ref, sc_ref, w_ref, cos_ref, sa_ref, sb_ref, o_ref):
    h = (_rms(x_ref[...]) * (1.0 + sc_ref[0]) + sh_ref[0]).astype(BF16)
    cos = cos_ref[...]
    sa = sa_ref[...]
    sb = sb_ref[...]
    blocks = [(0, A_Q_W, A_Q_W // LANES)]
    blocks += [(A_Q_W + i * B_W, B_W, 0) for i in range(3)]
    blocks += [(A_Q_W + 3 * B_W, 2 * A_KV_W, A_KV_W // LANES)]
    for c0, width, n_rot in blocks:
        p = jnp.dot(h, w_ref[:, c0:c0 + width], preferred_element_type=F32)
        if n_rot:
            parts = []
            for j in range(width // LANES):
                pj = p[:, j * LANES:(j + 1) * LANES]
                if j < n_rot:
                    pj = (pj * cos + pltpu.roll(pj, LANES - ROT_DIMS // 2, axis=1) * sa
                          + pltpu.roll(pj, ROT_DIMS // 2, axis=1) * sb)
                parts.append(pj)
            p = jnp.concatenate(parts, axis=1)
        o_ref[:, c0:c0 + width] = p.astype(BF16)


def _inproj(x2, shift, scale, w_ext, cos_t, sa_t, sb_t, seq):
    t, d = x2.shape
    n = w_ext.shape[1]
    tm = TM_PROJ
    tpb = seq // tm
    return pl.pallas_call(
        _inproj_body,
        out_shape=jax.ShapeDtypeStruct((t, n), BF16),
        grid=(t // tm,),
        in_specs=[pl.BlockSpec((tm, d), lambda i: (i, 0)),
                  pl.BlockSpec((1, 1, d), lambda i: (i // tpb, 0, 0)),
                  pl.BlockSpec((1, 1, d), lambda i: (i // tpb, 0, 0)),
                  pl.BlockSpec((d, n), lambda i: (0, 0)),
                  pl.BlockSpec((tm, LANES), lambda i: (i % tpb, 0)),
                  pl.BlockSpec((tm, LANES), lambda i: (i % tpb, 0)),
                  pl.BlockSpec((tm, LANES), lambda i: (i % tpb, 0))],
        out_specs=pl.BlockSpec((tm, n), lambda i: (i, 0)),
        compiler_params=_cparams(("arbitrary",)),
        name="inproj",
    )(x2, shift, scale, w_ext, cos_t, sa_t, sb_t)


def _swa_body(sink_ref, q_ref, kv_ref, g_ref, o_ref, *, seq):
    step = pl.program_id(1)
    blk = WINDOW
    kw_len = 3 * blk
    gw = A_GROUP * HEAD_DIM
    srows = A_GROUP * blk
    row = lax.broadcasted_iota(I32, (srows, 1), 0)
    grp = lax.broadcasted_iota(I32, (1, gw), 1) // HEAD_DIM
    blocks = range(SWA_BLOCKS_PER_STEP)
    def spread(tile):
        words = pltpu.bitcast(tile, jnp.uint32)
        swapped = pltpu.roll(words, HEAD_DIM, axis=1)
        low = lax.broadcasted_iota(I32, words.shape, 1) < HEAD_DIM
        out = []
        for pair in (jnp.where(low, words, swapped), jnp.where(low, swapped, words)):
            pair = pltpu.bitcast(pair, BF16)
            out.append(jnp.concatenate([pair] * (gw // LANES), axis=1))
        return out

    kws, vws, alloweds = [], [], []
    for bi in blocks:
        n = step * SWA_BLOCKS_PER_STEP + bi
        start = pl.multiple_of(jnp.clip(n * blk - blk, 0, seq - kw_len), blk)
        kv = kv_ref[pl.ds(start, kw_len), :]
        kws.append(jnp.concatenate(spread(kv[:, :A_KV_W]), axis=1))
        vws.append(jnp.concatenate(spread(kv[:, A_KV_W:]), axis=1))
        qpos = n * blk + row % blk
        kpos = start + lax.broadcasted_iota(I32, (1, kw_len), 1)
        alloweds.append(jnp.abs(qpos - kpos) <= WINDOW)
    sks = []
    for h in range(A_KV_HEADS):
        sk = jnp.full((srows, 1), sink_ref[h * A_GROUP], F32)
        for g in range(1, A_GROUP):
            sk = jnp.where(row // blk == g, sink_ref[h * A_GROUP + g], sk)
        sks.append(sk * LOG2E)
    for b0 in range(0, SWA_BLOCKS_PER_STEP, SWA_STAGED_BLOCKS):
        group = range(b0, b0 + SWA_STAGED_BLOCKS)
        units = [(bi, h) for bi in group for h in range(A_KV_HEADS)]
        qms = []
        for bi, h in units:
            qh = q_ref[bi * blk:(bi + 1) * blk, h * gw:(h + 1) * gw]
            qms.append(jnp.concatenate(
                [jnp.where(grp == g, qh, jnp.zeros_like(qh)) for g in range(A_GROUP)], axis=0))
        ss = [lax.dot_general(qms[u], kws[bi][:, h * gw:(h + 1) * gw], (((1,), (1,)), ((), ())),
                              preferred_element_type=F32) for u, (bi, h) in enumerate(units)]
        ss = [jnp.where(alloweds[bi], ss[u], NEG) for u, (bi, h) in enumerate(units)]
        ms = [jnp.maximum(jnp.max(ss[u], axis=-1, keepdims=True), sks[h]) for u, (bi, h) in enumerate(units)]
        ps = [jnp.exp2(ss[u] - ms[u]) for u in range(len(units))]
        invs = [1.0 / (jnp.sum(ps[u], axis=-1, keepdims=True) + jnp.exp2(sks[h] - ms[u]))
                for u, (bi, h) in enumerate(units)]
        ress = [jnp.dot(ps[u].astype(BF16), vws[bi][:, h * gw:(h + 1) * gw], preferred_element_type=F32) * invs[u]
                for u, (bi, h) in enumerate(units)]
        accs = []
        for res in ress:
            acc = res[0:blk]
            for g in range(1, A_GROUP):
                acc = jnp.where(grp == g, res[g * blk:(g + 1) * blk], acc)
            accs.append(acc)
        for gi, bi in enumerate(group):
            oa = jnp.concatenate(accs[gi * A_KV_HEADS:(gi + 1) * A_KV_HEADS], axis=1)
            o_ref[bi * blk:(bi + 1) * blk, :] = (_rms(oa) * g_ref[...]).astype(BF16)


def _swa(proj, sink, g_a, bsz, seq):
    t = proj.shape[0]
    blk = WINDOW * SWA_BLOCKS_PER_STEP
    nq = seq // blk
    grid_spec = pltpu.PrefetchScalarGridSpec(
        num_scalar_prefetch=1, grid=(bsz, nq),
        in_specs=[pl.BlockSpec((blk, PROJ_BLK), lambda b, n, s: (b * nq + n, 0)),
                  pl.BlockSpec((seq, 2 * A_KV_W), lambda b, n, s: (b, KV_A_COL_BLOCK)),
                  pl.BlockSpec((1, A_Q_W), lambda b, n, s: (0, 0))],
        out_specs=pl.BlockSpec((blk, A_Q_W), lambda b, n, s: (b * nq + n, 0)))
    return pl.pallas_call(
        functools.partial(_swa_body, seq=seq),
        out_shape=jax.ShapeDtypeStruct((t, A_Q_W), BF16),
        grid_spec=grid_spec,
        compiler_params=_cparams(("arbitrary", "arbitrary")),
        name="swa",
    )(sink, proj, proj, g_a)


def _na_body(q_ref, k_ref, v_ref, bias_ref, g_ref, o_ref, *, rows, kr, rb):
    blk = pl.program_id(1)
    per = LANES // HEAD_DIM
    n_pair = B_W // LANES
    half = lax.broadcasted_iota(I32, (1, LANES), 1) // HEAD_DIM
    gain = g_ref[...]

    def one_row(i, carry):
        r = blk * rb + i
        rs = jnp.clip(r - kr // 2, 0, rows - kr)
        start = pl.multiple_of(rs * GRID_W, GRID_W)
        kw = k_ref[pl.ds(start, kr * GRID_W), :]
        vw = v_ref[pl.ds(start, kr * GRID_W), :]
        q = q_ref[pl.ds(pl.multiple_of(i * GRID_W, GRID_W), GRID_W), :]
        s_parts = []
        for pr in range(n_pair):
            qp = q[:, pr * LANES:(pr + 1) * LANES]
            qm = jnp.concatenate([jnp.where(half == j, qp, jnp.zeros_like(qp)) for j in range(per)], axis=0)
            s_parts.append(lax.dot_general(qm, kw[:, pr * LANES:(pr + 1) * LANES],
                                           (((1,), (1,)), ((), ())), preferred_element_type=F32))
        s = jnp.concatenate(s_parts, axis=0) + bias_ref[r - rs]
        m = jnp.max(s, axis=-1, keepdims=True)
        p = jnp.exp2(s - m)
        inv = 1.0 / jnp.sum(p, axis=-1, keepdims=True)
        pn = p.astype(BF16)
        o_parts = []
        for pr in range(n_pair):
            srows = slice(pr * per * GRID_W, (pr + 1) * per * GRID_W)
            res = jnp.dot(pn[srows], vw[:, pr * LANES:(pr + 1) * LANES], preferred_element_type=F32) * inv[srows]
            op = res[0:GRID_W]
            for j in range(1, per):
                op = jnp.where(half == j, res[j * GRID_W:(j + 1) * GRID_W], op)
            o_parts.append(op)
        ob = jnp.concatenate(o_parts, axis=1)
        o_ref[pl.ds(pl.multiple_of(i * GRID_W, GRID_W), GRID_W), :] = (_rms(ob) * gain).astype(BF16)
        return carry

    def row_group(ig, carry):
        for u in range(NA_ROW_UNROLL):
            one_row(NA_ROW_UNROLL * ig + u, carry)
        return carry

    lax.fori_loop(0, rb // NA_ROW_UNROLL, row_group, 0)


def _natten(proj, bias_t, g_b, bsz, seq):
    t = proj.shape[0]
    rows = seq // GRID_W
    kr = min(NA_ROWS_MAX, rows)
    rb = NA_ROWS_PER_STEP
    nb = rows // rb
    return pl.pallas_call(
        functools.partial(_na_body, rows=rows, kr=kr, rb=rb),
        out_shape=jax.ShapeDtypeStruct((t, B_W), BF16),
        grid=(bsz, nb),
        in_specs=[pl.BlockSpec((rb * GRID_W, PROJ_BLK), lambda b, r: (b * nb + r, 1)),
                  pl.BlockSpec((seq, PROJ_BLK), lambda b, r: (b, 2)),
                  pl.BlockSpec((seq, PROJ_BLK), lambda b, r: (b, 3)),
                  pl.BlockSpec((kr, B_HEADS * GRID_W, kr * GRID_W), lambda b, r: (0, 0, 0)),
                  pl.BlockSpec((1, B_W), lambda b, r: (0, 0))],
        out_specs=pl.BlockSpec((rb * GRID_W, B_W), lambda b, r: (b * nb + r, 0)),
        compiler_params=_cparams(("arbitrary", "arbitrary")),
        name="natten",
    )(proj, proj, proj, bias_t, g_b)


def _na_bias_table(rpb, rows):
    kr = min(NA_ROWS_MAX, rows)
    kc = NA_COLS
    col = jnp.arange(GRID_W)
    col_start = jnp.clip(col - kc // 2, 0, GRID_W - kc)
    col_mask = (col[None, :] >= col_start[:, None]) & (col[None, :] < col_start[:, None] + kc)
    col_off = jnp.clip(col[None, :] - col[:, None], -(kc - 1), kc - 1) + (NA_COLS - 1)
    rows_sel = jnp.stack([rpb[:, NA_ROWS_MAX - 1 - cs:NA_ROWS_MAX - 1 - cs + kr, :]
                          for cs in range(kr)], axis=0)
    pick = (col_off[None, :, :] == jnp.arange(2 * kc - 1)[:, None, None]).astype(F32)
    tbl = jnp.einsum('chjr,rqk->chqjk', rows_sel, pick, precision=lax.Precision.HIGHEST)
    tbl = jnp.where(col_mask[None, None, :, None, :], tbl * LOG2E, NEG)
    return tbl.reshape(kr, rpb.shape[0] * GRID_W, kr * GRID_W).astype(F32)


def _outproj_body(x_ref, ma_ref, mb_ref, wo_ref, gm_ref, sf_ref, cf_ref, wr_ref, br_ref,
                  x1_ref, hs_ref, pg_ref, tc_ref, to_ref):
    tm = SORT_TM
    tiles = range(SORT_TILES_PER_STEP)
    tt = tm * SORT_TILES_PER_STEP
    n_pos = TOP_K * tm
    mix = (jnp.dot(ma_ref[...], wo_ref[:A_Q_W, :], preferred_element_type=F32)
           + jnp.dot(mb_ref[...], wo_ref[A_Q_W:, :], preferred_element_type=F32))
    x1 = x_ref[...] + gm_ref[0] * mix
    x1_ref[...] = x1
    h = _rms(x1) * (1.0 + cf_ref[0]) + sf_ref[0]
    h_hi = h.astype(BF16)
    h_lo = (h - h_hi.astype(F32)).astype(BF16)
    w_r = wr_ref[...]
    w_hi = w_r.astype(BF16)
    w_lo = (w_r - w_hi.astype(F32)).astype(BF16)
    l_hi = jnp.dot(h_hi, jnp.concatenate([w_hi, w_lo], axis=1), preferred_element_type=F32)
    logits = (l_hi[:, :LANES] + l_hi[:, LANES:]
              + jnp.dot(h_lo, w_hi, preferred_element_type=F32) + br_ref[...])
    lane = lax.broadcasted_iota(I32, (tt, LANES), 1)
    work = logits
    vals, idxs = [], []
    for _ in range(TOP_K):
        m = jnp.max(work, axis=-1, keepdims=True)
        ik = jnp.min(jnp.where(work == m, lane, LANES), axis=-1, keepdims=True)
        vals.append(m)
        idxs.append(ik)
        work = jnp.where(lane == ik, -jnp.inf, work)
    es = [jnp.exp(v - vals[0]) for v in vals]
    den = es[0] + es[1] + es[2] + es[3]
    mh = jnp.zeros((tt, LANES), F32)
    for ik in idxs:
        mh = mh + (lane == ik).astype(F32)
    mhb = mh.astype(BF16)
    tri = (lax.broadcasted_iota(I32, (tm, tm), 0) > lax.broadcasted_iota(I32, (tm, tm), 1)).astype(BF16)
    lower = (lax.broadcasted_iota(I32, (LANES, LANES), 0)
             < lax.broadcasted_iota(I32, (LANES, LANES), 1)).astype(BF16)
    mh_t = [mhb[ti * tm:(ti + 1) * tm] for ti in tiles]
    earlier = [jnp.dot(tri, mh_t[ti], preferred_element_type=F32) for ti in tiles]
    below = [jnp.dot(mh_t[ti], lower, preferred_element_type=F32) for ti in tiles]
    toff = [jnp.sum(below[ti], axis=0, keepdims=True) for ti in tiles]
    cnt = [jnp.sum(mh[ti * tm:(ti + 1) * tm], axis=0, keepdims=True) for ti in tiles]
    posf = jnp.concatenate([earlier[ti] + toff[ti] for ti in tiles], axis=0)
    pg = jnp.zeros((tt, LANES), F32)
    for k in range(TOP_K):
        pos_k = jnp.sum(jnp.where(lane == idxs[k], posf, 0.0), axis=-1, keepdims=True)
        pg = jnp.where(lane == k, pos_k, pg)
        pg = jnp.where(lane == TOP_K + k, es[k] / den, pg)
    pg_ref[...] = pg
    pgt_all = pg.T[:ROW_WORDS]
    pgt = [pgt_all[:, ti * tm:(ti + 1) * tm] for ti in tiles]
    for ti in tiles:
        tc_ref[ti] = cnt[ti]
        to_ref[ti] = toff[ti]
    pos_iota = lax.broadcasted_iota(I32, (n_pos, tm), 0)
    hits = []
    for ti in tiles:
        hit = pos_iota == pgt[ti][0:1, :].astype(I32)
        for k in range(1, TOP_K):
            hit = hit | (pos_iota == pgt[ti][k:k + 1, :].astype(I32))
        hits.append(jnp.where(hit, 1.0, 0.0).astype(BF16))
    srt = [jnp.dot(hits[ti], h_hi[ti * tm:(ti + 1) * tm], preferred_element_type=F32) for ti in tiles]
    for ti in tiles:
        for c, words in enumerate(_rows_to_words(srt[ti])):
            hs_ref[pl.ds(ti * n_pos * ROW_WORDS + c, n_pos, stride=ROW_WORDS), :] = words


def _outproj(x2, mix_a, mix_b, w_out, gate_m, shift_f, scale_f, w_r, b_r, seq):
    t, d = x2.shape
    tm = SORT_TM
    nt = t // tm
    ts = SORT_TILES_PER_STEP
    tpb = seq // (ts * tm)
    n_pos = TOP_K * tm
    mod_spec = pl.BlockSpec((1, 1, d), lambda i: (i // tpb, 0, 0))
    return pl.pallas_call(
        _outproj_body,
        out_shape=(jax.ShapeDtypeStruct((t, d), F32),
                   jax.ShapeDtypeStruct((t * TOP_K * ROW_WORDS, LANES), I32),
                   jax.ShapeDtypeStruct((t, LANES), F32),
                   jax.ShapeDtypeStruct((nt, 1, LANES), F32),
                   jax.ShapeDtypeStruct((nt, 1, LANES), F32)),
        grid=(nt // ts,),
        in_specs=[pl.BlockSpec((ts * tm, d), lambda i: (i, 0)),
                  pl.BlockSpec((ts * tm, A_Q_W), lambda i: (i, 0)),
                  pl.BlockSpec((ts * tm, B_W), lambda i: (i, 0)),
                  pl.BlockSpec((A_Q_W + B_W, d), lambda i: (0, 0)),
                  mod_spec, mod_spec, mod_spec,
                  pl.BlockSpec((d, LANES), lambda i: (0, 0)),
                  pl.BlockSpec((1, LANES), lambda i: (0, 0))],
        out_specs=(pl.BlockSpec((ts * tm, d), lambda i: (i, 0)),
                   pl.BlockSpec((ts * n_pos * ROW_WORDS, LANES), lambda i: (i, 0)),
                   pl.BlockSpec((ts * tm, LANES), lambda i: (i, 0)),
                   pl.BlockSpec((ts, 1, LANES), lambda i: (i, 0, 0)),
                   pl.BlockSpec((ts, 1, LANES), lambda i: (i, 0, 0))),
        compiler_params=_cparams(("arbitrary",)),
        name="outproj",
    )(x2, mix_a, mix_b, w_out, gate_m, shift_f, scale_f, w_r, b_r)


def _segment_copies(seg_ref, n_seg, first, last, src_hbm, dst_buf, sem, tbl=0):
    def body(i, carry):
        size = pl.multiple_of(seg_ref[tbl, 0, 2 * n_seg + i], ROW_WORDS)

        @pl.when(size > 0)
        def _():
            src = pl.multiple_of(seg_ref[tbl, 0, i], ROW_WORDS)
            dst = pl.multiple_of(seg_ref[tbl, 0, n_seg + i], ROW_WORDS)
            pltpu.make_async_copy(src_hbm.at[pl.ds(src, size), :], dst_buf.at[pl.ds(dst, size), :], sem).start()
        return carry

    lax.fori_loop(first, last, body, 0)


def _experts_body(be_ref, vl_ref, seg0_ref, segn_ref, hs_hbm, wgu32_ref, bgu_ref, wd32_ref, bd_ref, o_ref,
                  xbuf, xs_ref, wgu_ref, wd_ref, gsem, *, n_blk, n_seg):
    bm = MOE_BM
    rt = ROW_WORDS
    d_ff = wd_ref.shape[0]
    j = pl.program_id(0)
    slot = j % 2
    other = 1 - slot
    valid = vl_ref[j]

    def issue(seg_ref, s):
        _segment_copies(seg_ref, n_seg, seg_ref[0, 0, 3 * n_seg], seg_ref[0, 0, 3 * n_seg + 1],
                        hs_hbm, xbuf.at[s], gsem.at[s])

    @pl.when(j == 0)
    def _():
        xbuf[...] = jnp.zeros_like(xbuf)
        issue(seg0_ref, 0)

    @pl.when((j == 0) | (be_ref[j] != be_ref[jnp.maximum(j - 1, 0)]))
    def _():
        for c in range(d_ff // LANES):
            wgu_ref[:, 2 * c * LANES:(2 * c + 1) * LANES] = (
                wgu32_ref[0, :, c * LANES:(c + 1) * LANES].astype(BF16))
            wgu_ref[:, (2 * c + 1) * LANES:(2 * c + 2) * LANES] = (
                wgu32_ref[0, :, d_ff + c * LANES:d_ff + (c + 1) * LANES].astype(BF16))
        wd_ref[...] = wd32_ref[0].astype(BF16)

    @pl.when(valid > 0)
    def _():
        rows = pl.multiple_of(valid * rt, rt)
        pltpu.make_async_copy(hs_hbm.at[pl.ds(0, rows), :], xbuf.at[slot, pl.ds(0, rows), :],
                              gsem.at[slot]).wait()

    @pl.when(j + 1 < n_blk)
    def _():
        issue(segn_ref, other)

    sub = MOE_SUB
    n_chunk = d_ff // (2 * LANES)
    half = sub // 2
    second = sub * rt
    n_half = (valid + half - 1) // half

    def assemble(r0, n):
        return _words_to_rows([xbuf[slot, pl.ds(r0 + c, n, stride=rt), :] for c in range(rt)])

    def act_chunk(xs, c):
        gu = jnp.dot(xs, wgu_ref[:, 4 * c * LANES:(4 * c + 4) * LANES], preferred_element_type=F32)
        outs = []
        for t in range(2):
            lt = 2 * c + t
            gate = gu[:, 2 * t * LANES:(2 * t + 1) * LANES] + bgu_ref[0, :, lt * LANES:(lt + 1) * LANES]
            up = (gu[:, (2 * t + 1) * LANES:(2 * t + 2) * LANES]
                  + bgu_ref[0, :, d_ff + lt * LANES:d_ff + (lt + 1) * LANES])
            gate = jnp.minimum(gate, SWIGLU_LIMIT)
            up = jnp.clip(up, -SWIGLU_LIMIT, SWIGLU_LIMIT)
            glu = gate / (1.0 + jnp.exp(-SWIGLU_ALPHA * gate))
            outs.append(((up + 1.0) * glu).astype(BF16))
        return jnp.concatenate(outs, axis=1)

    def down(acts):
        return jnp.dot(jnp.concatenate(acts, axis=1), wd_ref[...], preferred_element_type=F32) + bd_ref[0]

    def store_rows(r0, ys):
        for c, words in enumerate(_rows_to_words(ys)):
            o_ref[pl.ds(r0 + c, ys.shape[0], stride=rt), :] = words

    def zero_rows(r0, n):
        o_ref[pl.ds(r0, n * rt), :] = jnp.zeros((n * rt, LANES), I32)

    def single_pass(n):
        xs_ref[0, :n, :] = assemble(0, n)
        xs = xs_ref[0, :n, :]
        store_rows(0, down([act_chunk(xs, c) for c in range(n_chunk)]))
        zero_rows(n * rt, bm - n)

    def double_pass(n2):
        xs_ref[0] = assemble(0, sub)
        xa = xs_ref[0]
        acts = []
        for c in range(n_chunk):
            acts.append(act_chunk(xa, c))
            if c == n_chunk // 2:
                xs_ref[1, :n2, :] = assemble(second, n2)
        ys_a = down(acts)
        xb = xs_ref[1, :n2, :]
        acts = []
        for c in range(n_chunk):
            acts.append(act_chunk(xb, c))
            if c == 1:
                store_rows(0, ys_a)
        store_rows(second, down(acts))
        if n2 < sub:
            zero_rows(second + n2 * rt, sub - n2)

    @pl.when(n_half == 0)
    def _():
        zero_rows(0, bm)

    for nh, run in ((1, lambda: single_pass(half)), (2, lambda: single_pass(sub)),
                    (3, lambda: double_pass(half)), (4, lambda: double_pass(sub))):
        pl.when(n_half == nh)(run)


def _experts(blk_expert, blk_valid, seg, hs, wgu, bgu, wd, bd, n_seg):
    bm = MOE_BM
    rt = ROW_WORDS
    n_blk, _, seg_w = seg.shape
    d = wd.shape[2]
    d_ff = wd.shape[1]
    assert d == 2 * rt * LANES and hs.shape[1] == LANES and bm == 2 * MOE_SUB
    smem = functools.partial(pl.BlockSpec, (1, 1, seg_w), memory_space=pltpu.SMEM)
    last = n_blk - 1
    grid_spec = pltpu.PrefetchScalarGridSpec(
        num_scalar_prefetch=2, grid=(n_blk,),
        in_specs=[smem(lambda j, be, vl: (j, 0, 0)),
                  smem(lambda j, be, vl: (jnp.minimum(j + 1, last), 0, 0)),
                  pl.BlockSpec(memory_space=pl.ANY),
                  pl.BlockSpec((1, d, 2 * d_ff), lambda j, be, vl: (be[j], 0, 0)),
                  pl.BlockSpec((1, 1, 2 * d_ff), lambda j, be, vl: (be[j], 0, 0)),
                  pl.BlockSpec((1, d_ff, d), lambda j, be, vl: (be[j], 0, 0)),
                  pl.BlockSpec((1, 1, d), lambda j, be, vl: (be[j], 0, 0))],
        out_specs=pl.BlockSpec((bm * rt, LANES), lambda j, be, vl: (j, 0)),
        scratch_shapes=[pltpu.VMEM((2, bm * rt, LANES), I32), pltpu.VMEM((2, MOE_SUB, d), BF16),
                        pltpu.VMEM((d, 2 * d_ff), BF16), pltpu.VMEM((d_ff, d), BF16),
                        pltpu.SemaphoreType.DMA((2,))])
    return pl.pallas_call(
        functools.partial(_experts_body, n_blk=n_blk, n_seg=n_seg),
        out_shape=jax.ShapeDtypeStruct((n_blk * bm * rt, LANES), I32),
        grid_spec=grid_spec,
        compiler_params=_cparams(("arbitrary",), VMEM_LIMIT_EXPERTS),
        name="experts",
    )(blk_expert, blk_valid, seg, seg, hs, wgu, bgu, wd, bd)


def _combine_body(seg0_ref, segn_ref, x1_ref, pg_ref, gf_ref, gfin_ref, ys_hbm, o_ref,
                  ybuf, sem, *, n_steps):
    tm = SORT_TM
    tiles = range(SORT_TILES_PER_STEP)
    n_pos = TOP_K * tm
    rt = ROW_WORDS
    i = pl.program_id(0)
    slot = i % 2
    other = 1 - slot

    def issue(seg_ref, s):
        for ti in tiles:
            _segment_copies(seg_ref, N_EXPERTS, 0, N_EXPERTS, ys_hbm, ybuf.at[s, ti], sem.at[s], tbl=ti)

    @pl.when(i == 0)
    def _():
        issue(seg0_ref, 0)

    for ti in tiles:
        pltpu.make_async_copy(ys_hbm.at[pl.ds(0, n_pos * rt), :], ybuf.at[slot, ti], sem.at[slot]).wait()

    @pl.when(i + 1 < n_steps)
    def _():
        issue(segn_ref, other)

    lane_pos = lax.broadcasted_iota(I32, (tm, n_pos), 1)
    ws = []
    for ti in tiles:
        pg = pg_ref[ti * tm:(ti + 1) * tm, :]
        w = jnp.zeros((tm, n_pos), F32)
        for k in range(TOP_K):
            w = jnp.where(lane_pos == pg[:, k:k + 1].astype(I32), pg[:, TOP_K + k:TOP_K + k + 1], w)
        ws.append(w)
    w_his = [w.astype(BF16) for w in ws]
    w_los = [(ws[ti] - w_his[ti].astype(F32)).astype(BF16) for ti in tiles]
    ysbs = [_words_to_rows([ybuf[slot, ti, pl.ds(c, n_pos, stride=rt), :] for c in range(rt)])
            for ti in tiles]
    ys = [jnp.dot(w_his[ti], ysbs[ti], preferred_element_type=F32)
          + jnp.dot(w_los[ti], ysbs[ti], preferred_element_type=F32) for ti in tiles]
    for ti in tiles:
        x2 = x1_ref[ti * tm:(ti + 1) * tm, :] + gf_ref[0] * ys[ti]
        o_ref[ti * tm:(ti + 1) * tm, :] = _rms(x2) * gfin_ref[...]


def _combine(seg, x1, pg, gate_f, g_final, ys, seq):
    t, d = x1.shape
    tm = SORT_TM
    ts = SORT_TILES_PER_STEP
    tpb = seq // (ts * tm)
    n_steps = t // (ts * tm)
    n_pos = TOP_K * tm
    seg_w = seg.shape[2]
    smem = functools.partial(pl.BlockSpec, (ts, 1, seg_w), memory_space=pltpu.SMEM)
    return pl.pallas_call(
        functools.partial(_combine_body, n_steps=n_steps),
        out_shape=jax.ShapeDtypeStruct((t, d), F32),
        grid=(n_steps,),
        in_specs=[smem(lambda i: (i, 0, 0)),
                  smem(lambda i: (jnp.minimum(i + 1, n_steps - 1), 0, 0)),
                  pl.BlockSpec((ts * tm, d), lambda i: (i, 0)),
                  pl.BlockSpec((ts * tm, LANES), lambda i: (i, 0)),
                  pl.BlockSpec((1, 1, d), lambda i: (i // tpb, 0, 0)),
                  pl.BlockSpec((1, d), lambda i: (0, 0)),
                  pl.BlockSpec(memory_space=pl.ANY)],
        out_specs=pl.BlockSpec((ts * tm, d), lambda i: (i, 0)),
        scratch_shapes=[pltpu.VMEM((2, ts, n_pos * ROW_WORDS, LANES), I32), pltpu.SemaphoreType.DMA((2,))],
        compiler_params=_cparams(("arbitrary",)),
        name="combine",
    )(seg, seg, x1, pg, gate_f, g_final, ys)


def _rope_tables(seq):
    half = ROT_DIMS // 2
    pos = jnp.arange(seq, dtype=F32)
    inv_freq = ROPE_THETA ** (-jnp.arange(0, ROT_DIMS, 2, dtype=F32) / ROT_DIMS)
    ang = pos[:, None] * inv_freq[None, :]
    cos, sin = jnp.cos(ang), jnp.sin(ang)
    ones = jnp.ones((seq, HEAD_DIM - ROT_DIMS), F32)
    zeros = jnp.zeros((seq, HEAD_DIM - ROT_DIMS), F32)
    zh = jnp.zeros((seq, half), F32)
    rep = LANES // HEAD_DIM
    cos_t = jnp.tile(jnp.concatenate([cos, cos, ones], axis=1), (1, rep))
    sa_t = jnp.tile(jnp.concatenate([-sin, zh, zeros], axis=1), (1, rep))
    sb_t = jnp.tile(jnp.concatenate([zh, sin, zeros], axis=1), (1, rep))
    return cos_t, sa_t, sb_t


def _arrange_in_proj(w_in):
    scale = HEAD_DIM ** -0.5 * LOG2E
    o = 0
    wqa = w_in[:, o:o + A_Q_W] * scale; o += A_Q_W
    wka = w_in[:, o:o + A_KV_W]; o += A_KV_W
    wva = w_in[:, o:o + A_KV_W]; o += A_KV_W
    wqb = w_in[:, o:o + B_W] * scale; o += B_W
    wkb = w_in[:, o:o + B_W]; o += B_W
    wvb = w_in[:, o:o + B_W]
    return jnp.concatenate([wqa, wqb, wkb, wvb, wka, wva], axis=1).astype(BF16)


def _round_up(n, m):
    return (n + m - 1) // m * m


def _moe_tables(tile_cnt, tile_off, n_blk):
    bm = MOE_BM
    nt = tile_cnt.shape[0]
    n_pos = TOP_K * SORT_TM
    cum = jnp.cumsum(tile_cnt, axis=0) - tile_cnt
    counts = jnp.sum(tile_cnt, axis=0)
    padded = ((counts + bm - 1) // bm) * bm
    pends = jnp.cumsum(padded)
    pstarts = pends - padded
    blk_first = jnp.arange(n_blk, dtype=I32) * bm
    blk_expert = jnp.minimum(
        jnp.sum((pends[None, :] <= blk_first[:, None]).astype(I32), axis=1), N_EXPERTS - 1)
    own = blk_expert[:, None] == jnp.arange(N_EXPERTS, dtype=I32)[None, :]

    def of_block_expert(per_expert):
        return jnp.sum(jnp.where(own[:, None, :], per_expert[None], 0), axis=-1)

    r0 = blk_first - of_block_expert(pstarts[None, :])[:, 0]
    blk_valid = jnp.clip(of_block_expert(counts[None, :])[:, 0] - r0, 0, bm)
    cum_b = of_block_expert(cum)
    cnt_b = of_block_expert(tile_cnt)
    off_b = of_block_expert(tile_off)
    lo = jnp.maximum(r0[:, None], cum_b)
    hi = jnp.minimum(r0[:, None] + bm, cum_b + cnt_b)
    size = jnp.maximum(hi - lo, 0)
    tile_ids = jnp.arange(nt, dtype=I32)[None, :]
    src = tile_ids * n_pos + off_b + (lo - cum_b)
    dst = jnp.where(size > 0, lo - r0[:, None], 0)
    first = jnp.min(jnp.where(size > 0, tile_ids, nt), axis=1)
    last = jnp.max(jnp.where(size > 0, tile_ids + 1, 0), axis=1)
    first = jnp.minimum(first, last)
    seg_w = _round_up(3 * nt + 2, LANES)
    exp_seg = jnp.concatenate(
        [src * ROW_WORDS, dst * ROW_WORDS, size * ROW_WORDS, first[:, None], last[:, None],
         jnp.zeros((n_blk, seg_w - 3 * nt - 2), I32)], axis=1).reshape(n_blk, 1, seg_w)
    comb_seg = jnp.concatenate(
        [(pstarts[None, :] + cum) * ROW_WORDS, tile_off * ROW_WORDS, tile_cnt * ROW_WORDS,
         jnp.zeros((nt, LANES - 3 * N_EXPERTS), I32)], axis=1).reshape(nt, 1, LANES)
    return blk_expert, blk_valid, exp_seg, comb_seg


def kernel(x, c, w_ada, b_ada, w_in, sink, rpb, g_out_a, g_out_b, w_out, w_router, b_router,
           w_gate_up, b_gate_up, w_down, b_down, g_final):
    bsz, seq, d = x.shape
    t = bsz * seq
    depth = w_ada.shape[0]
    assert depth == 1, "the final norm is fused into the layer combine"
    bm = MOE_BM
    n_blk = (t * TOP_K) // bm + N_EXPERTS
    nt = t // SORT_TM

    cos_t, sa_t, sb_t = _rope_tables(seq)
    x2 = x.reshape(t, d)
    for l in range(depth):
        mod = _ada(c, w_ada[l], b_ada[l][None, :])
        shift_m, scale_m, gate_m, shift_f, scale_f, gate_f = [
            m.reshape(bsz, 1, d) for m in jnp.split(mod, 6, axis=-1)]

        proj = _inproj(x2, shift_m, scale_m, _arrange_in_proj(w_in[l]), cos_t, sa_t, sb_t, seq)
        mix_a = _swa(proj, sink[l], g_out_a[l][None, :], bsz, seq)
        mix_b = _natten(proj, _na_bias_table(rpb[l], seq // GRID_W), g_out_b[l][None, :], bsz, seq)

        w_r = jnp.pad(w_router[l], ((0, 0), (0, LANES - N_EXPERTS)))
        b_r = jnp.pad(b_router[l], (0, LANES - N_EXPERTS), constant_values=NEG)[None, :]
        x1, hs, pg, tile_cnt, tile_off = _outproj(
            x2, mix_a, mix_b, w_out[l].astype(BF16), gate_m, shift_f, scale_f, w_r, b_r, seq)

        blk_expert, blk_valid, exp_seg, comb_seg = _moe_tables(
            tile_cnt[:, 0, :N_EXPERTS].astype(I32), tile_off[:, 0, :N_EXPERTS].astype(I32), n_blk)
        ys = _experts(blk_expert, blk_valid, exp_seg, hs,
                      w_gate_up[l], b_gate_up[l][:, None, :],
                      w_down[l], b_down[l][:, None, :], nt)
        x2 = _combine(comb_seg, x1, pg, gate_f, g_final[None, :], ys, seq)
    return x2.reshape(bsz, seq, d)
```

```python
import functools

import jax
import jax.numpy as jnp
from jax import lax
from jax.experimental import pallas as pl
from jax.experimental.pallas import tpu as pltpu

F32 = jnp.float32
BF16 = jnp.bfloat16
I32 = jnp.int32

HEAD_DIM = 64
A_Q_HEADS = 8
A_KV_HEADS = 2
A_GROUP = A_Q_HEADS // A_KV_HEADS
B_HEADS = 8
A_Q_W = A_Q_HEADS * HEAD_DIM
A_KV_W = A_KV_HEADS * HEAD_DIM
B_W = B_HEADS * HEAD_DIM
WINDOW = 128
ROT_DIMS = HEAD_DIM // 4
ROPE_THETA = 500000.0
GRID_W = 64
NA_ROWS_MAX = 8
NA_COLS = 16
N_EXPERTS = 32
TOP_K = 4
SWIGLU_LIMIT = 7.0
SWIGLU_ALPHA = 1.702
EPS = 1e-5
NEG = -1e30
LOG2E = 1.4426950408889634

LANES = 128
ROW_WORDS = 4
PROJ_BLK = 512
KV_A_COL_BLOCK = (A_Q_W + 3 * B_W) // (2 * A_KV_W)
TM_PROJ = 512
SORT_TM = 256
SORT_TILES_PER_STEP = 4
MOE_BM = 1024
MOE_SUB = 512
SWA_BLOCKS_PER_STEP = 2
SWA_STAGED_BLOCKS = 1
NA_ROWS_PER_STEP = 8
NA_ROW_UNROLL = 4
VMEM_LIMIT = 48 << 20
VMEM_LIMIT_EXPERTS = 58 << 20


def _cparams(sem, limit=VMEM_LIMIT):
    return pltpu.CompilerParams(dimension_semantics=sem, vmem_limit_bytes=limit)


def _rms(x):
    return x * lax.rsqrt(jnp.mean(x * x, axis=-1, keepdims=True) + EPS)


def _pack_bf16_pair(lo, hi):
    return pltpu.pack_elementwise([lo, hi], packed_dtype=BF16)


def _unpack_bf16_pair(words):
    return (pltpu.unpack_elementwise(words, index=0, packed_dtype=BF16, unpacked_dtype=F32),
            pltpu.unpack_elementwise(words, index=1, packed_dtype=BF16, unpacked_dtype=F32))


def _rows_to_words(x):
    return [_pack_bf16_pair(x[:, 2 * c * LANES:(2 * c + 1) * LANES], x[:, (2 * c + 1) * LANES:(2 * c + 2) * LANES])
            for c in range(ROW_WORDS)]


def _words_to_rows(words):
    parts = []
    for w in words:
        parts.extend(_unpack_bf16_pair(w))
    return jnp.concatenate(parts, axis=1).astype(BF16)


def _ada_body(c_ref, w_ref, b_ref, o_ref):
    c = c_ref[...]
    ca = c / (1.0 + jnp.exp(-c))
    o_ref[...] = jnp.dot(ca, w_ref[...], preferred_element_type=F32,
                         precision=lax.Precision.HIGHEST) + b_ref[...]


def _ada(c, w, b):
    bsz, d = c.shape
    n = w.shape[1]
    bn = n // 4
    return pl.pallas_call(
        _ada_body,
        out_shape=jax.ShapeDtypeStruct((bsz, n), F32),
        grid=(4,),
        in_specs=[pl.BlockSpec((bsz, d), lambda i: (0, 0)),
                  pl.BlockSpec((d, bn), lambda i: (0, i)),
                  pl.BlockSpec((1, bn), lambda i: (0, i))],
        out_specs=pl.BlockSpec((bsz, bn), lambda i: (0, i)),
        compiler_params=_cparams(("arbitrary",)),
        name="ada",
    )(c, w, b)


def _inproj_body(x_ref, sh_ref, sc_ref, w_ref, cos_ref, sa_ref, sb_ref, o_ref):
    h = (_rms(x_ref[...]) * (1.0 + sc_ref[0]) + sh_ref[0]).astype(BF16)
    cos = cos_ref[...]
    sa = sa_ref[...]
    sb = sb_ref[...]
    blocks = [(0, A_Q_W, A_Q_W // LANES)]
    blocks += [(A_Q_W + i * B_W, B_W, 0) for i in range(3)]
    blocks += [(A_Q_W + 3 * B_W, 2 * A_KV_W, A_KV_W // LANES)]
    for c0, width, n_rot in blocks:
        p = jnp.dot(h, w_ref[:, c0:c0 + width], preferred_element_type=F32)
        if n_rot:
            parts = []
            for j in range(width // LANES):
                pj = p[:, j * LANES:(j + 1) * LANES]
                if j < n_rot:
                    pj = (pj * cos + pltpu.roll(pj, LANES - ROT_DIMS // 2, axis=1) * sa
                          + pltpu.roll(pj, ROT_DIMS // 2, axis=1) * sb)
                parts.append(pj)
            p = jnp.concatenate(parts, axis=1)
        o_ref[:, c0:c0 + width] = p.astype(BF16)


def _inproj(x2, shift, scale, w_ext, cos_t, sa_t, sb_t, seq):
    t, d = x2.shape
    n = w_ext.shape[1]
    tm = TM_PROJ
    tpb = seq // tm
    return pl.pallas_call(
        _inproj_body,
        out_shape=jax.ShapeDtypeStruct((t, n), BF16),
        grid=(t // tm,),
        in_specs=[pl.BlockSpec((tm, d), lambda i: (i, 0)),
                  pl.BlockSpec((1, 1, d), lambda i: (i // tpb, 0, 0)),
                  pl.BlockSpec((1, 1, d), lambda i: (i // tpb, 0, 0)),
                  pl.BlockSpec((d, n), lambda i: (0, 0)),
                  pl.BlockSpec((tm, LANES), lambda i: (i % tpb, 0)),
                  pl.BlockSpec((tm, LANES), lambda i: (i % tpb, 0)),
                  pl.BlockSpec((tm, LANES), lambda i: (i % tpb, 0))],
        out_specs=pl.BlockSpec((tm, n), lambda i: (i, 0)),
        compiler_params=_cparams(("arbitrary",)),
        name="inproj",
    )(x2, shift, scale, w_ext, cos_t, sa_t, sb_t)


def _swa_body(sink_ref, q_ref, kv_ref, g_ref, o_ref, *, seq):
    step = pl.program_id(1)
    blk = WINDOW
    kw_len = 3 * blk
    gw = A_GROUP * HEAD_DIM
    srows = A_GROUP * blk
    row = lax.broadcasted_iota(I32, (srows, 1), 0)
    grp = lax.broadcasted_iota(I32, (1, gw), 1) // HEAD_DIM
    blocks = range(SWA_BLOCKS_PER_STEP)
    def spread(tile):
        words = pltpu.bitcast(tile, jnp.uint32)
        swapped = pltpu.roll(words, HEAD_DIM, axis=1)
        low = lax.broadcasted_iota(I32, words.shape, 1) < HEAD_DIM
        out = []
        for pair in (jnp.where(low, words, swapped), jnp.where(low, swapped, words)):
            pair = pltpu.bitcast(pair, BF16)
            out.append(jnp.concatenate([pair] * (gw // LANES), axis=1))
        return out

    kws, vws, alloweds = [], [], []
    for bi in blocks:
        n = step * SWA_BLOCKS_PER_STEP + bi
        start = pl.multiple_of(jnp.clip(n * blk - blk, 0, seq - kw_len), blk)
        kv = kv_ref[pl.ds(start, kw_len), :]
        kws.append(jnp.concatenate(spread(kv[:, :A_KV_W]), axis=1))
        vws.append(jnp.concatenate(spread(kv[:, A_KV_W:]), axis=1))
        qpos = n * blk + row % blk
        kpos = start + lax.broadcasted_iota(I32, (1, kw_len), 1)
        alloweds.append(jnp.abs(qpos - kpos) <= WINDOW)
    sks = []
    for h in range(A_KV_HEADS):
        sk = jnp.full((srows, 1), sink_ref[h * A_GROUP], F32)
        for g in range(1, A_GROUP):
            sk = jnp.where(row // blk == g, sink_ref[h * A_GROUP + g], sk)
        sks.append(sk * LOG2E)
    for b0 in range(0, SWA_BLOCKS_PER_STEP, SWA_STAGED_BLOCKS):
        group = range(b0, b0 + SWA_STAGED_BLOCKS)
        units = [(bi, h) for bi in group for h in range(A_KV_HEADS)]
        qms = []
        for bi, h in units:
            qh = q_ref[bi * blk:(bi + 1) * blk, h * gw:(h + 1) * gw]
            qms.append(jnp.concatenate(
                [jnp.where(grp == g, qh, jnp.zeros_like(qh)) for g in range(A_GROUP)], axis=0))
        ss = [lax.dot_general(qms[u], kws[bi][:, h * gw:(h + 1) * gw], (((1,), (1,)), ((), ())),
                              preferred_element_type=F32) for u, (bi, h) in enumerate(units)]
        ss = [jnp.where(alloweds[bi], ss[u], NEG) for u, (bi, h) in enumerate(units)]
        ms = [jnp.maximum(jnp.max(ss[u], axis=-1, keepdims=True), sks[h]) for u, (bi, h) in enumerate(units)]
        ps = [jnp.exp2(ss[u] - ms[u]) for u in range(len(units))]
        invs = [1.0 / (jnp.sum(ps[u], axis=-1, keepdims=True) + jnp.exp2(sks[h] - ms[u]))
                for u, (bi, h) in enumerate(units)]
        ress = [jnp.dot(ps[u].astype(BF16), vws[bi][:, h * gw:(h + 1) * gw], preferred_element_type=F32) * invs[u]
                for u, (bi, h) in enumerate(units)]
        accs = []
        for res in ress:
            acc = res[0:blk]
            for g in range(1, A_GROUP):
                acc = jnp.where(grp == g, res[g * blk:(g + 1) * blk], acc)
            accs.append(acc)
        for gi, bi in enumerate(group):
            oa = jnp.concatenate(accs[gi * A_KV_HEADS:(gi + 1) * A_KV_HEADS], axis=1)
            o_ref[bi * blk:(bi + 1) * blk, :] = (_rms(oa) * g_ref[...]).astype(BF16)


def _swa(proj, sink, g_a, bsz, seq):
    t = proj.shape[0]
    blk = WINDOW * SWA_BLOCKS_PER_STEP
    nq = seq // blk
    grid_spec = pltpu.PrefetchScalarGridSpec(
        num_scalar_prefetch=1, grid=(bsz, nq),
        in_specs=[pl.BlockSpec((blk, PROJ_BLK), lambda b, n, s: (b * nq + n, 0)),
                  pl.BlockSpec((seq, 2 * A_KV_W), lambda b, n, s: (b, KV_A_COL_BLOCK)),
                  pl.BlockSpec((1, A_Q_W), lambda b, n, s: (0, 0))],
        out_specs=pl.BlockSpec((blk, A_Q_W), lambda b, n, s: (b * nq + n, 0)))
    return pl.pallas_call(
        functools.partial(_swa_body, seq=seq),
        out_shape=jax.ShapeDtypeStruct((t, A_Q_W), BF16),
        grid_spec=grid_spec,
        compiler_params=_cparams(("arbitrary", "arbitrary")),
        name="swa",
    )(sink, proj, proj, g_a)


def _na_body(q_ref, k_ref, v_ref, bias_ref, g_ref, o_ref, *, rows, kr, rb):
    blk = pl.program_id(1)
    per = LANES // HEAD_DIM
    n_pair = B_W // LANES
    half = lax.broadcasted_iota(I32, (1, LANES), 1) // HEAD_DIM
    gain = g_ref[...]

    def one_row(i, carry):
        r = blk * rb + i
        rs = jnp.clip(r - kr // 2, 0, rows - kr)
        start = pl.multiple_of(rs * GRID_W, GRID_W)
        kw = k_ref[pl.ds(start, kr * GRID_W), :]
        vw = v_ref[pl.ds(start, kr * GRID_W), :]
        q = q_ref[pl.ds(pl.multiple_of(i * GRID_W, GRID_W), GRID_W), :]
        s_parts = []
        for pr in range(n_pair):
            qp = q[:, pr * LANES:(pr + 1) * LANES]
            qm = jnp.concatenate([jnp.where(half == j, qp, jnp.zeros_like(qp)) for j in range(per)], axis=0)
            s_parts.append(lax.dot_general(qm, kw[:, pr * LANES:(pr + 1) * LANES],
                                           (((1,), (1,)), ((), ())), preferred_element_type=F32))
        s = jnp.concatenate(s_parts, axis=0) + bias_ref[r - rs]
        m = jnp.max(s, axis=-1, keepdims=True)
        p = jnp.exp2(s - m)
        inv = 1.0 / jnp.sum(p, axis=-1, keepdims=True)
        pn = p.astype(BF16)
        o_parts = []
        for pr in range(n_pair):
            srows = slice(pr * per * GRID_W, (pr + 1) * per * GRID_W)
            res = jnp.dot(pn[srows], vw[:, pr * LANES:(pr + 1) * LANES], preferred_element_type=F32) * inv[srows]
            op = res[0:GRID_W]
            for j in range(1, per):
                op = jnp.where(half == j, res[j * GRID_W:(j + 1) * GRID_W], op)
            o_parts.append(op)
        ob = jnp.concatenate(o_parts, axis=1)
        o_ref[pl.ds(pl.multiple_of(i * GRID_W, GRID_W), GRID_W), :] = (_rms(ob) * gain).astype(BF16)
        return carry

    def row_group(ig, carry):
        for u in range(NA_ROW_UNROLL):
            one_row(NA_ROW_UNROLL * ig + u, carry)
        return carry

    lax.fori_loop(0, rb // NA_ROW_UNROLL, row_group, 0)


def _natten(proj, bias_t, g_b, bsz, seq):
    t = proj.shape[0]
    rows = seq // GRID_W
    kr = min(NA_ROWS_MAX, rows)
    rb = NA_ROWS_PER_STEP
    nb = rows // rb
    return pl.pallas_call(
        functools.partial(_na_body, rows=rows, kr=kr, rb=rb),
        out_shape=jax.ShapeDtypeStruct((t, B_W), BF16),
        grid=(bsz, nb),
        in_specs=[pl.BlockSpec((rb * GRID_W, PROJ_BLK), lambda b, r: (b * nb + r, 1)),
                  pl.BlockSpec((seq, PROJ_BLK), lambda b, r: (b, 2)),
                  pl.BlockSpec((seq, PROJ_BLK), lambda b, r: (b, 3)),
                  pl.BlockSpec((kr, B_HEADS * GRID_W, kr * GRID_W), lambda b, r: (0, 0, 0)),
                  pl.BlockSpec((1, B_W), lambda b, r: (0, 0))],
        out_specs=pl.BlockSpec((rb * GRID_W, B_W), lambda b, r: (b * nb + r, 0)),
        compiler_params=_cparams(("arbitrary", "arbitrary")),
        name="natten",
    )(proj, proj, proj, bias_t, g_b)


def _na_bias_table(rpb, rows):
    kr = min(NA_ROWS_MAX, rows)
    kc = NA_COLS
    col = jnp.arange(GRID_W)
    col_start = jnp.clip(col - kc // 2, 0, GRID_W - kc)
    col_mask = (col[None, :] >= col_start[:, None]) & (col[None, :] < col_start[:, None] + kc)
    col_off = jnp.clip(col[None, :] - col[:, None], -(kc - 1), kc - 1) + (NA_COLS - 1)
    rows_sel = jnp.stack([rpb[:, NA_ROWS_MAX - 1 - cs:NA_ROWS_MAX - 1 - cs + kr, :]
                          for cs in range(kr)], axis=0)
    pick = (col_off[None, :, :] == jnp.arange(2 * kc - 1)[:, None, None]).astype(F32)
    tbl = jnp.einsum('chjr,rqk->chqjk', rows_sel, pick, precision=lax.Precision.HIGHEST)
    tbl = jnp.where(col_mask[None, None, :, None, :], tbl * LOG2E, NEG)
    return tbl.reshape(kr, rpb.shape[0] * GRID_W, kr * GRID_W).astype(F32)


def _outproj_body(x_ref, ma_ref, mb_ref, wo_ref, gm_ref, sf_ref, cf_ref, wr_ref, br_ref,
                  x1_ref, hs_ref, pg_ref, tc_ref, to_ref):
    tm = SORT_TM
    tiles = range(SORT_TILES_PER_STEP)
    tt = tm * SORT_TILES_PER_STEP
    n_pos = TOP_K * tm
    mix = (jnp.dot(ma_ref[...], wo_ref[:A_Q_W, :], preferred_element_type=F32)
           + jnp.dot(mb_ref[...], wo_ref[A_Q_W:, :], preferred_element_type=F32))
    x1 = x_ref[...] + gm_ref[0] * mix
    x1_ref[...] = x1
    h = _rms(x1) * (1.0 + cf_ref[0]) + sf_ref[0]
    h_hi = h.astype(BF16)
    h_lo = (h - h_hi.astype(F32)).astype(BF16)
    w_r = wr_ref[...]
    w_hi = w_r.astype(BF16)
    w_lo = (w_r - w_hi.astype(F32)).astype(BF16)
    l_hi = jnp.dot(h_hi, jnp.concatenate([w_hi, w_lo], axis=1), preferred_element_type=F32)
    logits = (l_hi[:, :LANES] + l_hi[:, LANES:]
              + jnp.dot(h_lo, w_hi, preferred_element_type=F32) + br_ref[...])
    lane = lax.broadcasted_iota(I32, (tt, LANES), 1)
    work = logits
    vals, idxs = [], []
    for _ in range(TOP_K):
        m = jnp.max(work, axis=-1, keepdims=True)
        ik = jnp.min(jnp.where(work == m, lane, LANES), axis=-1, keepdims=True)
        vals.append(m)
        idxs.append(ik)
        work = jnp.where(lane == ik, -jnp.inf, work)
    es = [jnp.exp(v - vals[0]) for v in vals]
    den = es[0] + es[1] + es[2] + es[3]
    mh = jnp.zeros((tt, LANES), F32)
    for ik in idxs:
        mh = mh + (lane == ik).astype(F32)
    mhb = mh.astype(BF16)
    tri = (lax.broadcasted_iota(I32, (tm, tm), 0) > lax.broadcasted_iota(I32, (tm, tm), 1)).astype(BF16)
    lower = (lax.broadcasted_iota(I32, (LANES, LANES), 0)
             < lax.broadcasted_iota(I32, (LANES, LANES), 1)).astype(BF16)
    mh_t = [mhb[ti * tm:(ti + 1) * tm] for ti in tiles]
    earlier = [jnp.dot(tri, mh_t[ti], preferred_element_type=F32) for ti in tiles]
    below = [jnp.dot(mh_t[ti], lower, preferred_element_type=F32) for ti in tiles]
    toff = [jnp.sum(below[ti], axis=0, keepdims=True) for ti in tiles]
    cnt = [jnp.sum(mh[ti * tm:(ti + 1) * tm], axis=0, keepdims=True) for ti in tiles]
    posf = jnp.concatenate([earlier[ti] + toff[ti] for ti in tiles], axis=0)
    pg = jnp.zeros((tt, LANES), F32)
    for k in range(TOP_K):
        pos_k = jnp.sum(jnp.where(lane == idxs[k], posf, 0.0), axis=-1, keepdims=True)
        pg = jnp.where(lane == k, pos_k, pg)
        pg = jnp.where(lane == TOP_K + k, es[k] / den, pg)
    pg_ref[...] = pg
    pgt_all = pg.T[:ROW_WORDS]
    pgt = [pgt_all[:, ti * tm:(ti + 1) * tm] for ti in tiles]
    for ti in tiles:
        tc_ref[ti] = cnt[ti]
        to_ref[ti] = toff[ti]
    pos_iota = lax.broadcasted_iota(I32, (n_pos, tm), 0)
    hits = []
    for ti in tiles:
        hit = pos_iota == pgt[ti][0:1, :].astype(I32)
        for k in range(1, TOP_K):
            hit = hit | (pos_iota == pgt[ti][k:k + 1, :].astype(I32))
        hits.append(jnp.where(hit, 1.0, 0.0).astype(BF16))
    srt = [jnp.dot(hits[ti], h_hi[ti * tm:(ti + 1) * tm], preferred_element_type=F32) for ti in tiles]
    for ti in tiles:
        for c, words in enumerate(_rows_to_words(srt[ti])):
            hs_ref[pl.ds(ti * n_pos * ROW_WORDS + c, n_pos, stride=ROW_WORDS), :] = words


def _outproj(x2, mix_a, mix_b, w_out, gate_m, shift_f, scale_f, w_r, b_r, seq):
    t, d = x2.shape
    tm = SORT_TM
    nt = t // tm
    ts = SORT_TILES_PER_STEP
    tpb = seq // (ts * tm)
    n_pos = TOP_K * tm
    mod_spec = pl.BlockSpec((1, 1, d), lambda i: (i // tpb, 0, 0))
    return pl.pallas_call(
        _outproj_body,
        out_shape=(jax.ShapeDtypeStruct((t, d), F32),
                   jax.ShapeDtypeStruct((t * TOP_K * ROW_WORDS, LANES), I32),
                   jax.ShapeDtypeStruct((t, LANES), F32),
                   jax.ShapeDtypeStruct((nt, 1, LANES), F32),
                   jax.ShapeDtypeStruct((nt, 1, LANES), F32)),
        grid=(nt // ts,),
        in_specs=[pl.BlockSpec((ts * tm, d), lambda i: (i, 0)),
                  pl.BlockSpec((ts * tm, A_Q_W), lambda i: (i, 0)),
                  pl.BlockSpec((ts * tm, B_W), lambda i: (i, 0)),
                  pl.BlockSpec((A_Q_W + B_W, d), lambda i: (0, 0)),
                  mod_spec, mod_spec, mod_spec,
                  pl.BlockSpec((d, LANES), lambda i: (0, 0)),
                  pl.BlockSpec((1, LANES), lambda i: (0, 0))],
        out_specs=(pl.BlockSpec((ts * tm, d), lambda i: (i, 0)),
                   pl.BlockSpec((ts * n_pos * ROW_WORDS, LANES), lambda i: (i, 0)),
                   pl.BlockSpec((ts * tm, LANES), lambda i: (i, 0)),
                   pl.BlockSpec((ts, 1, LANES), lambda i: (i, 0, 0)),
                   pl.BlockSpec((ts, 1, LANES), lambda i: (i, 0, 0))),
        compiler_params=_cparams(("arbitrary",)),
        name="outproj",
    )(x2, mix_a, mix_b, w_out, gate_m, shift_f, scale_f, w_r, b_r)


def _segment_copies(seg_ref, n_seg, first, last, src_hbm, dst_buf, sem, tbl=0):
    def body(i, carry):
        size = pl.multiple_of(seg_ref[tbl, 0, 2 * n_seg + i], ROW_WORDS)

        @pl.when(size > 0)
        def _():
            src = pl.multiple_of(seg_ref[tbl, 0, i], ROW_WORDS)
            dst = pl.multiple_of(seg_ref[tbl, 0, n_seg + i], ROW_WORDS)
            pltpu.make_async_copy(src_hbm.at[pl.ds(src, size), :], dst_buf.at[pl.ds(dst, size), :], sem).start()
        return carry

    lax.fori_loop(first, last, body, 0)


def _experts_body(be_ref, vl_ref, seg0_ref, segn_ref, hs_hbm, wgu32_ref, bgu_ref, wd32_ref, bd_ref, o_ref,
                  xbuf, xs_ref, wgu_ref, wd_ref, gsem, *, n_blk, n_seg):
    bm = MOE_BM
    rt = ROW_WORDS
    d_ff = wd_ref.shape[0]
    j = pl.program_id(0)
    slot = j % 2
    other = 1 - slot
    valid = vl_ref[j]

    def issue(seg_ref, s):
        _segment_copies(seg_ref, n_seg, seg_ref[0, 0, 3 * n_seg], seg_ref[0, 0, 3 * n_seg + 1],
                        hs_hbm, xbuf.at[s], gsem.at[s])

    @pl.when(j == 0)
    def _():
        xbuf[...] = jnp.zeros_like(xbuf)
        issue(seg0_ref, 0)

    @pl.when((j == 0) | (be_ref[j] != be_ref[jnp.maximum(j - 1, 0)]))
    def _():
        for c in range(d_ff // LANES):
            wgu_ref[:, 2 * c * LANES:(2 * c + 1) * LANES] = (
                wgu32_ref[0, :, c * LANES:(c + 1) * LANES].astype(BF16))
            wgu_ref[:, (2 * c + 1) * LANES:(2 * c + 2) * LANES] = (
                wgu32_ref[0, :, d_ff + c * LANES:d_ff + (c + 1) * LANES].astype(BF16))
        wd_ref[...] = wd32_ref[0].astype(BF16)

    @pl.when(valid > 0)
    def _():
        rows = pl.multiple_of(valid * rt, rt)
        pltpu.make_async_copy(hs_hbm.at[pl.ds(0, rows), :], xbuf.at[slot, pl.ds(0, rows), :],
                              gsem.at[slot]).wait()

    @pl.when(j + 1 < n_blk)
    def _():
        issue(segn_ref, other)

    sub = MOE_SUB
    n_chunk = d_ff // (2 * LANES)
    half = sub // 2
    second = sub * rt
    n_half = (valid + half - 1) // half

    def assemble(r0, n):
        return _words_to_rows([xbuf[slot, pl.ds(r0 + c, n, stride=rt), :] for c in range(rt)])

    def act_chunk(xs, c):
        gu = jnp.dot(xs, wgu_ref[:, 4 * c * LANES:(4 * c + 4) * LANES], preferred_element_type=F32)
        outs = []
        for t in range(2):
            lt = 2 * c + t
            gate = gu[:, 2 * t * LANES:(2 * t + 1) * LANES] + bgu_ref[0, :, lt * LANES:(lt + 1) * LANES]
            up = (gu[:, (2 * t + 1) * LANES:(2 * t + 2) * LANES]
                  + bgu_ref[0, :, d_ff + lt * LANES:d_ff + (lt + 1) * LANES])
            gate = jnp.minimum(gate, SWIGLU_LIMIT)
            up = jnp.clip(up, -SWIGLU_LIMIT, SWIGLU_LIMIT)
            glu = gate * (0.5 + 0.5 * jnp.tanh((0.5 * SWIGLU_ALPHA) * gate))
            outs.append(((up + 1.0) * glu).astype(BF16))
        return jnp.concatenate(outs, axis=1)

    def down(acts):
        return jnp.dot(jnp.concatenate(acts, axis=1), wd_ref[...], preferred_element_type=F32) + bd_ref[0]

    def store_rows(r0, ys):
        for c, words in enumerate(_rows_to_words(ys)):
            o_ref[pl.ds(r0 + c, ys.shape[0], stride=rt), :] = words

    def zero_rows(r0, n):
        o_ref[pl.ds(r0, n * rt), :] = jnp.zeros((n * rt, LANES), I32)

    def single_pass(n):
        xs_ref[0, :n, :] = assemble(0, n)
        xs = xs_ref[0, :n, :]
        store_rows(0, down([act_chunk(xs, c) for c in range(n_chunk)]))
        zero_rows(n * rt, bm - n)

    def double_pass(n2):
        xs_ref[0] = assemble(0, sub)
        xa = xs_ref[0]
        acts = []
        for c in range(n_chunk):
            acts.append(act_chunk(xa, c))
            if c == n_chunk // 2:
                xs_ref[1, :n2, :] = assemble(second, n2)
        ys_a = down(acts)
        xb = xs_ref[1, :n2, :]
        acts = []
        for c in range(n_chunk):
            acts.append(act_chunk(xb, c))
            if c == 1:
                store_rows(0, ys_a)
        store_rows(second, down(acts))
        if n2 < sub:
            zero_rows(second + n2 * rt, sub - n2)

    @pl.when(n_half == 0)
    def _():
        zero_rows(0, bm)

    for nh, run in ((1, lambda: single_pass(half)), (2, lambda: single_pass(sub)),
                    (3, lambda: double_pass(half)), (4, lambda: double_pass(sub))):
        pl.when(n_half == nh)(run)


def _experts(blk_expert, blk_valid, seg, hs, wgu, bgu, wd, bd, n_seg):
    bm = MOE_BM
    rt = ROW_WORDS
    n_blk, _, seg_w = seg.shape
    d = wd.shape[2]
    d_ff = wd.shape[1]
    assert d == 2 * rt * LANES and hs.shape[1] == LANES and bm == 2 * MOE_SUB
    smem = functools.partial(pl.BlockSpec, (1, 1, seg_w), memory_space=pltpu.SMEM)
    last = n_blk - 1
    grid_spec = pltpu.PrefetchScalarGridSpec(
        num_scalar_prefetch=2, grid=(n_blk,),
        in_specs=[smem(lambda j, be, vl: (j, 0, 0)),
                  smem(lambda j, be, vl: (jnp.minimum(j + 1, last), 0, 0)),
                  pl.BlockSpec(memory_space=pl.ANY),
                  pl.BlockSpec((1, d, 2 * d_ff), lambda j, be, vl: (be[j], 0, 0)),
                  pl.BlockSpec((1, 1, 2 * d_ff), lambda j, be, vl: (be[j], 0, 0)),
                  pl.BlockSpec((1, d_ff, d), lambda j, be, vl: (be[j], 0, 0)),
                  pl.BlockSpec((1, 1, d), lambda j, be, vl: (be[j], 0, 0))],
        out_specs=pl.BlockSpec((bm * rt, LANES), lambda j, be, vl: (j, 0)),
        scratch_shapes=[pltpu.VMEM((2, bm * rt, LANES), I32), pltpu.VMEM((2, MOE_SUB, d), BF16),
                        pltpu.VMEM((d, 2 * d_ff), BF16), pltpu.VMEM((d_ff, d), BF16),
                        pltpu.SemaphoreType.DMA((2,))])
    return pl.pallas_call(
        functools.partial(_experts_body, n_blk=n_blk, n_seg=n_seg),
        out_shape=jax.ShapeDtypeStruct((n_blk * bm * rt, LANES), I32),
        grid_spec=grid_spec,
        compiler_params=_cparams(("arbitrary",), VMEM_LIMIT_EXPERTS),
        name="experts",
    )(blk_expert, blk_valid, seg, seg, hs, wgu, bgu, wd, bd)


def _combine_body(seg0_ref, segn_ref, x1_ref, pg_ref, gf_ref, gfin_ref, ys_hbm, o_ref,
                  ybuf, sem, *, n_steps):
    tm = SORT_TM
    tiles = range(SORT_TILES_PER_STEP)
    n_pos = TOP_K * tm
    rt = ROW_WORDS
    i = pl.program_id(0)
    slot = i % 2
    other = 1 - slot

    def issue(seg_ref, s):
        for ti in tiles:
            _segment_copies(seg_ref, N_EXPERTS, 0, N_EXPERTS, ys_hbm, ybuf.at[s, ti], sem.at[s], tbl=ti)

    @pl.when(i == 0)
    def _():
        issue(seg0_ref, 0)

    for ti in tiles:
        pltpu.make_async_copy(ys_hbm.at[pl.ds(0, n_pos * rt), :], ybuf.at[slot, ti], sem.at[slot]).wait()

    @pl.when(i + 1 < n_steps)
    def _():
        issue(segn_ref, other)

    lane_pos = lax.broadcasted_iota(I32, (tm, n_pos), 1)
    ws = []
    for ti in tiles:
        pg = pg_ref[ti * tm:(ti + 1) * tm, :]
        w = jnp.zeros((tm, n_pos), F32)
        for k in range(TOP_K):
            w = jnp.where(lane_pos == pg[:, k:k + 1].astype(I32), pg[:, TOP_K + k:TOP_K + k + 1], w)
        ws.append(w)
    w_his = [w.astype(BF16) for w in ws]
    w_los = [(ws[ti] - w_his[ti].astype(F32)).astype(BF16) for ti in tiles]
    ysbs = [_words_to_rows([ybuf[slot, ti, pl.ds(c, n_pos, stride=rt), :] for c in range(rt)])
            for ti in tiles]
    ys = [jnp.dot(w_his[ti], ysbs[ti], preferred_element_type=F32)
          + jnp.dot(w_los[ti], ysbs[ti], preferred_element_type=F32) for ti in tiles]
    for ti in tiles:
        x2 = x1_ref[ti * tm:(ti + 1) * tm, :] + gf_ref[0] * ys[ti]
        o_ref[ti * tm:(ti + 1) * tm, :] = _rms(x2) * gfin_ref[...]


def _combine(seg, x1, pg, gate_f, g_final, ys, seq):
    t, d = x1.shape
    tm = SORT_TM
    ts = SORT_TILES_PER_STEP
    tpb = seq // (ts * tm)
    n_steps = t // (ts * tm)
    n_pos = TOP_K * tm
    seg_w = seg.shape[2]
    smem = functools.partial(pl.BlockSpec, (ts, 1, seg_w), memory_space=pltpu.SMEM)
    return pl.pallas_call(
        functools.partial(_combine_body, n_steps=n_steps),
        out_shape=jax.ShapeDtypeStruct((t, d), F32),
        grid=(n_steps,),
        in_specs=[smem(lambda i: (i, 0, 0)),
                  smem(lambda i: (jnp.minimum(i + 1, n_steps - 1), 0, 0)),
                  pl.BlockSpec((ts * tm, d), lambda i: (i, 0)),
                  pl.BlockSpec((ts * tm, LANES), lambda i: (i, 0)),
                  pl.BlockSpec((1, 1, d), lambda i: (i // tpb, 0, 0)),
                  pl.BlockSpec((1, d), lambda i: (0, 0)),
                  pl.BlockSpec(memory_space=pl.ANY)],
        out_specs=pl.BlockSpec((ts * tm, d), lambda i: (i, 0)),
        scratch_shapes=[pltpu.VMEM((2, ts, n_pos * ROW_WORDS, LANES), I32), pltpu.SemaphoreType.DMA((2,))],
        compiler_params=_cparams(("arbitrary",)),
        name="combine",
    )(seg, seg, x1, pg, gate_f, g_final, ys)


def _rope_tables(seq):
    half = ROT_DIMS // 2
    pos = jnp.arange(seq, dtype=F32)
    inv_freq = ROPE_THETA ** (-jnp.arange(0, ROT_DIMS, 2, dtype=F32) / ROT_DIMS)
    ang = pos[:, None] * inv_freq[None, :]
    cos, sin = jnp.cos(ang), jnp.sin(ang)
    ones = jnp.ones((seq, HEAD_DIM - ROT_DIMS), F32)
    zeros = jnp.zeros((seq, HEAD_DIM - ROT_DIMS), F32)
    zh = jnp.zeros((seq, half), F32)
    rep = LANES // HEAD_DIM
    cos_t = jnp.tile(jnp.concatenate([cos, cos, ones], axis=1), (1, rep))
    sa_t = jnp.tile(jnp.concatenate([-sin, zh, zeros], axis=1), (1, rep))
    sb_t = jnp.tile(jnp.concatenate([zh, sin, zeros], axis=1), (1, rep))
    return cos_t, sa_t, sb_t


def _arrange_in_proj(w_in):
    scale = HEAD_DIM ** -0.5 * LOG2E
    o = 0
    wqa = w_in[:, o:o + A_Q_W] * scale; o += A_Q_W
    wka = w_in[:, o:o + A_KV_W]; o += A_KV_W
    wva = w_in[:, o:o + A_KV_W]; o += A_KV_W
    wqb = w_in[:, o:o + B_W] * scale; o += B_W
    wkb = w_in[:, o:o + B_W]; o += B_W
    wvb = w_in[:, o:o + B_W]
    return jnp.concatenate([wqa, wqb, wkb, wvb, wka, wva], axis=1).astype(BF16)


def _round_up(n, m):
    return (n + m - 1) // m * m


def _moe_tables(tile_cnt, tile_off, n_blk):
    bm = MOE_BM
    nt = tile_cnt.shape[0]
    n_pos = TOP_K * SORT_TM
    cum = jnp.cumsum(tile_cnt, axis=0) - tile_cnt
    counts = jnp.sum(tile_cnt, axis=0)
    padded = ((counts + bm - 1) // bm) * bm
    pends = jnp.cumsum(padded)
    pstarts = pends - padded
    blk_first = jnp.arange(n_blk, dtype=I32) * bm
    blk_expert = jnp.minimum(
        jnp.sum((pends[None, :] <= blk_first[:, None]).astype(I32), axis=1), N_EXPERTS - 1)
    own = blk_expert[:, None] == jnp.arange(N_EXPERTS, dtype=I32)[None, :]

    def of_block_expert(per_expert):
        return jnp.sum(jnp.where(own[:, None, :], per_expert[None], 0), axis=-1)

    r0 = blk_first - of_block_expert(pstarts[None, :])[:, 0]
    blk_valid = jnp.clip(of_block_expert(counts[None, :])[:, 0] - r0, 0, bm)
    cum_b = of_block_expert(cum)
    cnt_b = of_block_expert(tile_cnt)
    off_b = of_block_expert(tile_off)
    lo = jnp.maximum(r0[:, None], cum_b)
    hi = jnp.minimum(r0[:, None] + bm, cum_b + cnt_b)
    size = jnp.maximum(hi - lo, 0)
    tile_ids = jnp.arange(nt, dtype=I32)[None, :]
    src = tile_ids * n_pos + off_b + (lo - cum_b)
    dst = jnp.where(size > 0, lo - r0[:, None], 0)
    first = jnp.min(jnp.where(size > 0, tile_ids, nt), axis=1)
    last = jnp.max(jnp.where(size > 0, tile_ids + 1, 0), axis=1)
    first = jnp.minimum(first, last)
    seg_w = _round_up(3 * nt + 2, LANES)
    exp_seg = jnp.concatenate(
        [src * ROW_WORDS, dst * ROW_WORDS, size * ROW_WORDS, first[:, None], last[:, None],
         jnp.zeros((n_blk, seg_w - 3 * nt - 2), I32)], axis=1).reshape(n_blk, 1, seg_w)
    comb_seg = jnp.concatenate(
        [(pstarts[None, :] + cum) * ROW_WORDS, tile_off * ROW_WORDS, tile_cnt * ROW_WORDS,
         jnp.zeros((nt, LANES - 3 * N_EXPERTS), I32)], axis=1).reshape(nt, 1, LANES)
    return blk_expert, blk_valid, exp_seg, comb_seg


def kernel(x, c, w_ada, b_ada, w_in, sink, rpb, g_out_a, g_out_b, w_out, w_router, b_router,
           w_gate_up, b_gate_up, w_down, b_down, g_final):
    bsz, seq, d = x.shape
    t = bsz * seq
    depth = w_ada.shape[0]
    assert depth == 1, "the final norm is fused into the layer combine"
    bm = MOE_BM
    n_blk = (t * TOP_K) // bm + N_EXPERTS
    nt = t // SORT_TM

    cos_t, sa_t, sb_t = _rope_tables(seq)
    x2 = x.reshape(t, d)
    for l in range(depth):
        mod = _ada(c, w_ada[l], b_ada[l][None, :])
        shift_m, scale_m, gate_m, shift_f, scale_f, gate_f = [
            m.reshape(bsz, 1, d) for m in jnp.split(mod, 6, axis=-1)]

        proj = _inproj(x2, shift_m, scale_m, _arrange_in_proj(w_in[l]), cos_t, sa_t, sb_t, seq)
        mix_a = _swa(proj, sink[l], g_out_a[l][None, :], bsz, seq)
        mix_b = _natten(proj, _na_bias_table(rpb[l], seq // GRID_W), g_out_b[l][None, :], bsz, seq)

        w_r = jnp.pad(w_router[l], ((0, 0), (0, LANES - N_EXPERTS)))
        b_r = jnp.pad(b_router[l], (0, LANES - N_EXPERTS), constant_values=NEG)[None, :]
        x1, hs, pg, tile_cnt, tile_off = _outproj(
            x2, mix_a, mix_b, w_out[l].astype(BF16), gate_m, shift_f, scale_f, w_r, b_r, seq)

        blk_expert, blk_valid, exp_seg, comb_seg = _moe_tables(
            tile_cnt[:, 0, :N_EXPERTS].astype(I32), tile_off[:, 0, :N_EXPERTS].astype(I32), n_blk)
        ys = _experts(blk_expert, blk_valid, exp_seg, hs,
                      w_gate_up[l], b_gate_up[l][:, None, :],
                      w_down[l], b_down[l][:, None, :], nt)
        x2 = _combine(comb_seg, x1, pg, gate_f, g_final[None, :], ys, seq)
    return x2.reshape(bsz, seq, d)
```

```python
import functools

import jax
import jax.numpy as jnp
from jax import lax
from jax.experimental import pallas as pl
from jax.experimental.pallas import tpu as pltpu

F32 = jnp.float32
BF16 = jnp.bfloat16
I32 = jnp.int32

HEAD_DIM = 64
A_Q_HEADS = 8
A_KV_HEADS = 2
A_GROUP = A_Q_HEADS // A_KV_HEADS
B_HEADS = 8
A_Q_W = A_Q_HEADS * HEAD_DIM
A_KV_W = A_KV_HEADS * HEAD_DIM
B_W = B_HEADS * HEAD_DIM
WINDOW = 128
ROT_DIMS = HEAD_DIM // 4
ROPE_THETA = 500000.0
GRID_W = 64
NA_ROWS_MAX = 8
NA_COLS = 16
N_EXPERTS = 32
TOP_K = 4
SWIGLU_LIMIT = 7.0
SWIGLU_ALPHA = 1.702
EPS = 1e-5
NEG = -1e30
LOG2E = 1.4426950408889634

LANES = 128
ROW_WORDS = 4
PROJ_BLK = 512
KV_A_COL_BLOCK = (A_Q_W + 3 * B_W) // (2 * A_KV_W)
TM_PROJ = 512
SORT_TM = 256
SORT_TILES_PER_STEP = 4
MOE_BM = 1024
MOE_SUB = 512
SWA_BLOCKS_PER_STEP = 4
SWA_STAGED_BLOCKS = 1
NA_ROWS_PER_STEP = 16
NA_ROW_UNROLL = 4
VMEM_LIMIT = 48 << 20
VMEM_LIMIT_EXPERTS = 58 << 20


def _cparams(sem, limit=VMEM_LIMIT):
    return pltpu.CompilerParams(dimension_semantics=sem, vmem_limit_bytes=limit)


def _rms(x):
    return x * lax.rsqrt(jnp.mean(x * x, axis=-1, keepdims=True) + EPS)


def _pack_bf16_pair(lo, hi):
    return pltpu.pack_elementwise([lo, hi], packed_dtype=BF16)


def _unpack_bf16_pair(words):
    return (pltpu.unpack_elementwise(words, index=0, packed_dtype=BF16, unpacked_dtype=F32),
            pltpu.unpack_elementwise(words, index=1, packed_dtype=BF16, unpacked_dtype=F32))


def _rows_to_words(x):
    return [_pack_bf16_pair(x[:, 2 * c * LANES:(2 * c + 1) * LANES], x[:, (2 * c + 1) * LANES:(2 * c + 2) * LANES])
            for c in range(ROW_WORDS)]


def _words_to_rows(words):
    parts = []
    for w in words:
        parts.extend(_unpack_bf16_pair(w))
    return jnp.concatenate(parts, axis=1).astype(BF16)


def _ada_body(c_ref, w_ref, b_ref, o_ref):
    c = c_ref[...]
    ca = c / (1.0 + jnp.exp(-c))
    o_ref[...] = jnp.dot(ca, w_ref[...], preferred_element_type=F32,
                         precision=lax.Precision.HIGHEST) + b_ref[...]


def _ada(c, w, b):
    bsz, d = c.shape
    n = w.shape[1]
    bn = n // 4
    return pl.pallas_call(
        _ada_body,
        out_shape=jax.ShapeDtypeStruct((bsz, n), F32),
        grid=(4,),
        in_specs=[pl.BlockSpec((bsz, d), lambda i: (0, 0)),
                  pl.BlockSpec((d, bn), lambda i: (0, i)),
                  pl.BlockSpec((1, bn), lambda i: (0, i))],
        out_specs=pl.BlockSpec((bsz, bn), lambda i: (0, i)),
        compiler_params=_cparams(("arbitrary",)),
        name="ada",
    )(c, w, b)


def _inproj_body(x_ref, sh_ref, sc_ref, w_ref, cos_ref, sa_ref, sb_ref, o_ref):
    h = (_rms(x_ref[...]) * (1.0 + sc_ref[0]) + sh_ref[0]).astype(BF16)
    cos = cos_ref[...]
    sa = sa_ref[...]
    sb = sb_ref[...]
    blocks = [(0, A_Q_W, A_Q_W // LANES)]
    blocks += [(A_Q_W + i * B_W, B_W, 0) for i in range(3)]
    blocks += [(A_Q_W + 3 * B_W, 2 * A_KV_W, A_KV_W // LANES)]
    for c0, width, n_rot in blocks:
        p = jnp.dot(h, w_ref[:, c0:c0 + width], preferred_element_type=F32)
        if n_rot:
            parts = []
            for j in range(width // LANES):
                pj = p[:, j * LANES:(j + 1) * LANES]
                if j < n_rot:
                    pj = (pj * cos + pltpu.roll(pj, LANES - ROT_DIMS // 2, axis=1) * sa
                          + pltpu.roll(pj, ROT_DIMS // 2, axis=1) * sb)
                parts.append(pj)
            p = jnp.concatenate(parts, axis=1)
        o_ref[:, c0:c0 + width] = p.astype(BF16)


def _inproj(x2, shift, scale, w_ext, cos_t, sa_t, sb_t, seq):
    t, d = x2.shape
    n = w_ext.shape[1]
    tm = TM_PROJ
    tpb = seq // tm
    return pl.pallas_call(
        _inproj_body,
        out_shape=jax.ShapeDtypeStruct((t, n), BF16),
        grid=(t // tm,),
        in_specs=[pl.BlockSpec((tm, d), lambda i: (i, 0)),
                  pl.BlockSpec((1, 1, d), lambda i: (i // tpb, 0, 0)),
                  pl.BlockSpec((1, 1, d), lambda i: (i // tpb, 0, 0)),
                  pl.BlockSpec((d, n), lambda i: (0, 0)),
                  pl.BlockSpec((tm, LANES), lambda i: (i % tpb, 0)),
                  pl.BlockSpec((tm, LANES), lambda i: (i % tpb, 0)),
                  pl.BlockSpec((tm, LANES), lambda i: (i % tpb, 0))],
        out_specs=pl.BlockSpec((tm, n), lambda i: (i, 0)),
        compiler_params=_cparams(("arbitrary",)),
        name="inproj",
    )(x2, shift, scale, w_ext, cos_t, sa_t, sb_t)


def _swa_body(sink_ref, q_ref, kv_ref, g_ref, o_ref, *, seq):
    step = pl.program_id(1)
    blk = WINDOW
    kw_len = 3 * blk
    gw = A_GROUP * HEAD_DIM
    srows = A_GROUP * blk
    row = lax.broadcasted_iota(I32, (srows, 1), 0)
    grp = lax.broadcasted_iota(I32, (1, gw), 1) // HEAD_DIM
    blocks = range(SWA_BLOCKS_PER_STEP)
    def spread(tile):
        words = pltpu.bitcast(tile, jnp.uint32)
        swapped = pltpu.roll(words, HEAD_DIM, axis=1)
        low = lax.broadcasted_iota(I32, words.shape, 1) < HEAD_DIM
        out = []
        for pair in (jnp.where(low, words, swapped), jnp.where(low, swapped, words)):
            pair = pltpu.bitcast(pair, BF16)
            out.append(jnp.concatenate([pair] * (gw // LANES), axis=1))
        return out

    kws, vws, alloweds = [], [], []
    for bi in blocks:
        n = step * SWA_BLOCKS_PER_STEP + bi
        start = pl.multiple_of(jnp.clip(n * blk - blk, 0, seq - kw_len), blk)
        kv = kv_ref[pl.ds(start, kw_len), :]
        kws.append(jnp.concatenate(spread(kv[:, :A_KV_W]), axis=1))
        vws.append(jnp.concatenate(spread(kv[:, A_KV_W:]), axis=1))
        qpos = n * blk + row % blk
        kpos = start + lax.broadcasted_iota(I32, (1, kw_len), 1)
        alloweds.append(jnp.abs(qpos - kpos) <= WINDOW)
    sks = []
    for h in range(A_KV_HEADS):
        sk = jnp.full((srows, 1), sink_ref[h * A_GROUP], F32)
        for g in range(1, A_GROUP):
            sk = jnp.where(row // blk == g, sink_ref[h * A_GROUP + g], sk)
        sks.append(sk * LOG2E)
    for b0 in range(0, SWA_BLOCKS_PER_STEP, SWA_STAGED_BLOCKS):
        group = range(b0, b0 + SWA_STAGED_BLOCKS)
        units = [(bi, h) for bi in group for h in range(A_KV_HEADS)]
        qms = []
        for bi, h in units:
            qh = q_ref[bi * blk:(bi + 1) * blk, h * gw:(h + 1) * gw]
            qms.append(jnp.concatenate(
                [jnp.where(grp == g, qh, jnp.zeros_like(qh)) for g in range(A_GROUP)], axis=0))
        ss = [lax.dot_general(qms[u], kws[bi][:, h * gw:(h + 1) * gw], (((1,), (1,)), ((), ())),
                              preferred_element_type=F32) for u, (bi, h) in enumerate(units)]
        ss = [jnp.where(alloweds[bi], ss[u], NEG) for u, (bi, h) in enumerate(units)]
        ms = [jnp.maximum(jnp.max(ss[u], axis=-1, keepdims=True), sks[h]) for u, (bi, h) in enumerate(units)]
        ps = [jnp.exp2(ss[u] - ms[u]) for u in range(len(units))]
        invs = [1.0 / (jnp.sum(ps[u], axis=-1, keepdims=True) + jnp.exp2(sks[h] - ms[u]))
                for u, (bi, h) in enumerate(units)]
        ress = [jnp.dot(ps[u].astype(BF16), vws[bi][:, h * gw:(h + 1) * gw], preferred_element_type=F32) * invs[u]
                for u, (bi, h) in enumerate(units)]
        accs = []
        for res in ress:
            acc = res[0:blk]
            for g in range(1, A_GROUP):
                acc = jnp.where(grp == g, res[g * blk:(g + 1) * blk], acc)
            accs.append(acc)
        for gi, bi in enumerate(group):
            oa = jnp.concatenate(accs[gi * A_KV_HEADS:(gi + 1) * A_KV_HEADS], axis=1)
            o_ref[bi * blk:(bi + 1) * blk, :] = (_rms(oa) * g_ref[...]).astype(BF16)


def _swa(proj, sink, g_a, bsz, seq):
    t = proj.shape[0]
    blk = WINDOW * SWA_BLOCKS_PER_STEP
    nq = seq // blk
    grid_spec = pltpu.PrefetchScalarGridSpec(
        num_scalar_prefetch=1, grid=(bsz, nq),
        in_specs=[pl.BlockSpec((blk, PROJ_BLK), lambda b, n, s: (b * nq + n, 0)),
                  pl.BlockSpec((seq, 2 * A_KV_W), lambda b, n, s: (b, KV_A_COL_BLOCK)),
                  pl.BlockSpec((1, A_Q_W), lambda b, n, s: (0, 0))],
        out_specs=pl.BlockSpec((blk, A_Q_W), lambda b, n, s: (b * nq + n, 0)))
    return pl.pallas_call(
        functools.partial(_swa_body, seq=seq),
        out_shape=jax.ShapeDtypeStruct((t, A_Q_W), BF16),
        grid_spec=grid_spec,
        compiler_params=_cparams(("arbitrary", "arbitrary")),
        name="swa",
    )(sink, proj, proj, g_a)


def _na_body(q_ref, k_ref, v_ref, bias_ref, g_ref, o_ref, *, rows, kr, rb):
    blk = pl.program_id(1)
    per = LANES // HEAD_DIM
    n_pair = B_W // LANES
    half = lax.broadcasted_iota(I32, (1, LANES), 1) // HEAD_DIM
    gain = g_ref[...]

    def one_row(i, carry):
        r = blk * rb + i
        rs = jnp.clip(r - kr // 2, 0, rows - kr)
        start = pl.multiple_of(rs * GRID_W, GRID_W)
        kw = k_ref[pl.ds(start, kr * GRID_W), :]
        vw = v_ref[pl.ds(start, kr * GRID_W), :]
        q = q_ref[pl.ds(pl.multiple_of(i * GRID_W, GRID_W), GRID_W), :]
        s_parts = []
        for pr in range(n_pair):
            qp = q[:, pr * LANES:(pr + 1) * LANES]
            qm = jnp.concatenate([jnp.where(half == j, qp, jnp.zeros_like(qp)) for j in range(per)], axis=0)
            s_parts.append(lax.dot_general(qm, kw[:, pr * LANES:(pr + 1) * LANES],
                                           (((1,), (1,)), ((), ())), preferred_element_type=F32))
        s = jnp.concatenate(s_parts, axis=0) + bias_ref[r - rs]
        m = jnp.max(s, axis=-1, keepdims=True)
        p = jnp.exp2(s - m)
        inv = 1.0 / jnp.sum(p, axis=-1, keepdims=True)
        pn = p.astype(BF16)
        o_parts = []
        for pr in range(n_pair):
            srows = slice(pr * per * GRID_W, (pr + 1) * per * GRID_W)
            res = jnp.dot(pn[srows], vw[:, pr * LANES:(pr + 1) * LANES], preferred_element_type=F32) * inv[srows]
            op = res[0:GRID_W]
            for j in range(1, per):
                op = jnp.where(half == j, res[j * GRID_W:(j + 1) * GRID_W], op)
            o_parts.append(op)
        ob = jnp.concatenate(o_parts, axis=1)
        o_ref[pl.ds(pl.multiple_of(i * GRID_W, GRID_W), GRID_W), :] = (_rms(ob) * gain).astype(BF16)
        return carry

    def row_group(ig, carry):
        for u in range(NA_ROW_UNROLL):
            one_row(NA_ROW_UNROLL * ig + u, carry)
        return carry

    lax.fori_loop(0, rb // NA_ROW_UNROLL, row_group, 0)


def _natten(proj, bias_t, g_b, bsz, seq):
    t = proj.shape[0]
    rows = seq // GRID_W
    kr = min(NA_ROWS_MAX, rows)
    rb = NA_ROWS_PER_STEP
    nb = rows // rb
    return pl.pallas_call(
        functools.partial(_na_body, rows=rows, kr=kr, rb=rb),
        out_shape=jax.ShapeDtypeStruct((t, B_W), BF16),
        grid=(bsz, nb),
        in_specs=[pl.BlockSpec((rb * GRID_W, PROJ_BLK), lambda b, r: (b * nb + r, 1)),
                  pl.BlockSpec((seq, PROJ_BLK), lambda b, r: (b, 2)),
                  pl.BlockSpec((seq, PROJ_BLK), lambda b, r: (b, 3)),
                  pl.BlockSpec((kr, B_HEADS * GRID_W, kr * GRID_W), lambda b, r: (0, 0, 0)),
                  pl.BlockSpec((1, B_W), lambda b, r: (0, 0))],
        out_specs=pl.BlockSpec((rb * GRID_W, B_W), lambda b, r: (b * nb + r, 0)),
        compiler_params=_cparams(("arbitrary", "arbitrary")),
        name="natten",
    )(proj, proj, proj, bias_t, g_b)


def _na_bias_table(rpb, rows):
    kr = min(NA_ROWS_MAX, rows)
    kc = NA_COLS
    col = jnp.arange(GRID_W)
    col_start = jnp.clip(col - kc // 2, 0, GRID_W - kc)
    col_mask = (col[None, :] >= col_start[:, None]) & (col[None, :] < col_start[:, None] + kc)
    col_off = jnp.clip(col[None, :] - col[:, None], -(kc - 1), kc - 1) + (NA_COLS - 1)
    rows_sel = jnp.stack([rpb[:, NA_ROWS_MAX - 1 - cs:NA_ROWS_MAX - 1 - cs + kr, :]
                          for cs in range(kr)], axis=0)
    pick = (col_off[None, :, :] == jnp.arange(2 * kc - 1)[:, None, None]).astype(F32)
    tbl = jnp.einsum('chjr,rqk->chqjk', rows_sel, pick, precision=lax.Precision.HIGHEST)
    tbl = jnp.where(col_mask[None, None, :, None, :], tbl * LOG2E, NEG)
    return tbl.reshape(kr, rpb.shape[0] * GRID_W, kr * GRID_W).astype(F32)


def _outproj_body(x_ref, ma_ref, mb_ref, wo_ref, gm_ref, sf_ref, cf_ref, wr_ref, br_ref,
                  x1_ref, hs_ref, pg_ref, tc_ref, to_ref):
    tm = SORT_TM
    tiles = range(SORT_TILES_PER_STEP)
    tt = tm * SORT_TILES_PER_STEP
    n_pos = TOP_K * tm
    mix = (jnp.dot(ma_ref[...], wo_ref[:A_Q_W, :], preferred_element_type=F32)
           + jnp.dot(mb_ref[...], wo_ref[A_Q_W:, :], preferred_element_type=F32))
    x1 = x_ref[...] + gm_ref[0] * mix
    x1_ref[...] = x1
    h = _rms(x1) * (1.0 + cf_ref[0]) + sf_ref[0]
    h_hi = h.astype(BF16)
    h_lo = (h - h_hi.astype(F32)).astype(BF16)
    w_r = wr_ref[...]
    w_hi = w_r.astype(BF16)
    w_lo = (w_r - w_hi.astype(F32)).astype(BF16)
    l_hi = jnp.dot(h_hi, jnp.concatenate([w_hi, w_lo], axis=1), preferred_element_type=F32)
    logits = (l_hi[:, :LANES] + l_hi[:, LANES:]
              + jnp.dot(h_lo, w_hi, preferred_element_type=F32) + br_ref[...])
    lane = lax.broadcasted_iota(I32, (tt, LANES), 1)
    work = logits
    vals, idxs = [], []
    for _ in range(TOP_K):
        m = jnp.max(work, axis=-1, keepdims=True)
        ik = jnp.min(jnp.where(work == m, lane, LANES), axis=-1, keepdims=True)
        vals.append(m)
        idxs.append(ik)
        work = jnp.where(lane == ik, -jnp.inf, work)
    es = [jnp.exp(v - vals[0]) for v in vals]
    den = es[0] + es[1] + es[2] + es[3]
    mh = jnp.zeros((tt, LANES), F32)
    for ik in idxs:
        mh = mh + (lane == ik).astype(F32)
    mhb = mh.astype(BF16)
    tri = (lax.broadcasted_iota(I32, (tm, tm), 0) > lax.broadcasted_iota(I32, (tm, tm), 1)).astype(BF16)
    lower = (lax.broadcasted_iota(I32, (LANES, LANES), 0)
             < lax.broadcasted_iota(I32, (LANES, LANES), 1)).astype(BF16)
    mh_t = [mhb[ti * tm:(ti + 1) * tm] for ti in tiles]
    earlier = [jnp.dot(tri, mh_t[ti], preferred_element_type=F32) for ti in tiles]
    below = [jnp.dot(mh_t[ti], lower, preferred_element_type=F32) for ti in tiles]
    toff = [jnp.sum(below[ti], axis=0, keepdims=True) for ti in tiles]
    cnt = [jnp.sum(mh[ti * tm:(ti + 1) * tm], axis=0, keepdims=True) for ti in tiles]
    posf = jnp.concatenate([earlier[ti] + toff[ti] for ti in tiles], axis=0)
    pg = jnp.zeros((tt, LANES), F32)
    for k in range(TOP_K):
        pos_k = jnp.sum(jnp.where(lane == idxs[k], posf, 0.0), axis=-1, keepdims=True)
        pg = jnp.where(lane == k, pos_k, pg)
        pg = jnp.where(lane == TOP_K + k, es[k] / den, pg)
    pg_ref[...] = pg
    pgt_all = pg.T[:ROW_WORDS]
    pgt = [pgt_all[:, ti * tm:(ti + 1) * tm] for ti in tiles]
    for ti in tiles:
        tc_ref[ti] = cnt[ti]
        to_ref[ti] = toff[ti]
    pos_iota = lax.broadcasted_iota(I32, (n_pos, tm), 0)
    hits = []
    for ti in tiles:
        hit = pos_iota == pgt[ti][0:1, :].astype(I32)
        for k in range(1, TOP_K):
            hit = hit | (pos_iota == pgt[ti][k:k + 1, :].astype(I32))
        hits.append(jnp.where(hit, 1.0, 0.0).astype(BF16))
    srt = [jnp.dot(hits[ti], h_hi[ti * tm:(ti + 1) * tm], preferred_element_type=F32) for ti in tiles]
    for ti in tiles:
        for c, words in enumerate(_rows_to_words(srt[ti])):
            hs_ref[pl.ds(ti * n_pos * ROW_WORDS + c, n_pos, stride=ROW_WORDS), :] = words


def _outproj(x2, mix_a, mix_b, w_out, gate_m, shift_f, scale_f, w_r, b_r, seq):
    t, d = x2.shape
    tm = SORT_TM
    nt = t // tm
    ts = SORT_TILES_PER_STEP
    tpb = seq // (ts * tm)
    n_pos = TOP_K * tm
    mod_spec = pl.BlockSpec((1, 1, d), lambda i: (i // tpb, 0, 0))
    return pl.pallas_call(
        _outproj_body,
        out_shape=(jax.ShapeDtypeStruct((t, d), F32),
                   jax.ShapeDtypeStruct((t * TOP_K * ROW_WORDS, LANES), I32),
                   jax.ShapeDtypeStruct((t, LANES), F32),
                   jax.ShapeDtypeStruct((nt, 1, LANES), F32),
                   jax.ShapeDtypeStruct((nt, 1, LANES), F32)),
        grid=(nt // ts,),
        in_specs=[pl.BlockSpec((ts * tm, d), lambda i: (i, 0)),
                  pl.BlockSpec((ts * tm, A_Q_W), lambda i: (i, 0)),
                  pl.BlockSpec((ts * tm, B_W), lambda i: (i, 0)),
                  pl.BlockSpec((A_Q_W + B_W, d), lambda i: (0, 0)),
                  mod_spec, mod_spec, mod_spec,
                  pl.BlockSpec((d, LANES), lambda i: (0, 0)),
                  pl.BlockSpec((1, LANES), lambda i: (0, 0))],
        out_specs=(pl.BlockSpec((ts * tm, d), lambda i: (i, 0)),
                   pl.BlockSpec((ts * n_pos * ROW_WORDS, LANES), lambda i: (i, 0)),
                   pl.BlockSpec((ts * tm, LANES), lambda i: (i, 0)),
                   pl.BlockSpec((ts, 1, LANES), lambda i: (i, 0, 0)),
                   pl.BlockSpec((ts, 1, LANES), lambda i: (i, 0, 0))),
        compiler_params=_cparams(("arbitrary",)),
        name="outproj",
    )(x2, mix_a, mix_b, w_out, gate_m, shift_f, scale_f, w_r, b_r)


def _segment_copies(seg_ref, n_seg, first, last, src_hbm, dst_buf, sem, tbl=0):
    def body(i, carry):
        size = pl.multiple_of(seg_ref[tbl, 0, 2 * n_seg + i], ROW_WORDS)

        @pl.when(size > 0)
        def _():
            src = pl.multiple_of(seg_ref[tbl, 0, i], ROW_WORDS)
            dst = pl.multiple_of(seg_ref[tbl, 0, n_seg + i], ROW_WORDS)
            pltpu.make_async_copy(src_hbm.at[pl.ds(src, size), :], dst_buf.at[pl.ds(dst, size), :], sem).start()
        return carry

    lax.fori_loop(first, last, body, 0)


def _experts_body(be_ref, vl_ref, seg0_ref, segn_ref, hs_hbm, wgu32_ref, bgu_ref, wd32_ref, bd_ref, o_ref,
                  xbuf, xs_ref, wgu_ref, wd_ref, gsem, *, n_blk, n_seg):
    bm = MOE_BM
    rt = ROW_WORDS
    d_ff = wd_ref.shape[0]
    j = pl.program_id(0)
    slot = j % 2
    other = 1 - slot
    valid = vl_ref[j]

    def issue(seg_ref, s):
        _segment_copies(seg_ref, n_seg, seg_ref[0, 0, 3 * n_seg], seg_ref[0, 0, 3 * n_seg + 1],
                        hs_hbm, xbuf.at[s], gsem.at[s])

    @pl.when(j == 0)
    def _():
        xbuf[...] = jnp.zeros_like(xbuf)
        issue(seg0_ref, 0)

    @pl.when((j == 0) | (be_ref[j] != be_ref[jnp.maximum(j - 1, 0)]))
    def _():
        for c in range(d_ff // LANES):
            wgu_ref[:, 2 * c * LANES:(2 * c + 1) * LANES] = (
                wgu32_ref[0, :, c * LANES:(c + 1) * LANES].astype(BF16))
            wgu_ref[:, (2 * c + 1) * LANES:(2 * c + 2) * LANES] = (
                wgu32_ref[0, :, d_ff + c * LANES:d_ff + (c + 1) * LANES].astype(BF16))
        wd_ref[...] = wd32_ref[0].astype(BF16)

    @pl.when(valid > 0)
    def _():
        rows = pl.multiple_of(valid * rt, rt)
        pltpu.make_async_copy(hs_hbm.at[pl.ds(0, rows), :], xbuf.at[slot, pl.ds(0, rows), :],
                              gsem.at[slot]).wait()

    @pl.when(j + 1 < n_blk)
    def _():
        issue(segn_ref, other)

    sub = MOE_SUB
    n_chunk = d_ff // (2 * LANES)
    half = sub // 2
    second = sub * rt
    n_half = (valid + half - 1) // half

    def assemble(r0, n):
        return _words_to_rows([xbuf[slot, pl.ds(r0 + c, n, stride=rt), :] for c in range(rt)])

    def act_chunk(xs, c):
        gu = jnp.dot(xs, wgu_ref[:, 4 * c * LANES:(4 * c + 4) * LANES], preferred_element_type=F32)
        outs = []
        for t in range(2):
            lt = 2 * c + t
            gate = gu[:, 2 * t * LANES:(2 * t + 1) * LANES] + bgu_ref[0, :, lt * LANES:(lt + 1) * LANES]
            up = (gu[:, (2 * t + 1) * LANES:(2 * t + 2) * LANES]
                  + bgu_ref[0, :, d_ff + lt * LANES:d_ff + (lt + 1) * LANES])
            gate = jnp.minimum(gate, SWIGLU_LIMIT)
            up = jnp.clip(up, -SWIGLU_LIMIT, SWIGLU_LIMIT)
            glu = gate * (0.5 + 0.5 * jnp.tanh((0.5 * SWIGLU_ALPHA) * gate))
            outs.append(((up + 1.0) * glu).astype(BF16))
        return jnp.concatenate(outs, axis=1)

    def down(acts):
        return jnp.dot(jnp.concatenate(acts, axis=1), wd_ref[...], preferred_element_type=F32) + bd_ref[0]

    def store_rows(r0, ys):
        for c, words in enumerate(_rows_to_words(ys)):
            o_ref[pl.ds(r0 + c, ys.shape[0], stride=rt), :] = words

    def zero_rows(r0, n):
        o_ref[pl.ds(r0, n * rt), :] = jnp.zeros((n * rt, LANES), I32)

    def single_pass(n):
        xs_ref[0, :n, :] = assemble(0, n)
        xs = xs_ref[0, :n, :]
        store_rows(0, down([act_chunk(xs, c) for c in range(n_chunk)]))
        zero_rows(n * rt, bm - n)

    def double_pass(n2):
        xs_ref[0] = assemble(0, sub)
        xa = xs_ref[0]
        acts = []
        for c in range(n_chunk):
            acts.append(act_chunk(xa, c))
            if c == n_chunk // 2:
                xs_ref[1, :n2, :] = assemble(second, n2)
        ys_a = down(acts)
        xb = xs_ref[1, :n2, :]
        acts = []
        for c in range(n_chunk):
            acts.append(act_chunk(xb, c))
            if c == 1:
                store_rows(0, ys_a)
        store_rows(second, down(acts))
        if n2 < sub:
            zero_rows(second + n2 * rt, sub - n2)

    @pl.when(n_half == 0)
    def _():
        zero_rows(0, bm)

    for nh, run in ((1, lambda: single_pass(half)), (2, lambda: single_pass(sub)),
                    (3, lambda: double_pass(half)), (4, lambda: double_pass(sub))):
        pl.when(n_half == nh)(run)


def _experts(blk_expert, blk_valid, seg, hs, wgu, bgu, wd, bd, n_seg):
    bm = MOE_BM
    rt = ROW_WORDS
    n_blk, _, seg_w = seg.shape
    d = wd.shape[2]
    d_ff = wd.shape[1]
    assert d == 2 * rt * LANES and hs.shape[1] == LANES and bm == 2 * MOE_SUB
    smem = functools.partial(pl.BlockSpec, (1, 1, seg_w), memory_space=pltpu.SMEM)
    last = n_blk - 1
    grid_spec = pltpu.PrefetchScalarGridSpec(
        num_scalar_prefetch=2, grid=(n_blk,),
        in_specs=[smem(lambda j, be, vl: (j, 0, 0)),
                  smem(lambda j, be, vl: (jnp.minimum(j + 1, last), 0, 0)),
                  pl.BlockSpec(memory_space=pl.ANY),
                  pl.BlockSpec((1, d, 2 * d_ff), lambda j, be, vl: (be[j], 0, 0)),
                  pl.BlockSpec((1, 1, 2 * d_ff), lambda j, be, vl: (be[j], 0, 0)),
                  pl.BlockSpec((1, d_ff, d), lambda j, be, vl: (be[j], 0, 0)),
                  pl.BlockSpec((1, 1, d), lambda j, be, vl: (be[j], 0, 0))],
        out_specs=pl.BlockSpec((bm * rt, LANES), lambda j, be, vl: (j, 0)),
        scratch_shapes=[pltpu.VMEM((2, bm * rt, LANES), I32), pltpu.VMEM((2, MOE_SUB, d), BF16),
                        pltpu.VMEM((d, 2 * d_ff), BF16), pltpu.VMEM((d_ff, d), BF16),
                        pltpu.SemaphoreType.DMA((2,))])
    return pl.pallas_call(
        functools.partial(_experts_body, n_blk=n_blk, n_seg=n_seg),
        out_shape=jax.ShapeDtypeStruct((n_blk * bm * rt, LANES), I32),
        grid_spec=grid_spec,
        compiler_params=_cparams(("arbitrary",), VMEM_LIMIT_EXPERTS),
        name="experts",
    )(blk_expert, blk_valid, seg, seg, hs, wgu, bgu, wd, bd)


def _combine_body(seg0_ref, segn_ref, x1_ref, pg_ref, gf_ref, gfin_ref, ys_hbm, o_ref,
                  ybuf, sem, *, n_steps):
    tm = SORT_TM
    tiles = range(SORT_TILES_PER_STEP)
    n_pos = TOP_K * tm
    rt = ROW_WORDS
    i = pl.program_id(0)
    slot = i % 2
    other = 1 - slot

    def issue(seg_ref, s):
        for ti in tiles:
            _segment_copies(seg_ref, N_EXPERTS, 0, N_EXPERTS, ys_hbm, ybuf.at[s, ti], sem.at[s], tbl=ti)

    @pl.when(i == 0)
    def _():
        issue(seg0_ref, 0)

    for ti in tiles:
        pltpu.make_async_copy(ys_hbm.at[pl.ds(0, n_pos * rt), :], ybuf.at[slot, ti], sem.at[slot]).wait()

    @pl.when(i + 1 < n_steps)
    def _():
        issue(segn_ref, other)

    lane_pos = lax.broadcasted_iota(I32, (tm, n_pos), 1)
    ws = []
    for ti in tiles:
        pg = pg_ref[ti * tm:(ti + 1) * tm, :]
        w = jnp.zeros((tm, n_pos), F32)
        for k in range(TOP_K):
            w = jnp.where(lane_pos == pg[:, k:k + 1].astype(I32), pg[:, TOP_K + k:TOP_K + k + 1], w)
        ws.append(w)
    w_his = [w.astype(BF16) for w in ws]
    w_los = [(ws[ti] - w_his[ti].astype(F32)).astype(BF16) for ti in tiles]
    ysbs = [_words_to_rows([ybuf[slot, ti, pl.ds(c, n_pos, stride=rt), :] for c in range(rt)])
            for ti in tiles]
    ys = [jnp.dot(w_his[ti], ysbs[ti], preferred_element_type=F32)
          + jnp.dot(w_los[ti], ysbs[ti], preferred_element_type=F32) for ti in tiles]
    for ti in tiles:
        x2 = x1_ref[ti * tm:(ti + 1) * tm, :] + gf_ref[0] * ys[ti]
        o_ref[ti * tm:(ti + 1) * tm, :] = _rms(x2) * gfin_ref[...]


def _combine(seg, x1, pg, gate_f, g_final, ys, seq):
    t, d = x1.shape
    tm = SORT_TM
    ts = SORT_TILES_PER_STEP
    tpb = seq // (ts * tm)
    n_steps = t // (ts * tm)
    n_pos = TOP_K * tm
    seg_w = seg.shape[2]
    smem = functools.partial(pl.BlockSpec, (ts, 1, seg_w), memory_space=pltpu.SMEM)
    return pl.pallas_call(
        functools.partial(_combine_body, n_steps=n_steps),
        out_shape=jax.ShapeDtypeStruct((t, d), F32),
        grid=(n_steps,),
        in_specs=[smem(lambda i: (i, 0, 0)),
                  smem(lambda i: (jnp.minimum(i + 1, n_steps - 1), 0, 0)),
                  pl.BlockSpec((ts * tm, d), lambda i: (i, 0)),
                  pl.BlockSpec((ts * tm, LANES), lambda i: (i, 0)),
                  pl.BlockSpec((1, 1, d), lambda i: (i // tpb, 0, 0)),
                  pl.BlockSpec((1, d), lambda i: (0, 0)),
                  pl.BlockSpec(memory_space=pl.ANY)],
        out_specs=pl.BlockSpec((ts * tm, d), lambda i: (i, 0)),
        scratch_shapes=[pltpu.VMEM((2, ts, n_pos * ROW_WORDS, LANES), I32), pltpu.SemaphoreType.DMA((2,))],
        compiler_params=_cparams(("arbitrary",)),
        name="combine",
    )(seg, seg, x1, pg, gate_f, g_final, ys)


def _rope_tables(seq):
    half = ROT_DIMS // 2
    pos = jnp.arange(seq, dtype=F32)
    inv_freq = ROPE_THETA ** (-jnp.arange(0, ROT_DIMS, 2, dtype=F32) / ROT_DIMS)
    ang = pos[:, None] * inv_freq[None, :]
    cos, sin = jnp.cos(ang), jnp.sin(ang)
    ones = jnp.ones((seq, HEAD_DIM - ROT_DIMS), F32)
    zeros = jnp.zeros((seq, HEAD_DIM - ROT_DIMS), F32)
    zh = jnp.zeros((seq, half), F32)
    rep = LANES // HEAD_DIM
    cos_t = jnp.tile(jnp.concatenate([cos, cos, ones], axis=1), (1, rep))
    sa_t = jnp.tile(jnp.concatenate([-sin, zh, zeros], axis=1), (1, rep))
    sb_t = jnp.tile(jnp.concatenate([zh, sin, zeros], axis=1), (1, rep))
    return cos_t, sa_t, sb_t


def _arrange_in_proj(w_in):
    scale = HEAD_DIM ** -0.5 * LOG2E
    o = 0
    wqa = w_in[:, o:o + A_Q_W] * scale; o += A_Q_W
    wka = w_in[:, o:o + A_KV_W]; o += A_KV_W
    wva = w_in[:, o:o + A_KV_W]; o += A_KV_W
    wqb = w_in[:, o:o + B_W] * scale; o += B_W
    wkb = w_in[:, o:o + B_W]; o += B_W
    wvb = w_in[:, o:o + B_W]
    return jnp.concatenate([wqa, wqb, wkb, wvb, wka, wva], axis=1).astype(BF16)


def _round_up(n, m):
    return (n + m - 1) // m * m


def _moe_tables(tile_cnt, tile_off, n_blk):
    bm = MOE_BM
    nt = tile_cnt.shape[0]
    n_pos = TOP_K * SORT_TM
    cum = jnp.cumsum(tile_cnt, axis=0) - tile_cnt
    counts = jnp.sum(tile_cnt, axis=0)
    padded = ((counts + bm - 1) // bm) * bm
    pends = jnp.cumsum(padded)
    pstarts = pends - padded
    blk_first = jnp.arange(n_blk, dtype=I32) * bm
    blk_expert = jnp.minimum(
        jnp.sum((pends[None, :] <= blk_first[:, None]).astype(I32), axis=1), N_EXPERTS - 1)
    own = blk_expert[:, None] == jnp.arange(N_EXPERTS, dtype=I32)[None, :]

    def of_block_expert(per_expert):
        return jnp.sum(jnp.where(own[:, None, :], per_expert[None], 0), axis=-1)

    r0 = blk_first - of_block_expert(pstarts[None, :])[:, 0]
    blk_valid = jnp.clip(of_block_expert(counts[None, :])[:, 0] - r0, 0, bm)
    cum_b = of_block_expert(cum)
    cnt_b = of_block_expert(tile_cnt)
    off_b = of_block_expert(tile_off)
    lo = jnp.maximum(r0[:, None], cum_b)
    hi = jnp.minimum(r0[:, None] + bm, cum_b + cnt_b)
    size = jnp.maximum(hi - lo, 0)
    tile_ids = jnp.arange(nt, dtype=I32)[None, :]
    src = tile_ids * n_pos + off_b + (lo - cum_b)
    dst = jnp.where(size > 0, lo - r0[:, None], 0)
    first = jnp.min(jnp.where(size > 0, tile_ids, nt), axis=1)
    last = jnp.max(jnp.where(size > 0, tile_ids + 1, 0), axis=1)
    first = jnp.minimum(first, last)
    seg_w = _round_up(3 * nt + 2, LANES)
    exp_seg = jnp.concatenate(
        [src * ROW_WORDS, dst * ROW_WORDS, size * ROW_WORDS, first[:, None], last[:, None],
         jnp.zeros((n_blk, seg_w - 3 * nt - 2), I32)], axis=1).reshape(n_blk, 1, seg_w)
    comb_seg = jnp.concatenate(
        [(pstarts[None, :] + cum) * ROW_WORDS, tile_off * ROW_WORDS, tile_cnt * ROW_WORDS,
         jnp.zeros((nt, LANES - 3 * N_EXPERTS), I32)], axis=1).reshape(nt, 1, LANES)
    return blk_expert, blk_valid, exp_seg, comb_seg


def kernel(x, c, w_ada, b_ada, w_in, sink, rpb, g_out_a, g_out_b, w_out, w_router, b_router,
           w_gate_up, b_gate_up, w_down, b_down, g_final):
    bsz, seq, d = x.shape
    t = bsz * seq
    depth = w_ada.shape[0]
    assert depth == 1, "the final norm is fused into the layer combine"
    bm = MOE_BM
    n_blk = (t * TOP_K) // bm + N_EXPERTS
    nt = t // SORT_TM

    cos_t, sa_t, sb_t = _rope_tables(seq)
    x2 = x.reshape(t, d)
    for l in range(depth):
        mod = _ada(c, w_ada[l], b_ada[l][None, :])
        shift_m, scale_m, gate_m, shift_f, scale_f, gate_f = [
            m.reshape(bsz, 1, d) for m in jnp.split(mod, 6, axis=-1)]

        proj = _inproj(x2, shift_m, scale_m, _arrange_in_proj(w_in[l]), cos_t, sa_t, sb_t, seq)
        mix_a = _swa(proj, sink[l], g_out_a[l][None, :], bsz, seq)
        mix_b = _natten(proj, _na_bias_table(rpb[l], seq // GRID_W), g_out_b[l][None, :], bsz, seq)

        w_r = jnp.pad(w_router[l], ((0, 0), (0, LANES - N_EXPERTS)))
        b_r = jnp.pad(b_router[l], (0, LANES - N_EXPERTS), constant_values=NEG)[None, :]
        x1, hs, pg, tile_cnt, tile_off = _outproj(
            x2, mix_a, mix_b, w_out[l].astype(BF16), gate_m, shift_f, scale_f, w_r, b_r, seq)

        blk_expert, blk_valid, exp_seg, comb_seg = _moe_tables(
            tile_cnt[:, 0, :N_EXPERTS].astype(I32), tile_off[:, 0, :N_EXPERTS].astype(I32), n_blk)
        ys = _experts(blk_expert, blk_valid, exp_seg, hs,
                      w_gate_up[l], b_gate_up[l][:, None, :],
                      w_down[l], b_down[l][:, None, :], nt)
        x2 = _combine(comb_seg, x1, pg, gate_f, g_final[None, :], ys, seq)
    return x2.reshape(bsz, seq, d)
```

```python
import functools

import jax
import jax.numpy as jnp
from jax import lax
from jax.experimental import pallas as pl
from jax.experimental.pallas import tpu as pltpu

F32 = jnp.float32
BF16 = jnp.bfloat16
I32 = jnp.int32

HEAD_DIM = 64
A_Q_HEADS = 8
A_KV_HEADS = 2
A_GROUP = A_Q_HEADS // A_KV_HEADS
B_HEADS = 8
A_Q_W = A_Q_HEADS * HEAD_DIM
A_KV_W = A_KV_HEADS * HEAD_DIM
B_W = B_HEADS * HEAD_DIM
WINDOW = 128
ROT_DIMS = HEAD_DIM // 4
ROPE_THETA = 500000.0
GRID_W = 64
NA_ROWS_MAX = 8
NA_COLS = 16
N_EXPERTS = 32
TOP_K = 4
SWIGLU_LIMIT = 7.0
SWIGLU_ALPHA = 1.702
EPS = 1e-5
NEG = -1e30
LOG2E = 1.4426950408889634

LANES = 128
ROW_WORDS = 4
PROJ_BLK = 512
KV_A_COL_BLOCK = (A_Q_W + 3 * B_W) // (2 * A_KV_W)
TM_PROJ = 1024
SORT_TM = 256
SORT_TILES_PER_STEP = 4
MOE_BM = 1024
MOE_SUB = 512
SWA_BLOCKS_PER_STEP = 4
SWA_STAGED_BLOCKS = 1
NA_ROWS_PER_STEP = 16
NA_ROW_UNROLL = 4
VMEM_LIMIT = 48 << 20
VMEM_LIMIT_EXPERTS = 58 << 20


def _cparams(sem, limit=VMEM_LIMIT):
    return pltpu.CompilerParams(dimension_semantics=sem, vmem_limit_bytes=limit)


def _rms(x):
    return x * lax.rsqrt(jnp.mean(x * x, axis=-1, keepdims=True) + EPS)


def _pack_bf16_pair(lo, hi):
    return pltpu.pack_elementwise([lo, hi], packed_dtype=BF16)


def _unpack_bf16_pair(words):
    return (pltpu.unpack_elementwise(words, index=0, packed_dtype=BF16, unpacked_dtype=F32),
            pltpu.unpack_elementwise(words, index=1, packed_dtype=BF16, unpacked_dtype=F32))


def _rows_to_words(x):
    return [_pack_bf16_pair(x[:, 2 * c * LANES:(2 * c + 1) * LANES], x[:, (2 * c + 1) * LANES:(2 * c + 2) * LANES])
            for c in range(ROW_WORDS)]


def _words_to_rows(words):
    parts = []
    for w in words:
        parts.extend(_unpack_bf16_pair(w))
    return jnp.concatenate(parts, axis=1).astype(BF16)


def _ada_body(c_ref, w_ref, b_ref, o_ref):
    c = c_ref[...]
    ca = c / (1.0 + jnp.exp(-c))
    o_ref[...] = jnp.dot(ca, w_ref[...], preferred_element_type=F32,
                         precision=lax.Precision.HIGHEST) + b_ref[...]


def _ada(c, w, b):
    bsz, d = c.shape
    n = w.shape[1]
    bn = n // 4
    return pl.pallas_call(
        _ada_body,
        out_shape=jax.ShapeDtypeStruct((bsz, n), F32),
        grid=(4,),
        in_specs=[pl.BlockSpec((bsz, d), lambda i: (0, 0)),
                  pl.BlockSpec((d, bn), lambda i: (0, i)),
                  pl.BlockSpec((1, bn), lambda i: (0, i))],
        out_specs=pl.BlockSpec((bsz, bn), lambda i: (0, i)),
        compiler_params=_cparams(("arbitrary",)),
        name="ada",
    )(c, w, b)


def _inproj_body(x_ref, sh_ref, sc_ref, w_ref, cos_ref, sa_ref, sb_ref, o_ref):
    h = (_rms(x_ref[...]) * (1.0 + sc_ref[0]) + sh_ref[0]).astype(BF16)
    cos = cos_ref[...]
    sa = sa_ref[...]
    sb = sb_ref[...]
    blocks = [(0, A_Q_W, A_Q_W // LANES)]
    blocks += [(A_Q_W + i * B_W, B_W, 0) for i in range(3)]
    blocks += [(A_Q_W + 3 * B_W, 2 * A_KV_W, A_KV_W // LANES)]
    for c0, width, n_rot in blocks:
        p = jnp.dot(h, w_ref[:, c0:c0 + width], preferred_element_type=F32)
        if n_rot:
            parts = []
            for j in range(width // LANES):
                pj = p[:, j * LANES:(j + 1) * LANES]
                if j < n_rot:
                    pj = (pj * cos + pltpu.roll(pj, LANES - ROT_DIMS // 2, axis=1) * sa
                          + pltpu.roll(pj, ROT_DIMS // 2, axis=1) * sb)
                parts.append(pj)
            p = jnp.concatenate(parts, axis=1)
        o_ref[:, c0:c0 + width] = p.astype(BF16)


def _inproj(x2, shift, scale, w_ext, cos_t, sa_t, sb_t, seq):
    t, d = x2.shape
    n = w_ext.shape[1]
    tm = TM_PROJ
    tpb = seq // tm
    return pl.pallas_call(
        _inproj_body,
        out_shape=jax.ShapeDtypeStruct((t, n), BF16),
        grid=(t // tm,),
        in_specs=[pl.BlockSpec((tm, d), lambda i: (i, 0)),
                  pl.BlockSpec((1, 1, d), lambda i: (i // tpb, 0, 0)),
                  pl.BlockSpec((1, 1, d), lambda i: (i // tpb, 0, 0)),
                  pl.BlockSpec((d, n), lambda i: (0, 0)),
                  pl.BlockSpec((tm, LANES), lambda i: (i % tpb, 0)),
                  pl.BlockSpec((tm, LANES), lambda i: (i % tpb, 0)),
                  pl.BlockSpec((tm, LANES), lambda i: (i % tpb, 0))],
        out_specs=pl.BlockSpec((tm, n), lambda i: (i, 0)),
        compiler_params=_cparams(("arbitrary",)),
        name="inproj",
    )(x2, shift, scale, w_ext, cos_t, sa_t, sb_t)


def _swa_body(sink_ref, q_ref, kv_ref, g_ref, o_ref, *, seq):
    step = pl.program_id(1)
    blk = WINDOW
    kw_len = 3 * blk
    gw = A_GROUP * HEAD_DIM
    srows = A_GROUP * blk
    row = lax.broadcasted_iota(I32, (srows, 1), 0)
    grp = lax.broadcasted_iota(I32, (1, gw), 1) // HEAD_DIM
    blocks = range(SWA_BLOCKS_PER_STEP)
    def spread(tile):
        words = pltpu.bitcast(tile, jnp.uint32)
        swapped = pltpu.roll(words, HEAD_DIM, axis=1)
        low = lax.broadcasted_iota(I32, words.shape, 1) < HEAD_DIM
        out = []
        for pair in (jnp.where(low, words, swapped), jnp.where(low, swapped, words)):
            pair = pltpu.bitcast(pair, BF16)
            out.append(jnp.concatenate([pair] * (gw // LANES), axis=1))
        return out

    kws, vws, alloweds = [], [], []
    for bi in blocks:
        n = step * SWA_BLOCKS_PER_STEP + bi
        start = pl.multiple_of(jnp.clip(n * blk - blk, 0, seq - kw_len), blk)
        kv = kv_ref[pl.ds(start, kw_len), :]
        kws.append(jnp.concatenate(spread(kv[:, :A_KV_W]), axis=1))
        vws.append(jnp.concatenate(spread(kv[:, A_KV_W:]), axis=1))
        qpos = n * blk + row % blk
        kpos = start + lax.broadcasted_iota(I32, (1, kw_len), 1)
        alloweds.append(jnp.abs(qpos - kpos) <= WINDOW)
    sks = []
    for h in range(A_KV_HEADS):
        sk = jnp.full((srows, 1), sink_ref[h * A_GROUP], F32)
        for g in range(1, A_GROUP):
            sk = jnp.where(row // blk == g, sink_ref[h * A_GROUP + g], sk)
        sks.append(sk * LOG2E)
    for b0 in range(0, SWA_BLOCKS_PER_STEP, SWA_STAGED_BLOCKS):
        group = range(b0, b0 + SWA_STAGED_BLOCKS)
        units = [(bi, h) for bi in group for h in range(A_KV_HEADS)]
        qms = []
        for bi, h in units:
            qh = q_ref[bi * blk:(bi + 1) * blk, h * gw:(h + 1) * gw]
            qms.append(jnp.concatenate(
                [jnp.where(grp == g, qh, jnp.zeros_like(qh)) for g in range(A_GROUP)], axis=0))
        ss = [lax.dot_general(qms[u], kws[bi][:, h * gw:(h + 1) * gw], (((1,), (1,)), ((), ())),
                              preferred_element_type=F32) for u, (bi, h) in enumerate(units)]
        ss = [jnp.where(alloweds[bi], ss[u], NEG) for u, (bi, h) in enumerate(units)]
        ms = [jnp.maximum(jnp.max(ss[u], axis=-1, keepdims=True), sks[h]) for u, (bi, h) in enumerate(units)]
        ps = [jnp.exp2(ss[u] - ms[u]) for u in range(len(units))]
        invs = [1.0 / (jnp.sum(ps[u], axis=-1, keepdims=True) + jnp.exp2(sks[h] - ms[u]))
                for u, (bi, h) in enumerate(units)]
        ress = [jnp.dot(ps[u].astype(BF16), vws[bi][:, h * gw:(h + 1) * gw], preferred_element_type=F32) * invs[u]
                for u, (bi, h) in enumerate(units)]
        accs = []
        for res in ress:
            acc = res[0:blk]
            for g in range(1, A_GROUP):
                acc = jnp.where(grp == g, res[g * blk:(g + 1) * blk], acc)
            accs.append(acc)
        for gi, bi in enumerate(group):
            oa = jnp.concatenate(accs[gi * A_KV_HEADS:(gi + 1) * A_KV_HEADS], axis=1)
            o_ref[bi * blk:(bi + 1) * blk, :] = (_rms(oa) * g_ref[...]).astype(BF16)


def _swa(proj, sink, g_a, bsz, seq):
    t = proj.shape[0]
    blk = WINDOW * SWA_BLOCKS_PER_STEP
    nq = seq // blk
    grid_spec = pltpu.PrefetchScalarGridSpec(
        num_scalar_prefetch=1, grid=(bsz, nq),
        in_specs=[pl.BlockSpec((blk, PROJ_BLK), lambda b, n, s: (b * nq + n, 0)),
                  pl.BlockSpec((seq, 2 * A_KV_W), lambda b, n, s: (b, KV_A_COL_BLOCK)),
                  pl.BlockSpec((1, A_Q_W), lambda b, n, s: (0, 0))],
        out_specs=pl.BlockSpec((blk, A_Q_W), lambda b, n, s: (b * nq + n, 0)))
    return pl.pallas_call(
        functools.partial(_swa_body, seq=seq),
        out_shape=jax.ShapeDtypeStruct((t, A_Q_W), BF16),
        grid_spec=grid_spec,
        compiler_params=_cparams(("arbitrary", "arbitrary")),
        name="swa",
    )(sink, proj, proj, g_a)


def _na_body(q_ref, k_ref, v_ref, bias_ref, g_ref, o_ref, *, rows, kr, rb):
    blk = pl.program_id(1)
    per = LANES // HEAD_DIM
    n_pair = B_W // LANES
    half = lax.broadcasted_iota(I32, (1, LANES), 1) // HEAD_DIM
    gain = g_ref[...]

    def one_row(i, carry):
        r = blk * rb + i
        rs = jnp.clip(r - kr // 2, 0, rows - kr)
        start = pl.multiple_of(rs * GRID_W, GRID_W)
        kw = k_ref[pl.ds(start, kr * GRID_W), :]
        vw = v_ref[pl.ds(start, kr * GRID_W), :]
        q = q_ref[pl.ds(pl.multiple_of(i * GRID_W, GRID_W), GRID_W), :]
        s_parts = []
        for pr in range(n_pair):
            qp = q[:, pr * LANES:(pr + 1) * LANES]
            qm = jnp.concatenate([jnp.where(half == j, qp, jnp.zeros_like(qp)) for j in range(per)], axis=0)
            s_parts.append(lax.dot_general(qm, kw[:, pr * LANES:(pr + 1) * LANES],
                                           (((1,), (1,)), ((), ())), preferred_element_type=F32))
        s = jnp.concatenate(s_parts, axis=0) + bias_ref[r - rs]
        m = jnp.max(s, axis=-1, keepdims=True)
        p = jnp.exp2(s - m)
        inv = 1.0 / jnp.sum(p, axis=-1, keepdims=True)
        pn = p.astype(BF16)
        o_parts = []
        for pr in range(n_pair):
            srows = slice(pr * per * GRID_W, (pr + 1) * per * GRID_W)
            res = jnp.dot(pn[srows], vw[:, pr * LANES:(pr + 1) * LANES], preferred_element_type=F32) * inv[srows]
            op = res[0:GRID_W]
            for j in range(1, per):
                op = jnp.where(half == j, res[j * GRID_W:(j + 1) * GRID_W], op)
            o_parts.append(op)
        ob = jnp.concatenate(o_parts, axis=1)
        o_ref[pl.ds(pl.multiple_of(i * GRID_W, GRID_W), GRID_W), :] = (_rms(ob) * gain).astype(BF16)
        return carry

    def row_group(ig, carry):
        for u in range(NA_ROW_UNROLL):
            one_row(NA_ROW_UNROLL * ig + u, carry)
        return carry

    lax.fori_loop(0, rb // NA_ROW_UNROLL, row_group, 0)


def _natten(proj, bias_t, g_b, bsz, seq):
    t = proj.shape[0]
    rows = seq // GRID_W
    kr = min(NA_ROWS_MAX, rows)
    rb = NA_ROWS_PER_STEP
    nb = rows // rb
    return pl.pallas_call(
        functools.partial(_na_body, rows=rows, kr=kr, rb=rb),
        out_shape=jax.ShapeDtypeStruct((t, B_W), BF16),
        grid=(bsz, nb),
        in_specs=[pl.BlockSpec((rb * GRID_W, PROJ_BLK), lambda b, r: (b * nb + r, 1)),
                  pl.BlockSpec((seq, PROJ_BLK), lambda b, r: (b, 2)),
                  pl.BlockSpec((seq, PROJ_BLK), lambda b, r: (b, 3)),
                  pl.BlockSpec((kr, B_HEADS * GRID_W, kr * GRID_W), lambda b, r: (0, 0, 0)),
                  pl.BlockSpec((1, B_W), lambda b, r: (0, 0))],
        out_specs=pl.BlockSpec((rb * GRID_W, B_W), lambda b, r: (b * nb + r, 0)),
        compiler_params=_cparams(("arbitrary", "arbitrary")),
        name="natten",
    )(proj, proj, proj, bias_t, g_b)


def _na_bias_table(rpb, rows):
    kr = min(NA_ROWS_MAX, rows)
    kc = NA_COLS
    col = jnp.arange(GRID_W)
    col_start = jnp.clip(col - kc // 2, 0, GRID_W - kc)
    col_mask = (col[None, :] >= col_start[:, None]) & (col[None, :] < col_start[:, None] + kc)
    col_off = jnp.clip(col[None, :] - col[:, None], -(kc - 1), kc - 1) + (NA_COLS - 1)
    rows_sel = jnp.stack([rpb[:, NA_ROWS_MAX - 1 - cs:NA_ROWS_MAX - 1 - cs + kr, :]
                          for cs in range(kr)], axis=0)
    pick = (col_off[None, :, :] == jnp.arange(2 * kc - 1)[:, None, None]).astype(F32)
    tbl = jnp.einsum('chjr,rqk->chqjk', rows_sel, pick, precision=lax.Precision.HIGHEST)
    tbl = jnp.where(col_mask[None, None, :, None, :], tbl * LOG2E, NEG)
    return tbl.reshape(kr, rpb.shape[0] * GRID_W, kr * GRID_W).astype(F32)


def _outproj_body(x_ref, ma_ref, mb_ref, wo_ref, gm_ref, sf_ref, cf_ref, wr_ref, br_ref,
                  x1_ref, hs_ref, pg_ref, tc_ref, to_ref):
    tm = SORT_TM
    tiles = range(SORT_TILES_PER_STEP)
    tt = tm * SORT_TILES_PER_STEP
    n_pos = TOP_K * tm
    mix = (jnp.dot(ma_ref[...], wo_ref[:A_Q_W, :], preferred_element_type=F32)
           + jnp.dot(mb_ref[...], wo_ref[A_Q_W:, :], preferred_element_type=F32))
    x1 = x_ref[...] + gm_ref[0] * mix
    x1_ref[...] = x1
    h = _rms(x1) * (1.0 + cf_ref[0]) + sf_ref[0]
    h_hi = h.astype(BF16)
    h_lo = (h - h_hi.astype(F32)).astype(BF16)
    w_r = wr_ref[...]
    w_hi = w_r.astype(BF16)
    w_lo = (w_r - w_hi.astype(F32)).astype(BF16)
    l_hi = jnp.dot(h_hi, jnp.concatenate([w_hi, w_lo], axis=1), preferred_element_type=F32)
    logits = (l_hi[:, :LANES] + l_hi[:, LANES:]
              + jnp.dot(h_lo, w_hi, preferred_element_type=F32) + br_ref[...])
    lane = lax.broadcasted_iota(I32, (tt, LANES), 1)
    work = logits
    vals, idxs = [], []
    for _ in range(TOP_K):
        m = jnp.max(work, axis=-1, keepdims=True)
        ik = jnp.min(jnp.where(work == m, lane, LANES), axis=-1, keepdims=True)
        vals.append(m)
        idxs.append(ik)
        work = jnp.where(lane == ik, -jnp.inf, work)
    es = [jnp.exp(v - vals[0]) for v in vals]
    den = es[0] + es[1] + es[2] + es[3]
    mh = jnp.zeros((tt, LANES), F32)
    for ik in idxs:
        mh = mh + (lane == ik).astype(F32)
    mhb = mh.astype(BF16)
    tri = (lax.broadcasted_iota(I32, (tm, tm), 0) > lax.broadcasted_iota(I32, (tm, tm), 1)).astype(BF16)
    lower = (lax.broadcasted_iota(I32, (LANES, LANES), 0)
             < lax.broadcasted_iota(I32, (LANES, LANES), 1)).astype(BF16)
    mh_t = [mhb[ti * tm:(ti + 1) * tm] for ti in tiles]
    earlier = [jnp.dot(tri, mh_t[ti], preferred_element_type=F32) for ti in tiles]
    below = [jnp.dot(mh_t[ti], lower, preferred_element_type=F32) for ti in tiles]
    toff = [jnp.sum(below[ti], axis=0, keepdims=True) for ti in tiles]
    cnt = [jnp.sum(mh[ti * tm:(ti + 1) * tm], axis=0, keepdims=True) for ti in tiles]
    posf = jnp.concatenate([earlier[ti] + toff[ti] for ti in tiles], axis=0)
    pg = jnp.zeros((tt, LANES), F32)
    for k in range(TOP_K):
        pos_k = jnp.sum(jnp.where(lane == idxs[k], posf, 0.0), axis=-1, keepdims=True)
        pg = jnp.where(lane == k, pos_k, pg)
        pg = jnp.where(lane == TOP_K + k, es[k] / den, pg)
    pg_ref[...] = pg
    pgt_all = pg.T[:ROW_WORDS]
    pgt = [pgt_all[:, ti * tm:(ti + 1) * tm] for ti in tiles]
    for ti in tiles:
        tc_ref[ti] = cnt[ti]
        to_ref[ti] = toff[ti]
    pos_iota = lax.broadcasted_iota(I32, (n_pos, tm), 0)
    hits = []
    for ti in tiles:
        hit = pos_iota == pgt[ti][0:1, :].astype(I32)
        for k in range(1, TOP_K):
            hit = hit | (pos_iota == pgt[ti][k:k + 1, :].astype(I32))
        hits.append(jnp.where(hit, 1.0, 0.0).astype(BF16))
    srt = [jnp.dot(hits[ti], h_hi[ti * tm:(ti + 1) * tm], preferred_element_type=F32) for ti in tiles]
    for ti in tiles:
        for c, words in enumerate(_rows_to_words(srt[ti])):
            hs_ref[pl.ds(ti * n_pos * ROW_WORDS + c, n_pos, stride=ROW_WORDS), :] = words


def _outproj(x2, mix_a, mix_b, w_out, gate_m, shift_f, scale_f, w_r, b_r, seq):
    t, d = x2.shape
    tm = SORT_TM
    nt = t // tm
    ts = SORT_TILES_PER_STEP
    tpb = seq // (ts * tm)
    n_pos = TOP_K * tm
    mod_spec = pl.BlockSpec((1, 1, d), lambda i: (i // tpb, 0, 0))
    return pl.pallas_call(
        _outproj_body,
        out_shape=(jax.ShapeDtypeStruct((t, d), F32),
                   jax.ShapeDtypeStruct((t * TOP_K * ROW_WORDS, LANES), I32),
                   jax.ShapeDtypeStruct((t, LANES), F32),
                   jax.ShapeDtypeStruct((nt, 1, LANES), F32),
                   jax.ShapeDtypeStruct((nt, 1, LANES), F32)),
        grid=(nt // ts,),
        in_specs=[pl.BlockSpec((ts * tm, d), lambda i: (i, 0)),
                  pl.BlockSpec((ts * tm, A_Q_W), lambda i: (i, 0)),
                  pl.BlockSpec((ts * tm, B_W), lambda i: (i, 0)),
                  pl.BlockSpec((A_Q_W + B_W, d), lambda i: (0, 0)),
                  mod_spec, mod_spec, mod_spec,
                  pl.BlockSpec((d, LANES), lambda i: (0, 0)),
                  pl.BlockSpec((1, LANES), lambda i: (0, 0))],
        out_specs=(pl.BlockSpec((ts * tm, d), lambda i: (i, 0)),
                   pl.BlockSpec((ts * n_pos * ROW_WORDS, LANES), lambda i: (i, 0)),
                   pl.BlockSpec((ts * tm, LANES), lambda i: (i, 0)),
                   pl.BlockSpec((ts, 1, LANES), lambda i: (i, 0, 0)),
                   pl.BlockSpec((ts, 1, LANES), lambda i: (i, 0, 0))),
        compiler_params=_cparams(("arbitrary",)),
        name="outproj",
    )(x2, mix_a, mix_b, w_out, gate_m, shift_f, scale_f, w_r, b_r)


def _segment_copies(seg_ref, n_seg, first, last, src_hbm, dst_buf, sem, tbl=0):
    def body(i, carry):
        size = pl.multiple_of(seg_ref[tbl, 0, 2 * n_seg + i], ROW_WORDS)

        @pl.when(size > 0)
        def _():
            src = pl.multiple_of(seg_ref[tbl, 0, i], ROW_WORDS)
            dst = pl.multiple_of(seg_ref[tbl, 0, n_seg + i], ROW_WORDS)
            pltpu.make_async_copy(src_hbm.at[pl.ds(src, size), :], dst_buf.at[pl.ds(dst, size), :], sem).start()
        return carry

    lax.fori_loop(first, last, body, 0)


def _experts_body(be_ref, vl_ref, seg0_ref, segn_ref, hs_hbm, wgu32_ref, bgu_ref, wd32_ref, bd_ref, o_ref,
                  xbuf, xs_ref, wgu_ref, wd_ref, gsem, *, n_blk, n_seg):
    bm = MOE_BM
    rt = ROW_WORDS
    d_ff = wd_ref.shape[0]
    j = pl.program_id(0)
    slot = j % 2
    other = 1 - slot
    valid = vl_ref[j]

    def issue(seg_ref, s):
        _segment_copies(seg_ref, n_seg, seg_ref[0, 0, 3 * n_seg], seg_ref[0, 0, 3 * n_seg + 1],
                        hs_hbm, xbuf.at[s], gsem.at[s])

    @pl.when(j == 0)
    def _():
        xbuf[...] = jnp.zeros_like(xbuf)
        issue(seg0_ref, 0)

    @pl.when((j == 0) | (be_ref[j] != be_ref[jnp.maximum(j - 1, 0)]))
    def _():
        for c in range(d_ff // LANES):
            wgu_ref[:, 2 * c * LANES:(2 * c + 1) * LANES] = (
                wgu32_ref[0, :, c * LANES:(c + 1) * LANES].astype(BF16))
            wgu_ref[:, (2 * c + 1) * LANES:(2 * c + 2) * LANES] = (
                wgu32_ref[0, :, d_ff + c * LANES:d_ff + (c + 1) * LANES].astype(BF16))
        wd_ref[...] = wd32_ref[0].astype(BF16)

    @pl.when(valid > 0)
    def _():
        rows = pl.multiple_of(valid * rt, rt)
        pltpu.make_async_copy(hs_hbm.at[pl.ds(0, rows), :], xbuf.at[slot, pl.ds(0, rows), :],
                              gsem.at[slot]).wait()

    @pl.when(j + 1 < n_blk)
    def _():
        issue(segn_ref, other)

    sub = MOE_SUB
    n_chunk = d_ff // (2 * LANES)
    half = sub // 2
    second = sub * rt
    n_half = (valid + half - 1) // half

    def assemble(r0, n):
        return _words_to_rows([xbuf[slot, pl.ds(r0 + c, n, stride=rt), :] for c in range(rt)])

    def act_chunk(xs, c):
        gu = jnp.dot(xs, wgu_ref[:, 4 * c * LANES:(4 * c + 4) * LANES], preferred_element_type=F32)
        outs = []
        for t in range(2):
            lt = 2 * c + t
            gate = gu[:, 2 * t * LANES:(2 * t + 1) * LANES] + bgu_ref[0, :, lt * LANES:(lt + 1) * LANES]
            up = (gu[:, (2 * t + 1) * LANES:(2 * t + 2) * LANES]
                  + bgu_ref[0, :, d_ff + lt * LANES:d_ff + (lt + 1) * LANES])
            gate = jnp.minimum(gate, SWIGLU_LIMIT)
            up = jnp.clip(up, -SWIGLU_LIMIT, SWIGLU_LIMIT)
            glu = gate * (0.5 + 0.5 * jnp.tanh((0.5 * SWIGLU_ALPHA) * gate))
            outs.append(((up + 1.0) * glu).astype(BF16))
        return jnp.concatenate(outs, axis=1)

    def down(acts):
        return jnp.dot(jnp.concatenate(acts, axis=1), wd_ref[...], preferred_element_type=F32) + bd_ref[0]

    def store_rows(r0, ys):
        for c, words in enumerate(_rows_to_words(ys)):
            o_ref[pl.ds(r0 + c, ys.shape[0], stride=rt), :] = words

    def zero_rows(r0, n):
        o_ref[pl.ds(r0, n * rt), :] = jnp.zeros((n * rt, LANES), I32)

    def single_pass(n):
        xs_ref[0, :n, :] = assemble(0, n)
        xs = xs_ref[0, :n, :]
        store_rows(0, down([act_chunk(xs, c) for c in range(n_chunk)]))
        zero_rows(n * rt, bm - n)

    def double_pass(n2):
        xs_ref[0] = assemble(0, sub)
        xa = xs_ref[0]
        acts = []
        for c in range(n_chunk):
            acts.append(act_chunk(xa, c))
            if c == n_chunk // 2:
                xs_ref[1, :n2, :] = assemble(second, n2)
        ys_a = down(acts)
        xb = xs_ref[1, :n2, :]
        acts = []
        for c in range(n_chunk):
            acts.append(act_chunk(xb, c))
            if c == 1:
                store_rows(0, ys_a)
        store_rows(second, down(acts))
        if n2 < sub:
            zero_rows(second + n2 * rt, sub - n2)

    @pl.when(n_half == 0)
    def _():
        zero_rows(0, bm)

    for nh, run in ((1, lambda: single_pass(half)), (2, lambda: single_pass(sub)),
                    (3, lambda: double_pass(half)), (4, lambda: double_pass(sub))):
        pl.when(n_half == nh)(run)


def _experts(blk_expert, blk_valid, seg, hs, wgu, bgu, wd, bd, n_seg):
    bm = MOE_BM
    rt = ROW_WORDS
    n_blk, _, seg_w = seg.shape
    d = wd.shape[2]
    d_ff = wd.shape[1]
    assert d == 2 * rt * LANES and hs.shape[1] == LANES and bm == 2 * MOE_SUB
    smem = functools.partial(pl.BlockSpec, (1, 1, seg_w), memory_space=pltpu.SMEM)
    last = n_blk - 1
    grid_spec = pltpu.PrefetchScalarGridSpec(
        num_scalar_prefetch=2, grid=(n_blk,),
        in_specs=[smem(lambda j, be, vl: (j, 0, 0)),
                  smem(lambda j, be, vl: (jnp.minimum(j + 1, last), 0, 0)),
                  pl.BlockSpec(memory_space=pl.ANY),
                  pl.BlockSpec((1, d, 2 * d_ff), lambda j, be, vl: (be[j], 0, 0)),
                  pl.BlockSpec((1, 1, 2 * d_ff), lambda j, be, vl: (be[j], 0, 0)),
                  pl.BlockSpec((1, d_ff, d), lambda j, be, vl: (be[j], 0, 0)),
                  pl.BlockSpec((1, 1, d), lambda j, be, vl: (be[j], 0, 0))],
        out_specs=pl.BlockSpec((bm * rt, LANES), lambda j, be, vl: (j, 0)),
        scratch_shapes=[pltpu.VMEM((2, bm * rt, LANES), I32), pltpu.VMEM((2, MOE_SUB, d), BF16),
                        pltpu.VMEM((d, 2 * d_ff), BF16), pltpu.VMEM((d_ff, d), BF16),
                        pltpu.SemaphoreType.DMA((2,))])
    return pl.pallas_call(
        functools.partial(_experts_body, n_blk=n_blk, n_seg=n_seg),
        out_shape=jax.ShapeDtypeStruct((n_blk * bm * rt, LANES), I32),
        grid_spec=grid_spec,
        compiler_params=_cparams(("arbitrary",), VMEM_LIMIT_EXPERTS),
        name="experts",
    )(blk_expert, blk_valid, seg, seg, hs, wgu, bgu, wd, bd)


def _combine_body(seg0_ref, segn_ref, x1_ref, pg_ref, gf_ref, gfin_ref, ys_hbm, o_ref,
                  ybuf, sem, *, n_steps):
    tm = SORT_TM
    tiles = range(SORT_TILES_PER_STEP)
    n_pos = TOP_K * tm
    rt = ROW_WORDS
    i = pl.program_id(0)
    slot = i % 2
    other = 1 - slot

    def issue(seg_ref, s):
        for ti in tiles:
            _segment_copies(seg_ref, N_EXPERTS, 0, N_EXPERTS, ys_hbm, ybuf.at[s, ti], sem.at[s], tbl=ti)

    @pl.when(i == 0)
    def _():
        issue(seg0_ref, 0)

    for ti in tiles:
        pltpu.make_async_copy(ys_hbm.at[pl.ds(0, n_pos * rt), :], ybuf.at[slot, ti], sem.at[slot]).wait()

    @pl.when(i + 1 < n_steps)
    def _():
        issue(segn_ref, other)

    lane_pos = lax.broadcasted_iota(I32, (tm, n_pos), 1)
    ws = []
    for ti in tiles:
        pg = pg_ref[ti * tm:(ti + 1) * tm, :]
        w = jnp.zeros((tm, n_pos), F32)
        for k in range(TOP_K):
            w = jnp.where(lane_pos == pg[:, k:k + 1].astype(I32), pg[:, TOP_K + k:TOP_K + k + 1], w)
        ws.append(w)
    w_his = [w.astype(BF16) for w in ws]
    w_los = [(ws[ti] - w_his[ti].astype(F32)).astype(BF16) for ti in tiles]
    ysbs = [_words_to_rows([ybuf[slot, ti, pl.ds(c, n_pos, stride=rt), :] for c in range(rt)])
            for ti in tiles]
    ys = [jnp.dot(w_his[ti], ysbs[ti], preferred_element_type=F32)
          + jnp.dot(w_los[ti], ysbs[ti], preferred_element_type=F32) for ti in tiles]
    for ti in tiles:
        x2 = x1_ref[ti * tm:(ti + 1) * tm, :] + gf_ref[0] * ys[ti]
        o_ref[ti * tm:(ti + 1) * tm, :] = _rms(x2) * gfin_ref[...]


def _combine(seg, x1, pg, gate_f, g_final, ys, seq):
    t, d = x1.shape
    tm = SORT_TM
    ts = SORT_TILES_PER_STEP
    tpb = seq // (ts * tm)
    n_steps = t // (ts * tm)
    n_pos = TOP_K * tm
    seg_w = seg.shape[2]
    smem = functools.partial(pl.BlockSpec, (ts, 1, seg_w), memory_space=pltpu.SMEM)
    return pl.pallas_call(
        functools.partial(_combine_body, n_steps=n_steps),
        out_shape=jax.ShapeDtypeStruct((t, d), F32),
        grid=(n_steps,),
        in_specs=[smem(lambda i: (i, 0, 0)),
                  smem(lambda i: (jnp.minimum(i + 1, n_steps - 1), 0, 0)),
                  pl.BlockSpec((ts * tm, d), lambda i: (i, 0)),
                  pl.BlockSpec((ts * tm, LANES), lambda i: (i, 0)),
                  pl.BlockSpec((1, 1, d), lambda i: (i // tpb, 0, 0)),
                  pl.BlockSpec((1, d), lambda i: (0, 0)),
                  pl.BlockSpec(memory_space=pl.ANY)],
        out_specs=pl.BlockSpec((ts * tm, d), lambda i: (i, 0)),
        scratch_shapes=[pltpu.VMEM((2, ts, n_pos * ROW_WORDS, LANES), I32), pltpu.SemaphoreType.DMA((2,))],
        compiler_params=_cparams(("arbitrary",)),
        name="combine",
    )(seg, seg, x1, pg, gate_f, g_final, ys)


def _rope_tables(seq):
    half = ROT_DIMS // 2
    pos = jnp.arange(seq, dtype=F32)
    inv_freq = ROPE_THETA ** (-jnp.arange(0, ROT_DIMS, 2, dtype=F32) / ROT_DIMS)
    ang = pos[:, None] * inv_freq[None, :]
    cos, sin = jnp.cos(ang), jnp.sin(ang)
    ones = jnp.ones((seq, HEAD_DIM - ROT_DIMS), F32)
    zeros = jnp.zeros((seq, HEAD_DIM - ROT_DIMS), F32)
    zh = jnp.zeros((seq, half), F32)
    rep = LANES // HEAD_DIM
    cos_t = jnp.tile(jnp.concatenate([cos, cos, ones], axis=1), (1, rep))
    sa_t = jnp.tile(jnp.concatenate([-sin, zh, zeros], axis=1), (1, rep))
    sb_t = jnp.tile(jnp.concatenate([zh, sin, zeros], axis=1), (1, rep))
    return cos_t, sa_t, sb_t


def _arrange_in_proj(w_in):
    scale = HEAD_DIM ** -0.5 * LOG2E
    o = 0
    wqa = w_in[:, o:o + A_Q_W] * scale; o += A_Q_W
    wka = w_in[:, o:o + A_KV_W]; o += A_KV_W
    wva = w_in[:, o:o + A_KV_W]; o += A_KV_W
    wqb = w_in[:, o:o + B_W] * scale; o += B_W
    wkb = w_in[:, o:o + B_W]; o += B_W
    wvb = w_in[:, o:o + B_W]
    return jnp.concatenate([wqa, wqb, wkb, wvb, wka, wva], axis=1).astype(BF16)


def _round_up(n, m):
    return (n + m - 1) // m * m


def _moe_tables(tile_cnt, tile_off, n_blk):
    bm = MOE_BM
    nt = tile_cnt.shape[0]
    n_pos = TOP_K * SORT_TM
    cum = jnp.cumsum(tile_cnt, axis=0) - tile_cnt
    counts = jnp.sum(tile_cnt, axis=0)
    padded = ((counts + bm - 1) // bm) * bm
    pends = jnp.cumsum(padded)
    pstarts = pends - padded
    blk_first = jnp.arange(n_blk, dtype=I32) * bm
    blk_expert = jnp.minimum(
        jnp.sum((pends[None, :] <= blk_first[:, None]).astype(I32), axis=1), N_EXPERTS - 1)
    own = blk_expert[:, None] == jnp.arange(N_EXPERTS, dtype=I32)[None, :]

    def of_block_expert(per_expert):
        return jnp.sum(jnp.where(own[:, None, :], per_expert[None], 0), axis=-1)

    r0 = blk_first - of_block_expert(pstarts[None, :])[:, 0]
    blk_valid = jnp.clip(of_block_expert(counts[None, :])[:, 0] - r0, 0, bm)
    cum_b = of_block_expert(cum)
    cnt_b = of_block_expert(tile_cnt)
    off_b = of_block_expert(tile_off)
    lo = jnp.maximum(r0[:, None], cum_b)
    hi = jnp.minimum(r0[:, None] + bm, cum_b + cnt_b)
    size = jnp.maximum(hi - lo, 0)
    tile_ids = jnp.arange(nt, dtype=I32)[None, :]
    src = tile_ids * n_pos + off_b + (lo - cum_b)
    dst = jnp.where(size > 0, lo - r0[:, None], 0)
    first = jnp.min(jnp.where(size > 0, tile_ids, nt), axis=1)
    last = jnp.max(jnp.where(size > 0, tile_ids + 1, 0), axis=1)
    first = jnp.minimum(first, last)
    seg_w = _round_up(3 * nt + 2, LANES)
    exp_seg = jnp.concatenate(
        [src * ROW_WORDS, dst * ROW_WORDS, size * ROW_WORDS, first[:, None], last[:, None],
         jnp.zeros((n_blk, seg_w - 3 * nt - 2), I32)], axis=1).reshape(n_blk, 1, seg_w)
    comb_seg = jnp.concatenate(
        [(pstarts[None, :] + cum) * ROW_WORDS, tile_off * ROW_WORDS, tile_cnt * ROW_WORDS,
         jnp.zeros((nt, LANES - 3 * N_EXPERTS), I32)], axis=1).reshape(nt, 1, LANES)
    return blk_expert, blk_valid, exp_seg, comb_seg


def kernel(x, c, w_ada, b_ada, w_in, sink, rpb, g_out_a, g_out_b, w_out, w_router, b_router,
           w_gate_up, b_gate_up, w_down, b_down, g_final):
    bsz, seq, d = x.shape
    t = bsz * seq
    depth = w_ada.shape[0]
    assert depth == 1, "the final norm is fused into the layer combine"
    bm = MOE_BM
    n_blk = (t * TOP_K) // bm + N_EXPERTS
    nt = t // SORT_TM

    cos_t, sa_t, sb_t = _rope_tables(seq)
    x2 = x.reshape(t, d)
    for l in range(depth):
        mod = _ada(c, w_ada[l], b_ada[l][None, :])
        shift_m, scale_m, gate_m, shift_f, scale_f, gate_f = [
            m.reshape(bsz, 1, d) for m in jnp.split(mod, 6, axis=-1)]

        proj = _inproj(x2, shift_m, scale_m, _arrange_in_proj(w_in[l]), cos_t, sa_t, sb_t, seq)
        mix_a = _swa(proj, sink[l], g_out_a[l][None, :], bsz, seq)
        mix_b = _natten(proj, _na_bias_table(rpb[l], seq // GRID_W), g_out_b[l][None, :], bsz, seq)

        w_r = jnp.pad(w_router[l], ((0, 0), (0, LANES - N_EXPERTS)))
        b_r = jnp.pad(b_router[l], (0, LANES - N_EXPERTS), constant_values=NEG)[None, :]
        x1, hs, pg, tile_cnt, tile_off = _outproj(
            x2, mix_a, mix_b, w_out[l].astype(BF16), gate_m, shift_f, scale_f, w_r, b_r, seq)

        blk_expert, blk_valid, exp_seg, comb_seg = _moe_tables(
            tile_cnt[:, 0, :N_EXPERTS].astype(I32), tile_off[:, 0, :N_EXPERTS].astype(I32), n_blk)
        ys = _experts(blk_expert, blk_valid, exp_seg, hs,
                      w_gate_up[l], b_gate_up[l][:, None, :],
                      w_down[l], b_down[l][:, None, :], nt)
        x2 = _combine(comb_seg, x1, pg, gate_f, g_final[None, :], ys, seq)
    return x2.reshape(bsz, seq, d)
```
